```python
import math
import jax, jax.numpy as jnp
from jax import lax
import numpy as np

D_MODEL = 1024
BATCH = 8
SEQ = 8192
DEPTH = 4

N_MIXERS = 2
MIX_WIDTH = 2 * D_MODEL
EPS = 1e-6

MEM_LEN = 256
MEM_HEADS = 4
MEM_WIDTH = MIX_WIDTH // 4
MEM_HEAD_DIM = MEM_WIDTH // MEM_HEADS

SSD_HEAD_DIM = 64
SSD_INNER = MIX_WIDTH - MEM_WIDTH
SSD_HEADS = SSD_INNER // SSD_HEAD_DIM
SSD_GROUPS = 4
SSD_HEADS_PER_GROUP = SSD_HEADS // SSD_GROUPS
SSD_STATE = 128
SSD_CONV = 4
SSD_CHUNK = 128
SSD_CONV_DIM = SSD_INNER + 2 * SSD_GROUPS * SSD_STATE
SSD_IN = SSD_INNER + SSD_CONV_DIM + SSD_HEADS + MEM_WIDTH

SB_HEAD_DIM = 128
SB_WIDTH = MIX_WIDTH - MEM_WIDTH
SB_HEADS = SB_WIDTH // SB_HEAD_DIM
SB_BLOCK = 128
SB_IN = 3 * SB_WIDTH + MEM_WIDTH

FFN_HIDDEN = ((-(-8 * D_MODEL // 3) + 255) // 256) * 256

N_SSD_LAYERS = (DEPTH + 1) // 2
N_SB_LAYERS = DEPTH // 2

kernel_name = "hybrid_ssd_stickbreak_memxattn_trunk"


def _rmsnorm(x, g):
    xf = x.astype(jnp.float32)
    y = xf * lax.rsqrt(jnp.mean(xf * xf, axis=-1, keepdims=True) + EPS)
    return (y * g.astype(jnp.float32)).astype(x.dtype)


def _swiglu(h, w_gate_up, w_down):
    g, u = jnp.split(h @ w_gate_up, 2, axis=-1)
    return (jax.nn.silu(g) * u) @ w_down


def _memory_attention(q, mem_n, w_kv, q_g, k_g):
    b, s = q.shape[:2]
    m = mem_n.shape[1]
    q = _rmsnorm(q.reshape(b, s, MEM_HEADS, MEM_HEAD_DIM), q_g)
    k, v = jnp.split(mem_n @ w_kv, 2, axis=-1)
    k = _rmsnorm(k.reshape(b, m, MEM_HEADS, MEM_HEAD_DIM), k_g)
    v = v.reshape(b, m, MEM_HEADS, MEM_HEAD_DIM)
    scores = jnp.einsum("bshd,bmhd->bhsm", q, k).astype(jnp.float32) * (MEM_HEAD_DIM ** -0.5)
    p = jax.nn.softmax(scores, axis=-1).astype(v.dtype)
    o = jnp.einsum("bhsm,bmhd->bshd", p, v)
    return o.reshape(b, s, MEM_WIDTH)


def _causal_depthwise_conv(u, w, bias):
    c = u.shape[-1]
    out = lax.conv_general_dilated(
        u, w[:, None, :].astype(u.dtype), window_strides=(1,),
        padding=[(SSD_CONV - 1, 0)], dimension_numbers=("NWC", "WIO", "NWC"),
        feature_group_count=c)
    return out + bias


def _ssd_chunked(xs, dt, a, bm, cm):
    b, s = xs.shape[:2]
    nc = s // SSD_CHUNK
    L, G, R, P, N = SSD_CHUNK, SSD_GROUPS, SSD_HEADS_PER_GROUP, SSD_HEAD_DIM, SSD_STATE
    f32 = jnp.float32
    xdt = (xs.astype(f32) * dt[..., None]).reshape(b, nc, L, G, R, P)
    da = (dt * a).reshape(b, nc, L, G, R)
    bm = bm.astype(f32).reshape(b, nc, L, G, N)
    cm = cm.astype(f32).reshape(b, nc, L, G, N)
    cs = jnp.cumsum(da, axis=2)
    causal = jnp.tril(jnp.ones((L, L), dtype=bool))[:, :, None, None]
    seg = jnp.where(causal, cs[:, :, :, None] - cs[:, :, None, :], -jnp.inf)
    w_intra = jnp.einsum("bclgn,bcsgn->bclsg", cm, bm)[..., None] * jnp.exp(seg)
    y_diag = jnp.einsum("bclsgr,bcsgrp->bclgrp", w_intra, xdt)
    decay_to_end = jnp.exp(cs[:, :, -1:] - cs)
    chunk_states = jnp.einsum("bclgn,bclgr,bclgrp->bcgrpn", bm, decay_to_end, xdt)
    chunk_decay = jnp.exp(cs[:, :, -1])

    def step(state, inp):
        st, dec = inp
        return state * dec[..., None, None] + st, state

    h0 = jnp.zeros((b, G, R, P, N), f32)
    _, h_in = lax.scan(step, h0, (jnp.moveaxis(chunk_states, 1, 0), jnp.moveaxis(chunk_decay, 1, 0)))
    h_in = jnp.moveaxis(h_in, 0, 1)
    y_off = jnp.einsum("bclgn,bcgrpn->bclgrp", cm, h_in) * jnp.exp(cs)[..., None]
    return (y_diag + y_off).reshape(b, s, SSD_HEADS, P)


def _ssd_mix(h, mem_n, w_in, conv_w, conv_b, dt_bias, a_log, d_skip, norm_g, w_out,
             mem_w_kv, mq_g, mk_g):
    b, s, _ = h.shape
    proj = h @ w_in
    z, xbc, dt_raw, q_mem = jnp.split(
        proj, [SSD_INNER, SSD_INNER + SSD_CONV_DIM, SSD_INNER + SSD_CONV_DIM + SSD_HEADS], axis=-1)
    xbc = jax.nn.silu(_causal_depthwise_conv(xbc, conv_w, conv_b))
    xs, bm, cm = jnp.split(xbc, [SSD_INNER, SSD_INNER + SSD_GROUPS * SSD_STATE], axis=-1)
    dt = jax.nn.softplus(dt_raw.astype(jnp.float32) + dt_bias.astype(jnp.float32))
    a = -jnp.exp(a_log.astype(jnp.float32))
    xs = xs.reshape(b, s, SSD_HEADS, SSD_HEAD_DIM)
    y = _ssd_chunked(xs, dt, a,
                     bm.reshape(b, s, SSD_GROUPS, SSD_STATE),
                     cm.reshape(b, s, SSD_GROUPS, SSD_STATE))
    y = y + d_skip.astype(jnp.float32)[:, None] * xs.astype(jnp.float32)
    y = y.reshape(b, s, SSD_INNER) * jax.nn.silu(z.astype(jnp.float32))
    gs = SSD_INNER // SSD_GROUPS
    y = _rmsnorm(y.reshape(b, s, SSD_GROUPS, gs), norm_g.reshape(SSD_GROUPS, gs))
    y = y.reshape(b, s, SSD_INNER).astype(h.dtype)
    o_mem = _memory_attention(q_mem, mem_n, mem_w_kv, mq_g, mk_g)
    return jnp.concatenate([y, o_mem], axis=-1) @ w_out


def _stick_breaking(q, k, v):
    seq = q.shape[1]
    scale = SB_HEAD_DIM ** -0.5
    outs = []
    for blk in range(seq // SB_BLOCK):
        t0 = blk * SB_BLOCK
        end = t0 + SB_BLOCK
        qb, kb, vb = q[:, t0:end], k[:, :end], v[:, :end]
        z = jnp.einsum("bthd,bshd->bhts", qb, kb).astype(jnp.float32) * scale
        causal = jnp.arange(end)[None, :] < jnp.arange(t0, end)[:, None]
        log_1m = jnp.where(causal, jax.nn.log_sigmoid(-z), 0.0)
        log_survive = lax.cumsum(log_1m, axis=3, reverse=True) - log_1m
        w = jnp.where(causal, jnp.exp(jax.nn.log_sigmoid(z) + log_survive), 0.0)
        outs.append(jnp.einsum("bhts,bshd->bthd", w.astype(vb.dtype), vb))
    return jnp.concatenate(outs, axis=1)


def _sb_mix(h, mem_n, w_in, q_g, k_g, w_out, mem_w_kv, mq_g, mk_g):
    b, s, _ = h.shape
    q, k, v, q_mem = jnp.split(h @ w_in, [SB_WIDTH, 2 * SB_WIDTH, 3 * SB_WIDTH], axis=-1)
    shp = (b, s, SB_HEADS, SB_HEAD_DIM)
    q = _rmsnorm(q.reshape(shp), q_g)
    k = _rmsnorm(k.reshape(shp), k_g)
    o = _stick_breaking(q, k, v.reshape(shp)).reshape(b, s, SB_WIDTH)
    o_mem = _memory_attention(q_mem, mem_n, mem_w_kv, mq_g, mk_g)
    return jnp.concatenate([o, o_mem], axis=-1) @ w_out


def _fwd_setup_inputs(seed: int = 0) -> dict:
    key = jax.random.key(seed)
    ks = jax.random.split(key, 24)
    nrm = jax.random.normal
    res_scale = 1.0 / math.sqrt(2 * DEPTH)
    dt = jnp.exp(jax.random.uniform(ks[10], (N_SSD_LAYERS, SSD_HEADS))
                 * (math.log(0.1) - math.log(0.001)) + math.log(0.001))
    return {
        "x": nrm(ks[0], (BATCH, SEQ, D_MODEL), jnp.float32),
        "mem": nrm(ks[1], (BATCH, MEM_LEN, D_MODEL), jnp.float32),
        "mix_norm_g": 1.0 + 0.02 * nrm(ks[2], (DEPTH, D_MODEL)),
        "ffn_norm_g": 1.0 + 0.02 * nrm(ks[3], (DEPTH, D_MODEL)),
        "mem_norm_g": 1.0 + 0.02 * nrm(ks[4], (D_MODEL,)),
        "mem_w_kv": nrm(ks[5], (DEPTH, D_MODEL, 2 * MEM_WIDTH)) * D_MODEL ** -0.5,
        "mem_q_norm_g": 1.0 + 0.02 * nrm(ks[6], (DEPTH, MEM_HEAD_DIM)),
        "mem_k_norm_g": 1.0 + 0.02 * nrm(ks[7], (DEPTH, MEM_HEAD_DIM)),
        "ssd_w_in": nrm(ks[8], (N_SSD_LAYERS, D_MODEL, SSD_IN)) * D_MODEL ** -0.5,
        "ssd_conv_w": nrm(ks[9], (N_SSD_LAYERS, SSD_CONV, SSD_CONV_DIM)) * SSD_CONV ** -0.5,
        "ssd_conv_b": 0.02 * nrm(ks[11], (N_SSD_LAYERS, SSD_CONV_DIM)),
        "ssd_dt_bias": dt + jnp.log(-jnp.expm1(-dt)),
        "ssd_a_log": jnp.log(jax.random.uniform(ks[12], (N_SSD_LAYERS, SSD_HEADS), minval=1.0, maxval=16.0)),
        "ssd_d": 1.0 + 0.02 * nrm(ks[13], (N_SSD_LAYERS, SSD_HEADS)),
        "ssd_norm_g": 1.0 + 0.02 * nrm(ks[14], (N_SSD_LAYERS, SSD_INNER)),
        "ssd_w_out": nrm(ks[15], (N_SSD_LAYERS, MIX_WIDTH, D_MODEL)) * MIX_WIDTH ** -0.5 * res_scale,
        "sb_w_in": nrm(ks[16], (N_SB_LAYERS, D_MODEL, SB_IN)) * D_MODEL ** -0.5,
        "sb_q_norm_g": 1.0 + 0.02 * nrm(ks[17], (N_SB_LAYERS, SB_HEAD_DIM)),
        "sb_k_norm_g": 1.0 + 0.02 * nrm(ks[18], (N_SB_LAYERS, SB_HEAD_DIM)),
        "sb_w_out": nrm(ks[19], (N_SB_LAYERS, MIX_WIDTH, D_MODEL)) * MIX_WIDTH ** -0.5 * res_scale,
        "ffn_w_gate_up": nrm(ks[20], (DEPTH, D_MODEL, 2 * FFN_HIDDEN)) * D_MODEL ** -0.5,
        "ffn_w_down": nrm(ks[21], (DEPTH, FFN_HIDDEN, D_MODEL)) * FFN_HIDDEN ** -0.5 * res_scale,
    }


def _fwd_reference(x, mem, mix_norm_g, ffn_norm_g, mem_norm_g, mem_w_kv, mem_q_norm_g, mem_k_norm_g,
              ssd_w_in, ssd_conv_w, ssd_conv_b, ssd_dt_bias, ssd_a_log, ssd_d, ssd_norm_g, ssd_w_out,
              sb_w_in, sb_q_norm_g, sb_k_norm_g, sb_w_out, ffn_w_gate_up, ffn_w_down):
    mem_n = _rmsnorm(mem, mem_norm_g)
    for i in range(DEPTH):
        j = i // N_MIXERS
        h = _rmsnorm(x, mix_norm_g[i])
        if i % N_MIXERS == 0:
            mix = _ssd_mix(h, mem_n, ssd_w_in[j], ssd_conv_w[j], ssd_conv_b[j], ssd_dt_bias[j],
                           ssd_a_log[j], ssd_d[j], ssd_norm_g[j], ssd_w_out[j],
                           mem_w_kv[i], mem_q_norm_g[i], mem_k_norm_g[i])
        else:
            mix = _sb_mix(h, mem_n, sb_w_in[j], sb_q_norm_g[j], sb_k_norm_g[j], sb_w_out[j],
                          mem_w_kv[i], mem_q_norm_g[i], mem_k_norm_g[i])
        x = x + mix
        x = x + _swiglu(_rmsnorm(x, ffn_norm_g[i]), ffn_w_gate_up[i], ffn_w_down[i])
    return x


import jax as _jax
import jax.numpy as _jnp

TWIN_FORMAT = 'train_step'
FWD_PARAMS = ['x', 'mem', 'mix_norm_g', 'ffn_norm_g', 'mem_norm_g', 'mem_w_kv', 'mem_q_norm_g', 'mem_k_norm_g', 'ssd_w_in', 'ssd_conv_w', 'ssd_conv_b', 'ssd_dt_bias', 'ssd_a_log', 'ssd_d', 'ssd_norm_g', 'ssd_w_out', 'sb_w_in', 'sb_q_norm_g', 'sb_k_norm_g', 'sb_w_out', 'ffn_w_gate_up', 'ffn_w_down']
TWIN_WEIGHTS = ['mix_norm_g', 'ffn_norm_g', 'mem_norm_g', 'mem_w_kv', 'mem_q_norm_g', 'mem_k_norm_g', 'ssd_w_in', 'ssd_conv_w', 'ssd_conv_b', 'ssd_dt_bias', 'ssd_a_log', 'ssd_d', 'ssd_norm_g', 'ssd_w_out', 'sb_w_in', 'sb_q_norm_g', 'sb_k_norm_g', 'sb_w_out', 'ffn_w_gate_up', 'ffn_w_down']
TWIN_DIFF_INPUT = 'x'
TWIN_INPUTS = ['x', 'mem', 'mix_norm_g', 'ffn_norm_g', 'mem_norm_g', 'mem_w_kv', 'mem_q_norm_g', 'mem_k_norm_g', 'ssd_w_in', 'ssd_conv_w', 'ssd_conv_b', 'ssd_dt_bias', 'ssd_a_log', 'ssd_d', 'ssd_norm_g', 'ssd_w_out', 'sb_w_in', 'sb_q_norm_g', 'sb_k_norm_g', 'sb_w_out', 'ffn_w_gate_up', 'ffn_w_down', 'loss_target', 'm_mix_norm_g', 'm_ffn_norm_g', 'm_mem_norm_g', 'm_mem_w_kv', 'm_mem_q_norm_g', 'm_mem_k_norm_g', 'm_ssd_w_in', 'm_ssd_conv_w', 'm_ssd_conv_b', 'm_ssd_dt_bias', 'm_ssd_a_log', 'm_ssd_d', 'm_ssd_norm_g', 'm_ssd_w_out', 'm_sb_w_in', 'm_sb_q_norm_g', 'm_sb_k_norm_g', 'm_sb_w_out', 'm_ffn_w_gate_up', 'm_ffn_w_down', 'v_mix_norm_g', 'v_ffn_norm_g', 'v_mem_norm_g', 'v_mem_w_kv', 'v_mem_q_norm_g', 'v_mem_k_norm_g', 'v_ssd_w_in', 'v_ssd_conv_w', 'v_ssd_conv_b', 'v_ssd_dt_bias', 'v_ssd_a_log', 'v_ssd_d', 'v_ssd_norm_g', 'v_ssd_w_out', 'v_sb_w_in', 'v_sb_q_norm_g', 'v_sb_k_norm_g', 'v_sb_w_out', 'v_ffn_w_gate_up', 'v_ffn_w_down']
TWIN_OUTPUTS = ['loss', 'grad_x', 'grad_mix_norm_g', 'grad_ffn_norm_g', 'grad_mem_norm_g', 'grad_mem_w_kv', 'grad_mem_q_norm_g', 'grad_mem_k_norm_g', 'grad_ssd_w_in', 'grad_ssd_conv_w', 'grad_ssd_conv_b', 'grad_ssd_dt_bias', 'grad_ssd_a_log', 'grad_ssd_d', 'grad_ssd_norm_g', 'grad_ssd_w_out', 'grad_sb_w_in', 'grad_sb_q_norm_g', 'grad_sb_k_norm_g', 'grad_sb_w_out', 'grad_ffn_w_gate_up', 'grad_ffn_w_down', 'delta_mix_norm_g', 'delta_ffn_norm_g', 'delta_mem_norm_g', 'delta_mem_w_kv', 'delta_mem_q_norm_g', 'delta_mem_k_norm_g', 'delta_ssd_w_in', 'delta_ssd_conv_w', 'delta_ssd_conv_b', 'delta_ssd_dt_bias', 'delta_ssd_a_log', 'delta_ssd_d', 'delta_ssd_norm_g', 'delta_ssd_w_out', 'delta_sb_w_in', 'delta_sb_q_norm_g', 'delta_sb_k_norm_g', 'delta_sb_w_out', 'delta_ffn_w_gate_up', 'delta_ffn_w_down', 'new_m_mix_norm_g', 'new_m_ffn_norm_g', 'new_m_mem_norm_g', 'new_m_mem_w_kv', 'new_m_mem_q_norm_g', 'new_m_mem_k_norm_g', 'new_m_ssd_w_in', 'new_m_ssd_conv_w', 'new_m_ssd_conv_b', 'new_m_ssd_dt_bias', 'new_m_ssd_a_log', 'new_m_ssd_d', 'new_m_ssd_norm_g', 'new_m_ssd_w_out', 'new_m_sb_w_in', 'new_m_sb_q_norm_g', 'new_m_sb_k_norm_g', 'new_m_sb_w_out', 'new_m_ffn_w_gate_up', 'new_m_ffn_w_down', 'new_v_mix_norm_g', 'new_v_ffn_norm_g', 'new_v_mem_norm_g', 'new_v_mem_w_kv', 'new_v_mem_q_norm_g', 'new_v_mem_k_norm_g', 'new_v_ssd_w_in', 'new_v_ssd_conv_w', 'new_v_ssd_conv_b', 'new_v_ssd_dt_bias', 'new_v_ssd_a_log', 'new_v_ssd_d', 'new_v_ssd_norm_g', 'new_v_ssd_w_out', 'new_v_sb_w_in', 'new_v_sb_q_norm_g', 'new_v_sb_k_norm_g', 'new_v_sb_w_out', 'new_v_ffn_w_gate_up', 'new_v_ffn_w_down']
TWIN_LEAF_KINDS = {'loss': 'loss', 'grad_x': 'grad_x', 'grad_mix_norm_g': 'grad_w', 'grad_ffn_norm_g': 'grad_w', 'grad_mem_norm_g': 'grad_w', 'grad_mem_w_kv': 'grad_w', 'grad_mem_q_norm_g': 'grad_w', 'grad_mem_k_norm_g': 'grad_w', 'grad_ssd_w_in': 'grad_w', 'grad_ssd_conv_w': 'grad_w', 'grad_ssd_conv_b': 'grad_w', 'grad_ssd_dt_bias': 'grad_w', 'grad_ssd_a_log': 'grad_w', 'grad_ssd_d': 'grad_w', 'grad_ssd_norm_g': 'grad_w', 'grad_ssd_w_out': 'grad_w', 'grad_sb_w_in': 'grad_w', 'grad_sb_q_norm_g': 'grad_w', 'grad_sb_k_norm_g': 'grad_w', 'grad_sb_w_out': 'grad_w', 'grad_ffn_w_gate_up': 'grad_w', 'grad_ffn_w_down': 'grad_w', 'delta_mix_norm_g': 'delta_w', 'delta_ffn_norm_g': 'delta_w', 'delta_mem_norm_g': 'delta_w', 'delta_mem_w_kv': 'delta_w', 'delta_mem_q_norm_g': 'delta_w', 'delta_mem_k_norm_g': 'delta_w', 'delta_ssd_w_in': 'delta_w', 'delta_ssd_conv_w': 'delta_w', 'delta_ssd_conv_b': 'delta_w', 'delta_ssd_dt_bias': 'delta_w', 'delta_ssd_a_log': 'delta_w', 'delta_ssd_d': 'delta_w', 'delta_ssd_norm_g': 'delta_w', 'delta_ssd_w_out': 'delta_w', 'delta_sb_w_in': 'delta_w', 'delta_sb_q_norm_g': 'delta_w', 'delta_sb_k_norm_g': 'delta_w', 'delta_sb_w_out': 'delta_w', 'delta_ffn_w_gate_up': 'delta_w', 'delta_ffn_w_down': 'delta_w', 'new_m_mix_norm_g': 'new_m', 'new_m_ffn_norm_g': 'new_m', 'new_m_mem_norm_g': 'new_m', 'new_m_mem_w_kv': 'new_m', 'new_m_mem_q_norm_g': 'new_m', 'new_m_mem_k_norm_g': 'new_m', 'new_m_ssd_w_in': 'new_m', 'new_m_ssd_conv_w': 'new_m', 'new_m_ssd_conv_b': 'new_m', 'new_m_ssd_dt_bias': 'new_m', 'new_m_ssd_a_log': 'new_m', 'new_m_ssd_d': 'new_m', 'new_m_ssd_norm_g': 'new_m', 'new_m_ssd_w_out': 'new_m', 'new_m_sb_w_in': 'new_m', 'new_m_sb_q_norm_g': 'new_m', 'new_m_sb_k_norm_g': 'new_m', 'new_m_sb_w_out': 'new_m', 'new_m_ffn_w_gate_up': 'new_m', 'new_m_ffn_w_down': 'new_m', 'new_v_mix_norm_g': 'new_v', 'new_v_ffn_norm_g': 'new_v', 'new_v_mem_norm_g': 'new_v', 'new_v_mem_w_kv': 'new_v', 'new_v_mem_q_norm_g': 'new_v', 'new_v_mem_k_norm_g': 'new_v', 'new_v_ssd_w_in': 'new_v', 'new_v_ssd_conv_w': 'new_v', 'new_v_ssd_conv_b': 'new_v', 'new_v_ssd_dt_bias': 'new_v', 'new_v_ssd_a_log': 'new_v', 'new_v_ssd_d': 'new_v', 'new_v_ssd_norm_g': 'new_v', 'new_v_ssd_w_out': 'new_v', 'new_v_sb_w_in': 'new_v', 'new_v_sb_q_norm_g': 'new_v', 'new_v_sb_k_norm_g': 'new_v', 'new_v_sb_w_out': 'new_v', 'new_v_ffn_w_gate_up': 'new_v', 'new_v_ffn_w_down': 'new_v'}


def _forward(args):
    return _fwd_reference(*[args[k] for k in FWD_PARAMS])


def _output_shape():
    def fwd():
        inp = _fwd_setup_inputs(0)
        return _fwd_reference(*[inp[k] for k in FWD_PARAMS])
    out = _jax.eval_shape(fwd)
    return out.shape, out.dtype

N_MICROBATCH = 1
ADAM_LR = 0.001
ADAM_B1 = 0.9
ADAM_B2 = 0.999
ADAM_EPS = 1e-08
ADAM_WD = 0.01
ADAM_STEP = 10
PER_EXAMPLE_BATCH_AXIS = {'x': 0, 'mem': 0, 'loss_target': 0}
SHARED_INPUTS = []
_WEIGHT_DTYPES = {'mix_norm_g': _jnp.float32, 'ffn_norm_g': _jnp.float32, 'mem_norm_g': _jnp.float32, 'mem_w_kv': _jnp.float32, 'mem_q_norm_g': _jnp.float32, 'mem_k_norm_g': _jnp.float32, 'ssd_w_in': _jnp.float32, 'ssd_conv_w': _jnp.float32, 'ssd_conv_b': _jnp.float32, 'ssd_dt_bias': _jnp.float32, 'ssd_a_log': _jnp.float32, 'ssd_d': _jnp.float32, 'ssd_norm_g': _jnp.float32, 'ssd_w_out': _jnp.float32, 'sb_w_in': _jnp.float32, 'sb_q_norm_g': _jnp.float32, 'sb_k_norm_g': _jnp.float32, 'sb_w_out': _jnp.float32, 'ffn_w_gate_up': _jnp.float32, 'ffn_w_down': _jnp.float32}
MOMENT_SCALE = {'mix_norm_g': 1.664479e+00, 'ffn_norm_g': 6.167022e+00, 'mem_norm_g': 9.084625e-02, 'mem_w_kv': 2.415912e-02, 'mem_q_norm_g': 1.398186e-01, 'mem_k_norm_g': 1.397374e-01, 'ssd_w_in': 1.321926e-01, 'ssd_conv_w': 1.815526e-01, 'ssd_conv_b': 5.885242e-01, 'ssd_dt_bias': 2.634350e-01, 'ssd_a_log': 1.383742e+00, 'ssd_d': 1.216488e+00, 'ssd_norm_g': 5.483690e+00, 'ssd_w_out': 1.120705e+00, 'sb_w_in': 6.588561e-02, 'sb_q_norm_g': 2.939778e+00, 'sb_k_norm_g': 2.933529e+00, 'sb_w_out': 3.786224e-01, 'ffn_w_gate_up': 7.963582e-02, 'ffn_w_down': 4.069846e-01}


def _to_microbatches(a, axis):
    t = _jnp.moveaxis(a, axis, 0)
    t = t.reshape((N_MICROBATCH, t.shape[0] // N_MICROBATCH) + t.shape[1:])
    return _jnp.moveaxis(t, 1, axis + 1)


def setup_inputs(seed: int = 0) -> dict:
    inp = _fwd_setup_inputs(seed)
    key = _jax.random.fold_in(_jax.random.key(seed), 7919)
    shape, _ = _output_shape()
    out = dict(inp)
    out["loss_target"] = _jax.random.normal(_jax.random.fold_in(key, 0), shape, _jnp.float32)
    for i, name in enumerate(TWIN_WEIGHTS):
        w = inp[name].astype(_jnp.float32)
        if MOMENT_SCALE is None:
            s = _jnp.sqrt(_jnp.mean(_jnp.square(w)) + 1e-30)
        else:
            s = MOMENT_SCALE[name]
        km, kv = _jax.random.split(_jax.random.fold_in(key, i + 1))
        out[name] = w
        out["m_" + name] = s * _jax.random.normal(km, w.shape, _jnp.float32)
        out["v_" + name] = (s * s) * _jax.random.uniform(kv, w.shape, _jnp.float32, 0.5, 1.5)
    if N_MICROBATCH > 1:
        for name, axis in PER_EXAMPLE_BATCH_AXIS.items():
            out[name] = _to_microbatches(out[name], axis)
    return {'x': out['x'], 'mem': out['mem'], 'mix_norm_g': out['mix_norm_g'], 'ffn_norm_g': out['ffn_norm_g'], 'mem_norm_g': out['mem_norm_g'], 'mem_w_kv': out['mem_w_kv'], 'mem_q_norm_g': out['mem_q_norm_g'], 'mem_k_norm_g': out['mem_k_norm_g'], 'ssd_w_in': out['ssd_w_in'], 'ssd_conv_w': out['ssd_conv_w'], 'ssd_conv_b': out['ssd_conv_b'], 'ssd_dt_bias': out['ssd_dt_bias'], 'ssd_a_log': out['ssd_a_log'], 'ssd_d': out['ssd_d'], 'ssd_norm_g': out['ssd_norm_g'], 'ssd_w_out': out['ssd_w_out'], 'sb_w_in': out['sb_w_in'], 'sb_q_norm_g': out['sb_q_norm_g'], 'sb_k_norm_g': out['sb_k_norm_g'], 'sb_w_out': out['sb_w_out'], 'ffn_w_gate_up': out['ffn_w_gate_up'], 'ffn_w_down': out['ffn_w_down'], 'loss_target': out['loss_target'], 'm_mix_norm_g': out['m_mix_norm_g'], 'm_ffn_norm_g': out['m_ffn_norm_g'], 'm_mem_norm_g': out['m_mem_norm_g'], 'm_mem_w_kv': out['m_mem_w_kv'], 'm_mem_q_norm_g': out['m_mem_q_norm_g'], 'm_mem_k_norm_g': out['m_mem_k_norm_g'], 'm_ssd_w_in': out['m_ssd_w_in'], 'm_ssd_conv_w': out['m_ssd_conv_w'], 'm_ssd_conv_b': out['m_ssd_conv_b'], 'm_ssd_dt_bias': out['m_ssd_dt_bias'], 'm_ssd_a_log': out['m_ssd_a_log'], 'm_ssd_d': out['m_ssd_d'], 'm_ssd_norm_g': out['m_ssd_norm_g'], 'm_ssd_w_out': out['m_ssd_w_out'], 'm_sb_w_in': out['m_sb_w_in'], 'm_sb_q_norm_g': out['m_sb_q_norm_g'], 'm_sb_k_norm_g': out['m_sb_k_norm_g'], 'm_sb_w_out': out['m_sb_w_out'], 'm_ffn_w_gate_up': out['m_ffn_w_gate_up'], 'm_ffn_w_down': out['m_ffn_w_down'], 'v_mix_norm_g': out['v_mix_norm_g'], 'v_ffn_norm_g': out['v_ffn_norm_g'], 'v_mem_norm_g': out['v_mem_norm_g'], 'v_mem_w_kv': out['v_mem_w_kv'], 'v_mem_q_norm_g': out['v_mem_q_norm_g'], 'v_mem_k_norm_g': out['v_mem_k_norm_g'], 'v_ssd_w_in': out['v_ssd_w_in'], 'v_ssd_conv_w': out['v_ssd_conv_w'], 'v_ssd_conv_b': out['v_ssd_conv_b'], 'v_ssd_dt_bias': out['v_ssd_dt_bias'], 'v_ssd_a_log': out['v_ssd_a_log'], 'v_ssd_d': out['v_ssd_d'], 'v_ssd_norm_g': out['v_ssd_norm_g'], 'v_ssd_w_out': out['v_ssd_w_out'], 'v_sb_w_in': out['v_sb_w_in'], 'v_sb_q_norm_g': out['v_sb_q_norm_g'], 'v_sb_k_norm_g': out['v_sb_k_norm_g'], 'v_sb_w_out': out['v_sb_w_out'], 'v_ffn_w_gate_up': out['v_ffn_w_gate_up'], 'v_ffn_w_down': out['v_ffn_w_down']}


def _loss(weights, diff, rest, loss_target):
    with _jax.named_scope("forward"):
        args = {**rest, TWIN_DIFF_INPUT: diff, **{k: w.astype(_WEIGHT_DTYPES[k]) for k, w in weights.items()}}
        y = _forward(args)
    with _jax.named_scope("loss_head"):
        err = _jnp.square(y.astype(_jnp.float32) - loss_target)
        return 0.5 * _jnp.sum(_jnp.mean(err, axis=-1)) if err.ndim else 0.5 * err


def _adamw(w, g, m, v):
    m = ADAM_B1 * m + (1.0 - ADAM_B1) * g
    v = ADAM_B2 * v + (1.0 - ADAM_B2) * _jnp.square(g)
    m_hat = m / (1.0 - ADAM_B1 ** ADAM_STEP)
    v_hat = v / (1.0 - ADAM_B2 ** ADAM_STEP)
    delta = -ADAM_LR * (m_hat / (_jnp.sqrt(v_hat) + ADAM_EPS) + ADAM_WD * w)
    return delta, m, v


def reference(x, mem, mix_norm_g, ffn_norm_g, mem_norm_g, mem_w_kv, mem_q_norm_g, mem_k_norm_g, ssd_w_in, ssd_conv_w, ssd_conv_b, ssd_dt_bias, ssd_a_log, ssd_d, ssd_norm_g, ssd_w_out, sb_w_in, sb_q_norm_g, sb_k_norm_g, sb_w_out, ffn_w_gate_up, ffn_w_down, loss_target, m_mix_norm_g, m_ffn_norm_g, m_mem_norm_g, m_mem_w_kv, m_mem_q_norm_g, m_mem_k_norm_g, m_ssd_w_in, m_ssd_conv_w, m_ssd_conv_b, m_ssd_dt_bias, m_ssd_a_log, m_ssd_d, m_ssd_norm_g, m_ssd_w_out, m_sb_w_in, m_sb_q_norm_g, m_sb_k_norm_g, m_sb_w_out, m_ffn_w_gate_up, m_ffn_w_down, v_mix_norm_g, v_ffn_norm_g, v_mem_norm_g, v_mem_w_kv, v_mem_q_norm_g, v_mem_k_norm_g, v_ssd_w_in, v_ssd_conv_w, v_ssd_conv_b, v_ssd_dt_bias, v_ssd_a_log, v_ssd_d, v_ssd_norm_g, v_ssd_w_out, v_sb_w_in, v_sb_q_norm_g, v_sb_k_norm_g, v_sb_w_out, v_ffn_w_gate_up, v_ffn_w_down):
    given = dict(x=x, mem=mem, mix_norm_g=mix_norm_g, ffn_norm_g=ffn_norm_g, mem_norm_g=mem_norm_g, mem_w_kv=mem_w_kv, mem_q_norm_g=mem_q_norm_g, mem_k_norm_g=mem_k_norm_g, ssd_w_in=ssd_w_in, ssd_conv_w=ssd_conv_w, ssd_conv_b=ssd_conv_b, ssd_dt_bias=ssd_dt_bias, ssd_a_log=ssd_a_log, ssd_d=ssd_d, ssd_norm_g=ssd_norm_g, ssd_w_out=ssd_w_out, sb_w_in=sb_w_in, sb_q_norm_g=sb_q_norm_g, sb_k_norm_g=sb_k_norm_g, sb_w_out=sb_w_out, ffn_w_gate_up=ffn_w_gate_up, ffn_w_down=ffn_w_down, loss_target=loss_target, m_mix_norm_g=m_mix_norm_g, m_ffn_norm_g=m_ffn_norm_g, m_mem_norm_g=m_mem_norm_g, m_mem_w_kv=m_mem_w_kv, m_mem_q_norm_g=m_mem_q_norm_g, m_mem_k_norm_g=m_mem_k_norm_g, m_ssd_w_in=m_ssd_w_in, m_ssd_conv_w=m_ssd_conv_w, m_ssd_conv_b=m_ssd_conv_b, m_ssd_dt_bias=m_ssd_dt_bias, m_ssd_a_log=m_ssd_a_log, m_ssd_d=m_ssd_d, m_ssd_norm_g=m_ssd_norm_g, m_ssd_w_out=m_ssd_w_out, m_sb_w_in=m_sb_w_in, m_sb_q_norm_g=m_sb_q_norm_g, m_sb_k_norm_g=m_sb_k_norm_g, m_sb_w_out=m_sb_w_out, m_ffn_w_gate_up=m_ffn_w_gate_up, m_ffn_w_down=m_ffn_w_down, v_mix_norm_g=v_mix_norm_g, v_ffn_norm_g=v_ffn_norm_g, v_mem_norm_g=v_mem_norm_g, v_mem_w_kv=v_mem_w_kv, v_mem_q_norm_g=v_mem_q_norm_g, v_mem_k_norm_g=v_mem_k_norm_g, v_ssd_w_in=v_ssd_w_in, v_ssd_conv_w=v_ssd_conv_w, v_ssd_conv_b=v_ssd_conv_b, v_ssd_dt_bias=v_ssd_dt_bias, v_ssd_a_log=v_ssd_a_log, v_ssd_d=v_ssd_d, v_ssd_norm_g=v_ssd_norm_g, v_ssd_w_out=v_ssd_w_out, v_sb_w_in=v_sb_w_in, v_sb_q_norm_g=v_sb_q_norm_g, v_sb_k_norm_g=v_sb_k_norm_g, v_sb_w_out=v_sb_w_out, v_ffn_w_gate_up=v_ffn_w_gate_up, v_ffn_w_down=v_ffn_w_down)
    weights = {n: given[n] for n in TWIN_WEIGHTS}
    shared = {n: given[n] for n in SHARED_INPUTS}
    per_example = {n: given[n] for n in ['x', 'mem']}
    grad_fn = _jax.value_and_grad(_loss, argnums=(0, 1))

    def one_microbatch(ex, loss_target):
        ex = dict(ex)
        diff = ex.pop(TWIN_DIFF_INPUT)
        return grad_fn(weights, diff, {**shared, **ex}, loss_target)

    if N_MICROBATCH == 1:
        loss, (grad_w, grad_x) = one_microbatch(per_example, given["loss_target"])
    else:
        def body(carry, xs):
            loss_sum, grad_sum = carry
            l_k, (gw_k, gx_k) = one_microbatch(xs[0], xs[1])
            with _jax.named_scope("update"):
                return (loss_sum + l_k, _jax.tree.map(_jnp.add, grad_sum, gw_k)), gx_k

        init = (_jnp.zeros((), _jnp.float32), _jax.tree.map(_jnp.zeros_like, weights))
        (loss, grad_w), grad_x = _jax.lax.scan(body, init, (per_example, given["loss_target"]))
    with _jax.named_scope("update"):
        delta_w, new_m, new_v = {}, {}, {}
        for n in TWIN_WEIGHTS:
            delta_w[n], new_m[n], new_v[n] = _adamw(weights[n], grad_w[n], given["m_" + n], given["v_" + n])
    return (loss, grad_x, *[grad_w[n] for n in TWIN_WEIGHTS], *[delta_w[n] for n in TWIN_WEIGHTS],
            *[new_m[n] for n in TWIN_WEIGHTS], *[new_v[n] for n in TWIN_WEIGHTS])
```

```python
import functools
import math

import jax
import jax.numpy as jnp
from jax import lax
from jax.experimental import pallas as pl
from jax.experimental.pallas import tpu as pltpu

f32, bf16 = jnp.float32, jnp.bfloat16
MESH = pl.DeviceIdType.MESH

N_DEV = 8
D = 1024
DEPTH = 4
EPS = 1e-6
MEM_LEN, MEM_HEADS, MEM_W, HD = 256, 4, 512, 128
SSD_INNER, SSD_HEADS, SSD_G, SSD_P, SSD_N, SSD_L = 1536, 24, 4, 64, 128, 128
SSD_GW = SSD_INNER // SSD_G
SSD_CONV_DIM = 2560
SSD_IN = 4632
SSD_IN_PAD = 4736
SB_W, SB_HEADS, SB_IN = 1536, 12, 5120
SB_BLK = 128
SB_SCALE = HD ** -0.5
FFN_H = 2816
LANES = 128
VMEM_LIMIT = 56 * 1024 * 1024

ADAM_LR, ADAM_B1, ADAM_B2, ADAM_EPS, ADAM_WD, ADAM_STEP = 0.001, 0.9, 0.999, 1e-08, 0.01, 10

HIGHEST = lax.Precision.HIGHEST


def _params(n_grid):
    return pltpu.CompilerParams(dimension_semantics=("arbitrary",) * n_grid, vmem_limit_bytes=VMEM_LIMIT)


def _dg(a, b, ca, cb):
    return lax.dot_general(a.astype(bf16), b.astype(bf16), (((ca,), (cb,)), ((), ())), preferred_element_type=f32)


@jax.custom_vjp
def bdot_nn(a, b):
    return _dg(a, b, 1, 0)


def _nn_fwd(a, b):
    return _dg(a, b, 1, 0), (a, b)


def _nn_bwd(res, ct):
    a, b = res
    return _dg(ct, b, 1, 1).astype(a.dtype), _dg(a, ct, 0, 0).astype(b.dtype)


bdot_nn.defvjp(_nn_fwd, _nn_bwd)


@jax.custom_vjp
def bdot_nt(a, b):
    return _dg(a, b, 1, 1)


def _nt_fwd(a, b):
    return _dg(a, b, 1, 1), (a, b)


def _nt_bwd(res, ct):
    a, b = res
    return _dg(ct, b, 1, 0).astype(a.dtype), _dg(ct, a, 0, 0).astype(b.dtype)


bdot_nt.defvjp(_nt_fwd, _nt_bwd)


@jax.custom_vjp
def bdot_tn(a, b):
    return _dg(a, b, 0, 0)


def _tn_fwd(a, b):
    return _dg(a, b, 0, 0), (a, b)


def _tn_bwd(res, ct):
    a, b = res
    return _dg(b, ct, 1, 1).astype(a.dtype), _dg(a, ct, 1, 0).astype(b.dtype)


bdot_tn.defvjp(_tn_fwd, _tn_bwd)


def _rms(x, g):
    return x * lax.rsqrt(jnp.mean(x * x, axis=-1, keepdims=True) + EPS) * g


def _iota(shape, axis):
    return lax.broadcasted_iota(jnp.int32, shape, axis)


def _fn_call(fn, args, in_specs, out_shapes, out_specs, grid, name, acc=None):
    n_in = len(args)
    acc = acc or {}
    n_grid = len(grid)

    def body(*refs):
        ins, outs = refs[:n_in], refs[n_in:]
        res = fn(*[r[...] for r in ins])
        if not isinstance(res, (tuple, list)):
            res = (res,)
        for k, (o, r) in enumerate(zip(outs, res)):
            mode = acc.get(k)
            if mode is None:
                o[...] = r.astype(o.dtype)
                continue
            if mode == "last":
                first = pl.program_id(n_grid - 1) == 0
            else:
                first = functools.reduce(jnp.logical_and, [pl.program_id(d) == 0 for d in range(n_grid)])

            @pl.when(first)
            def _(o=o, r=r):
                o[...] = r.astype(o.dtype)

            @pl.when(jnp.logical_not(first))
            def _(o=o, r=r):
                o[...] += r.astype(o.dtype)

    return pl.pallas_call(
        body, grid=grid, in_specs=in_specs, out_specs=out_specs, out_shape=out_shapes, name=name,
        compiler_params=_params(n_grid))(*args)


def _matmul(a, b, *, ta=False, tb=False, out_dtype=f32, tm, tn, tk, res=None, name):
    M, K = (a.shape[1], a.shape[0]) if ta else a.shape
    N = b.shape[0] if tb else b.shape[1]
    tm, tn, tk = min(tm, M), min(tn, N), min(tk, K)
    assert M % tm == 0 and N % tn == 0 and K % tk == 0, (name, M, N, K, tm, tn, tk)
    nk = K // tk
    a_spec = pl.BlockSpec((tk, tm), lambda i, j, k: (k, i)) if ta else pl.BlockSpec((tm, tk), lambda i, j, k: (i, k))
    b_spec = pl.BlockSpec((tn, tk), lambda i, j, k: (j, k)) if tb else pl.BlockSpec((tk, tn), lambda i, j, k: (k, j))
    o_spec = pl.BlockSpec((tm, tn), lambda i, j, k: (i, j))
    ca, cb = (0 if ta else 1), (1 if tb else 0)

    def body(*refs):
        if res is None:
            a_ref, b_ref, o_ref, acc_ref = refs
            r_ref = None
        else:
            a_ref, b_ref, r_ref, o_ref, acc_ref = refs
        k = pl.program_id(2)
        part = _dg(a_ref[...], b_ref[...], ca, cb)

        @pl.when(k == 0)
        def _():
            acc_ref[...] = part

        @pl.when(k > 0)
        def _():
            acc_ref[...] += part

        @pl.when(k == nk - 1)
        def _():
            out = acc_ref[...]
            if r_ref is not None:
                out = out + r_ref[...].astype(f32)
            o_ref[...] = out.astype(o_ref.dtype)

    args = (a, b) if res is None else (a, b, res)
    in_specs = [a_spec, b_spec] + ([] if res is None else [o_spec])
    return pl.pallas_call(
        body, grid=(M // tm, N // tn, nk), in_specs=in_specs, out_specs=o_spec,
        out_shape=jax.ShapeDtypeStruct((M, N), out_dtype), name=name,
        scratch_shapes=[pltpu.VMEM((tm, tn), f32)], compiler_params=_params(3))(*args)


def _row_tile(T):
    return min(T, 512)


def _my_pos():
    return lax.axis_index("x"), lax.axis_index("y"), lax.axis_index("c")


def _allgather_hbm(xs, name):
    R, C = xs.shape

    def body(x_ref, out_ref, send_sems, recv_sems, local_sem):
        x, y, c = _my_pos()
        me, sibling = (x, y, c), (x, y, 1 - c)
        chips = [(1 - x, y), (x, 1 - y), (1 - x, 1 - y)]

        def slot(px, py, pc):
            return out_ref.at[4 * px + 2 * py + pc]

        def copy(k, block, to, src=None):
            return pltpu.make_async_remote_copy(
                src_ref=slot(*block) if src is None else src, dst_ref=slot(*block),
                send_sem=send_sems.at[k], recv_sem=recv_sems.at[k], device_id=to, device_id_type=MESH)

        mine = pltpu.make_async_copy(x_ref, slot(*me), local_sem)
        mine.start()
        first = [copy(0, me, sibling, src=x_ref)]
        first += [copy(1 + j, me, (*chip, c), src=x_ref) for j, chip in enumerate(chips)]
        for cp in first:
            cp.start()
        passed = [copy(4 + j, (*chip, c), sibling) for j, chip in enumerate(chips)]
        for j, chip in enumerate(chips):
            copy(1 + j, (*chip, c), me).wait_recv()
            passed[j].start()
        copy(0, sibling, me).wait_recv()
        for j, chip in enumerate(chips):
            copy(4 + j, (*chip, 1 - c), me).wait_recv()
        for cp in first + passed:
            cp.wait_send()
        mine.wait()

    return pl.pallas_call(
        body, out_shape=jax.ShapeDtypeStruct((N_DEV, R, C), xs.dtype),
        in_specs=[pl.BlockSpec(memory_space=pl.ANY)], out_specs=pl.BlockSpec(memory_space=pl.ANY),
        scratch_shapes=[pltpu.SemaphoreType.DMA((7,)), pltpu.SemaphoreType.DMA((7,)), pltpu.SemaphoreType.DMA],
        name=name)(xs)


def _allgather_vmem(xs, name):
    R, C = xs.shape

    def body(x_ref, out_ref, send_sems, recv_sems):
        x, y, c = _my_pos()
        me = 4 * x + 2 * y + c
        out_ref[me] = x_ref[...]
        copies = []
        for k in range(1, N_DEV):
            px = 1 - x if k & 4 else x
            py = 1 - y if k & 2 else y
            pc = 1 - c if k & 1 else c
            cp = pltpu.make_async_remote_copy(
                src_ref=x_ref, dst_ref=out_ref.at[me], send_sem=send_sems.at[k - 1], recv_sem=recv_sems.at[k - 1],
                device_id=(px, py, pc), device_id_type=MESH)
            cp.start()
            copies.append(cp)
        for cp in copies:
            cp.wait()

    return pl.pallas_call(
        body, out_shape=jax.ShapeDtypeStruct((N_DEV, R, C), xs.dtype),
        in_specs=[pl.BlockSpec(memory_space=pltpu.VMEM)], out_specs=pl.BlockSpec(memory_space=pltpu.VMEM),
        scratch_shapes=[pltpu.SemaphoreType.DMA((7,)), pltpu.SemaphoreType.DMA((7,))], name=name)(xs)


def _exchange_slices(g, name):
    _, R, C = g.shape

    def body(g_ref, out_ref, send_sems, recv_sems, local_sem):
        x, y, c = _my_pos()
        me = 4 * x + 2 * y + c
        mine = pltpu.make_async_copy(g_ref.at[me], out_ref.at[me], local_sem)
        mine.start()
        copies = []
        for k in range(1, N_DEV):
            px = 1 - x if k & 4 else x
            py = 1 - y if k & 2 else y
            pc = 1 - c if k & 1 else c
            cp = pltpu.make_async_remote_copy(
                src_ref=g_ref.at[4 * px + 2 * py + pc], dst_ref=out_ref.at[me],
                send_sem=send_sems.at[k - 1], recv_sem=recv_sems.at[k - 1],
                device_id=(px, py, pc), device_id_type=MESH)
            cp.start()
            copies.append(cp)
        for cp in copies:
            cp.wait()
        mine.wait()

    return pl.pallas_call(
        body, out_shape=jax.ShapeDtypeStruct(g.shape, g.dtype),
        in_specs=[pl.BlockSpec(memory_space=pl.ANY)], out_specs=pl.BlockSpec(memory_space=pl.ANY),
        scratch_shapes=[pltpu.SemaphoreType.DMA((7,)), pltpu.SemaphoreType.DMA((7,)), pltpu.SemaphoreType.DMA],
        name=name)(g)


def _adamw_math(w, g, m, v):
    m = ADAM_B1 * m + (1.0 - ADAM_B1) * g
    v = ADAM_B2 * v + (1.0 - ADAM_B2) * jnp.square(g)
    m_hat = m / (1.0 - ADAM_B1 ** ADAM_STEP)
    v_hat = v / (1.0 - ADAM_B2 ** ADAM_STEP)
    delta = -ADAM_LR * (m_hat / (jnp.sqrt(v_hat) + ADAM_EPS) + ADAM_WD * w)
    return delta, m, v


def _sum_slots(parts):
    g = parts[0].astype(f32)
    for i in range(1, N_DEV):
        g = g + parts[i].astype(f32)
    return g


def _adamw_reduce(parts, w, m, v, tr, name):
    R = w.shape[0]
    assert R % tr == 0

    def fn(p, w, m, v):
        g = _sum_slots(p)
        return (g,) + _adamw_math(w, g, m, v)

    row = pl.BlockSpec((tr, LANES), lambda i: (i, 0))
    sds = jax.ShapeDtypeStruct((R, LANES), f32)
    return _fn_call(fn, (parts, w, m, v), [pl.BlockSpec((N_DEV, tr, LANES), lambda i: (0, i, 0)), row, row, row],
                    (sds,) * 4, (row,) * 4, (R // tr,), name)


def _rmsnorm_fwd(x, g, name):
    T = x.shape[0]
    tm = _row_tile(T)
    row = pl.BlockSpec((tm, D), lambda i: (i, 0))
    par = pl.BlockSpec((1, D), lambda i: (0, 0))
    return _fn_call(lambda x, g: _rms(x, g), (x, g), [row, par], jax.ShapeDtypeStruct((T, D), bf16), row, (T // tm,), name)


def _rmsnorm_bwd(x, g, dh, dres, name):
    T = x.shape[0]
    tm = _row_tile(T)
    row = pl.BlockSpec((tm, D), lambda i: (i, 0))
    par = pl.BlockSpec((1, D), lambda i: (0, 0))

    def fn(x, g, dh, dres):
        _, vjp = jax.vjp(_rms, x, g)
        dx, dg = vjp(dh.astype(f32))
        return dx + dres, dg

    return _fn_call(fn, (x, g, dh, dres), [row, par, row, row],
                    (jax.ShapeDtypeStruct((T, D), f32), jax.ShapeDtypeStruct((1, D), f32)), (row, par),
                    (T // tm,), name, acc={1: "all"})


def _swiglu_act(g, u):
    return jax.nn.silu(g) * u


def _ffn_act_fwd(gu, name):
    T = gu.shape[0]
    tm, tc = _row_tile(T), 256
    nc = FFN_H // tc
    return _fn_call(_swiglu_act, (gu, gu),
                    [pl.BlockSpec((tm, tc), lambda i, j: (i, j)), pl.BlockSpec((tm, tc), lambda i, j: (i, nc + j))],
                    jax.ShapeDtypeStruct((T, FFN_H), bf16), pl.BlockSpec((tm, tc), lambda i, j: (i, j)),
                    (T // tm, nc), name)


def _ffn_act_bwd(gu, dact, name):
    T = gu.shape[0]
    tm, tc = _row_tile(T), 256
    nc = FFN_H // tc

    def fn(g, u, da):
        _, vjp = jax.vjp(_swiglu_act, g, u)
        return vjp(da.astype(f32))

    lo = pl.BlockSpec((tm, tc), lambda i, j: (i, j))
    hi = pl.BlockSpec((tm, tc), lambda i, j: (i, nc + j))
    dg, du = _fn_call(fn, (gu, gu, dact), [lo, hi, lo],
                      (jax.ShapeDtypeStruct((T, FFN_H), bf16),) * 2, (lo, lo), (T // tm, nc), name)
    return jnp.concatenate([dg, du], axis=1)


def _loss_head(x, target, name):
    T = x.shape[0]
    tm = _row_tile(T)
    row = pl.BlockSpec((tm, D), lambda i: (i, 0))
    par = pl.BlockSpec((1, LANES), lambda i: (0, 0))

    def fn(x, t):
        e = x - t
        s = jnp.sum(e * e, axis=0, keepdims=True)
        part = s[:, 0:LANES]
        for k in range(1, D // LANES):
            part = part + s[:, k * LANES:(k + 1) * LANES]
        return e * (1.0 / D), part * (0.5 / D)

    return _fn_call(fn, (x, target), [row, row],
                    (jax.ShapeDtypeStruct((T, D), f32), jax.ShapeDtypeStruct((1, LANES), f32)), (row, par),
                    (T // tm,), name, acc={1: "all"})


def _memkv_fn(mem, mg, wkv, kg):
    mn = _rms(mem, mg)
    kv = bdot_nn(mn, wkv)
    ks = [_rms(kv[:, h * HD:(h + 1) * HD], kg) for h in range(MEM_HEADS)]
    return jnp.concatenate(ks, axis=1), kv[:, MEM_W:]


def _memkv_fwd(mem, mg, wkv, kg, name):
    whole = lambda s: pl.BlockSpec(s, lambda i: (0,) * len(s))
    sds = jax.ShapeDtypeStruct((MEM_LEN, MEM_W), f32)
    return _fn_call(lambda m, g, w, k: _memkv_fn(m, g, w.astype(f32), k), (mem, mg, wkv, kg),
                    [whole((MEM_LEN, D)), whole((1, D)), whole((D, 2 * MEM_W)), whole((1, HD))],
                    (sds, sds), (whole((MEM_LEN, MEM_W)),) * 2, (1,), name)


def _memkv_bwd(mem, mg, wkv, kg, dk, dv, name):
    whole = lambda s: pl.BlockSpec(s, lambda i: (0,) * len(s))

    def fn(m, g, w, k, dk, dv):
        _, vjp = jax.vjp(lambda g, w, k: _memkv_fn(m, g, w, k), g, w.astype(f32), k)
        return vjp((dk, dv))

    return _fn_call(fn, (mem, mg, wkv, kg, dk, dv),
                    [whole((MEM_LEN, D)), whole((1, D)), whole((D, 2 * MEM_W)), whole((1, HD)),
                     whole((MEM_LEN, MEM_W)), whole((MEM_LEN, MEM_W))],
                    (jax.ShapeDtypeStruct((1, D), f32), jax.ShapeDtypeStruct((D, 2 * MEM_W), f32),
                     jax.ShapeDtypeStruct((1, HD), f32)),
                    (whole((1, D)), whole((D, 2 * MEM_W)), whole((1, HD))), (1,), name)


def _memattn_fn(q, k, v, qg):
    qn = _rms(q, qg)
    s = bdot_nt(qn, k) * (HD ** -0.5)
    s = s - jnp.max(s, axis=-1, keepdims=True)
    p = jnp.exp(s)
    p = p / jnp.sum(p, axis=-1, keepdims=True)
    return bdot_nn(p, v)


def _memattn_fwd(proj, q_col, k, v, qg, name):
    T = proj.shape[0]
    tm = _row_tile(T)
    return _fn_call(_memattn_fn, (proj, k, v, qg),
                    [pl.BlockSpec((tm, HD), lambda h, i: (i, q_col + h)), pl.BlockSpec((MEM_LEN, HD), lambda h, i: (0, h)),
                     pl.BlockSpec((MEM_LEN, HD), lambda h, i: (0, h)), pl.BlockSpec((1, HD), lambda h, i: (0, 0))],
                    jax.ShapeDtypeStruct((T, MEM_W), bf16), pl.BlockSpec((tm, HD), lambda h, i: (i, h)),
                    (MEM_HEADS, T // tm), name)


def _memattn_bwd(proj, q_col, k, v, qg, dycat, do_col, name):
    T = proj.shape[0]
    tm = _row_tile(T)

    def fn(q, k, v, qg, do):
        _, vjp = jax.vjp(_memattn_fn, q, k, v, qg)
        dq, dk, dv, dg = vjp(do.astype(f32))
        return dq, dk, dv, dg[None]

    kv_spec = pl.BlockSpec((MEM_LEN, HD), lambda h, i: (0, h))
    kv_sds = jax.ShapeDtypeStruct((MEM_LEN, MEM_W), f32)
    return _fn_call(fn, (proj, k, v, qg, dycat),
                    [pl.BlockSpec((tm, HD), lambda h, i: (i, q_col + h)), kv_spec, kv_spec,
                     pl.BlockSpec((1, HD), lambda h, i: (0, 0)), pl.BlockSpec((tm, HD), lambda h, i: (i, do_col + h))],
                    (jax.ShapeDtypeStruct((T, MEM_W), bf16), kv_sds, kv_sds, jax.ShapeDtypeStruct((MEM_HEADS, 1, HD), f32)),
                    (pl.BlockSpec((tm, HD), lambda h, i: (i, h)), kv_spec, kv_spec,
                     pl.BlockSpec((1, 1, HD), lambda h, i: (h, 0, 0))),
                    (MEM_HEADS, T // tm), name, acc={1: "last", 2: "last", 3: "last"})


def _conv_taps(xp, w, first, tm):
    out = w[0:1, :] * xp[first:first + tm, :]
    for k in range(1, 4):
        out = out + w[k:k + 1, :] * xp[first + k:first + k + tm, :]
    return out


def _conv_blocks(T):
    tm, tc = _row_tile(T), 512
    nt = T // tm
    cur = pl.BlockSpec((tm, tc), lambda j, i: (i, 3 + j))
    prev = pl.BlockSpec((8, tc), lambda j, i: (jnp.maximum(i * (tm // 8) - 1, 0), 3 + j))
    par4 = pl.BlockSpec((4, tc), lambda j, i: (0, j))
    par1 = pl.BlockSpec((1, tc), lambda j, i: (0, j))
    out = pl.BlockSpec((tm, tc), lambda j, i: (i, j))
    return tm, tc, nt, cur, prev, par4, par1, out


def _conv_fwd(proj, w, b, name):
    T = proj.shape[0]
    tm, tc, nt, cur, prev, par4, par1, out = _conv_blocks(T)

    def body(prev_ref, cur_ref, w_ref, b_ref, o_ref):
        halo = jnp.where(pl.program_id(1) == 0, 0.0, prev_ref[...])
        xp = jnp.concatenate([halo, cur_ref[...]], axis=0)
        o_ref[...] = jax.nn.silu(_conv_taps(xp, w_ref[...], 5, tm) + b_ref[...])

    return pl.pallas_call(body, grid=(SSD_CONV_DIM // tc, nt), in_specs=[prev, cur, par4, par1], out_specs=out,
                          out_shape=jax.ShapeDtypeStruct((T, SSD_CONV_DIM), f32), name=name,
                          compiler_params=_params(2))(proj, proj, w, b)


def _conv_bwd_pre(proj, w, b, dact, name):
    T = proj.shape[0]
    tm, tc, nt, cur, prev, par4, par1, out = _conv_blocks(T)

    def body(prev_ref, cur_ref, w_ref, b_ref, da_ref, dp_ref, dw_ref, db_ref):
        i = pl.program_id(1)
        halo = jnp.where(i == 0, 0.0, prev_ref[...])
        xp = jnp.concatenate([halo, cur_ref[...]], axis=0)
        pre = _conv_taps(xp, w_ref[...], 5, tm) + b_ref[...]
        sig = jax.nn.sigmoid(pre)
        dpre = da_ref[...] * (sig * (1.0 + pre * (1.0 - sig)))
        dp_ref[...] = dpre
        dw = jnp.concatenate([jnp.sum(dpre * xp[5 + k:5 + k + tm, :], axis=0, keepdims=True) for k in range(4)], axis=0)
        db = jnp.sum(dpre, axis=0, keepdims=True)

        @pl.when(i == 0)
        def _():
            dw_ref[...] = dw
            db_ref[...] = db

        @pl.when(i > 0)
        def _():
            dw_ref[...] += dw
            db_ref[...] += db

    return pl.pallas_call(
        body, grid=(SSD_CONV_DIM // tc, nt), in_specs=[prev, cur, par4, par1, out], out_specs=(out, par4, par1),
        out_shape=(jax.ShapeDtypeStruct((T, SSD_CONV_DIM), f32), jax.ShapeDtypeStruct((4, SSD_CONV_DIM), f32),
                   jax.ShapeDtypeStruct((1, SSD_CONV_DIM), f32)),
        name=name, compiler_params=_params(2))(proj, proj, w, b, dact)


def _conv_bwd_in(dpre, w, name):
    T = dpre.shape[0]
    tm, tc, nt, _, _, par4, _, out = _conv_blocks(T)
    nxt = pl.BlockSpec((8, tc), lambda j, i: (jnp.minimum((i + 1) * (tm // 8), T // 8 - 1), j))

    def body(cur_ref, nxt_ref, w_ref, o_ref):
        halo = jnp.where(pl.program_id(1) == nt - 1, 0.0, nxt_ref[...])
        xp = jnp.concatenate([cur_ref[...], halo], axis=0)
        w = w_ref[...]
        acc = w[3:4, :] * xp[0:tm, :]
        for k in range(3):
            acc = acc + w[k:k + 1, :] * xp[3 - k:3 - k + tm, :]
        o_ref[...] = acc.astype(o_ref.dtype)

    return pl.pallas_call(body, grid=(SSD_CONV_DIM // tc, nt), in_specs=[out, nxt, par4], out_specs=out,
                          out_shape=jax.ShapeDtypeStruct((T, SSD_CONV_DIM), bf16), name=name,
                          compiler_params=_params(2))(dpre, dpre, w)


def _ssd_chunk(hbase, xs, bm, cm, z, dtr, dtb, alog, dsk, ng, ht):
    L = SSD_L
    dt = jax.nn.softplus(dtr + dtb)
    da = dt * (-jnp.exp(alog))
    li, si = _iota((L, L), 0), _iota((L, L), 1)
    causal = li >= si
    cs = jnp.dot(causal.astype(f32), da, precision=HIGHEST, preferred_element_type=f32)
    cs_t = cs.T
    expand = (_iota((LANES, SSD_GW), 0) == hbase + _iota((LANES, SSD_GW), 1) // SSD_P).astype(f32)
    cs_e = jnp.dot(cs, expand, precision=HIGHEST, preferred_element_type=f32)
    dt_e = jnp.dot(dt, expand, precision=HIGHEST, preferred_element_type=f32)
    xdt = xs * dt_e
    cb = bdot_nt(cm, bm)
    chan_head = _iota((1, SSD_GW), 1) // SSD_P
    y = jnp.zeros((L, SSD_GW), f32)
    for r in range(SSD_GW // SSD_P):
        cs_col = jnp.sum(cs * (_iota((1, LANES), 1) == hbase + r).astype(f32), axis=1, keepdims=True)
        cs_row = jnp.sum(cs_t * (_iota((LANES, 1), 0) == hbase + r).astype(f32), axis=0, keepdims=True)
        decay = jnp.where(causal, jnp.exp(jnp.where(causal, cs_col - cs_row, 0.0)), 0.0)
        y = y + bdot_nn(cb * decay, xdt * (chan_head == r).astype(f32))
    y = y + jnp.exp(cs_e) * bdot_nn(cm, ht)
    cs_last = jnp.sum(cs_e * (_iota((L, 1), 0) == L - 1).astype(f32), axis=0, keepdims=True)
    ht_new = ht * jnp.exp(cs_last) + bdot_tn(bm, xdt * jnp.exp(cs_last - cs_e))
    y = (y + dsk * xs) * jax.nn.silu(z)
    return _rms(y, ng), ht_new


def _ssd_specs(T, rev):
    nc = T // SSD_L
    cidx = (lambda c: nc - 1 - c) if rev else (lambda c: c)
    return nc, dict(
        xs=pl.BlockSpec((SSD_L, SSD_GW), lambda g, c: (cidx(c), g)),
        bm=pl.BlockSpec((SSD_L, SSD_N), lambda g, c: (cidx(c), 12 + g)),
        cm=pl.BlockSpec((SSD_L, SSD_N), lambda g, c: (cidx(c), 16 + g)),
        z=pl.BlockSpec((SSD_L, SSD_GW), lambda g, c: (cidx(c), g)),
        dt=pl.BlockSpec((SSD_L, LANES), lambda g, c: (cidx(c), 36)),
        p128=pl.BlockSpec((1, LANES), lambda g, c: (0, 0)),
        pgw=pl.BlockSpec((1, SSD_GW), lambda g, c: (0, g)),
        hs=pl.BlockSpec((None, None, SSD_N, SSD_GW), lambda g, c: (g, cidx(c), 0, 0)),
        grp=pl.BlockSpec((SSD_L, SSD_N), lambda g, c: (cidx(c), g)),
    )


def _ssd_fwd(xbc, proj, dtb, alog, dsk, ng, name):
    T = proj.shape[0]
    nc, s = _ssd_specs(T, False)

    def body(xs_ref, bm_ref, cm_ref, z_ref, dt_ref, dtb_ref, alog_ref, dsk_ref, ng_ref, y_ref, hs_ref, h_scr):
        @pl.when(pl.program_id(1) == 0)
        def _():
            h_scr[...] = jnp.zeros_like(h_scr)

        ht = h_scr[...]
        hs_ref[...] = ht
        y, ht_new = _ssd_chunk(pl.program_id(0) * (SSD_GW // SSD_P), xs_ref[...], bm_ref[...], cm_ref[...], z_ref[...],
                               dt_ref[...], dtb_ref[...], alog_ref[...], dsk_ref[...], ng_ref[...], ht)
        y_ref[...] = y.astype(y_ref.dtype)
        h_scr[...] = ht_new

    return pl.pallas_call(
        body, grid=(SSD_G, nc),
        in_specs=[s["xs"], s["bm"], s["cm"], s["z"], s["dt"], s["p128"], s["p128"], s["pgw"], s["pgw"]],
        out_specs=(s["xs"], s["hs"]),
        out_shape=(jax.ShapeDtypeStruct((T, SSD_INNER), bf16), jax.ShapeDtypeStruct((SSD_G, nc, SSD_N, SSD_GW), f32)),
        scratch_shapes=[pltpu.VMEM((SSD_N, SSD_GW), f32)], name=name, compiler_params=_params(2))(
            xbc, xbc, xbc, proj, proj, dtb, alog, dsk, ng)


def _ssd_bwd(xbc, proj, dtb, alog, dsk, ng, hs, dycat, name):
    T = proj.shape[0]
    nc, s = _ssd_specs(T, True)

    def body(xs_ref, bm_ref, cm_ref, z_ref, dt_ref, dtb_ref, alog_ref, dsk_ref, ng_ref, hs_ref, dy_ref,
             dxs_ref, dbm_ref, dcm_ref, dz_ref, ddt_ref, ddtb_ref, dalog_ref, ddsk_ref, dng_ref, dh_scr):
        c = pl.program_id(1)

        @pl.when(c == 0)
        def _():
            dh_scr[...] = jnp.zeros_like(dh_scr)

        hbase = pl.program_id(0) * (SSD_GW // SSD_P)
        _, vjp = jax.vjp(functools.partial(_ssd_chunk, hbase), xs_ref[...], bm_ref[...], cm_ref[...], z_ref[...],
                         dt_ref[...], dtb_ref[...], alog_ref[...], dsk_ref[...], ng_ref[...], hs_ref[...])
        dxs, dbm, dcm, dz, ddt, ddtb, dalog, ddsk, dng, dht = vjp((dy_ref[...].astype(f32), dh_scr[...]))
        dxs_ref[...] = dxs
        dbm_ref[...] = dbm
        dcm_ref[...] = dcm
        dz_ref[...] = dz.astype(dz_ref.dtype)
        ddt_ref[...] = ddt
        dh_scr[...] = dht

        @pl.when(c == 0)
        def _():
            ddtb_ref[...] = ddtb
            dalog_ref[...] = dalog
            ddsk_ref[...] = ddsk
            dng_ref[...] = dng

        @pl.when(c > 0)
        def _():
            ddtb_ref[...] += ddtb
            dalog_ref[...] += dalog
            ddsk_ref[...] += ddsk
            dng_ref[...] += dng

    cidx = lambda c: nc - 1 - c
    g128 = pl.BlockSpec((None, 1, LANES), lambda g, c: (g, 0, 0))
    return pl.pallas_call(
        body, grid=(SSD_G, nc),
        in_specs=[s["xs"], s["bm"], s["cm"], s["z"], s["dt"], s["p128"], s["p128"], s["pgw"], s["pgw"], s["hs"], s["xs"]],
        out_specs=(s["xs"], s["grp"], s["grp"], s["xs"],
                   pl.BlockSpec((None, SSD_L, LANES), lambda g, c: (g, cidx(c), 0)), g128, g128, s["pgw"], s["pgw"]),
        out_shape=(jax.ShapeDtypeStruct((T, SSD_INNER), f32), jax.ShapeDtypeStruct((T, SSD_G * SSD_N), f32),
                   jax.ShapeDtypeStruct((T, SSD_G * SSD_N), f32), jax.ShapeDtypeStruct((T, SSD_INNER), bf16),
                   jax.ShapeDtypeStruct((SSD_G, T, LANES), f32), jax.ShapeDtypeStruct((SSD_G, 1, LANES), f32),
                   jax.ShapeDtypeStruct((SSD_G, 1, LANES), f32), jax.ShapeDtypeStruct((1, SSD_INNER), f32),
                   jax.ShapeDtypeStruct((1, SSD_INNER), f32)),
        scratch_shapes=[pltpu.VMEM((SSD_N, SSD_GW), f32)], name=name, compiler_params=_params(2))(
            xbc, xbc, xbc, proj, proj, dtb, alog, dsk, ng, hs, dycat)


def _qk_norm_fn(q, k, qg, kg):
    return _rms(q, qg), _rms(k, kg)


def _sb_qknorm_fwd(proj, qg, kg, name):
    T = proj.shape[0]
    tm = _row_tile(T)
    par = pl.BlockSpec((1, HD), lambda h, i: (0, 0))
    out = pl.BlockSpec((tm, HD), lambda h, i: (i, h))
    sds = jax.ShapeDtypeStruct((T, SB_W), bf16)
    return _fn_call(_qk_norm_fn, (proj, proj, qg, kg),
                    [out, pl.BlockSpec((tm, HD), lambda h, i: (i, SB_HEADS + h)), par, par],
                    (sds, sds), (out, out), (SB_HEADS, T // tm), name)


def _sb_qknorm_bwd(proj, qg, kg, dqn, dkn, name):
    T = proj.shape[0]
    tm = _row_tile(T)
    par = pl.BlockSpec((1, HD), lambda h, i: (0, 0))
    out = pl.BlockSpec((tm, HD), lambda h, i: (i, h))
    gout = pl.BlockSpec((1, 1, HD), lambda h, i: (h, 0, 0))

    def fn(q, k, qg, kg, dqn, dkn):
        _, vjp = jax.vjp(_qk_norm_fn, q, k, qg, kg)
        dq, dk, dqg, dkg = vjp((dqn, dkn))
        return dq, dk, dqg[None], dkg[None]

    sds = jax.ShapeDtypeStruct((T, SB_W), bf16)
    gsds = jax.ShapeDtypeStruct((SB_HEADS, 1, HD), f32)
    return _fn_call(fn, (proj, proj, qg, kg, dqn, dkn),
                    [out, pl.BlockSpec((tm, HD), lambda h, i: (i, SB_HEADS + h)), par, par, out, out],
                    (sds, sds, gsds, gsds), (out, out, gout, gout), (SB_HEADS, T // tm), name, acc={2: "last", 3: "last"})


def _split_dot(a, tri):
    hi = a.astype(bf16)
    lo = (a - hi.astype(f32)).astype(bf16)
    return jnp.dot(hi, tri, preferred_element_type=f32) + jnp.dot(lo, tri, preferred_element_type=f32)


def _sb_weights(q, kblk, run, later, mask):
    z = _dg(q, kblk, 1, 1) * SB_SCALE
    sp = jax.nn.softplus(z)
    if mask is not None:
        sp = jnp.where(mask, sp, 0.0)
    w = jnp.exp(z - sp - _split_dot(sp, later) - run)
    if mask is not None:
        w = jnp.where(mask, w, 0.0)
    return z, sp, w


def _sb_fwd(qn, kn, proj, name):
    T = qn.shape[0]
    B = SB_BLK
    nq = T // B

    def body(q_ref, k_ref, v_ref, o_ref, ox_ref):
        qb = pl.program_id(1)
        q = q_ref[...]
        ri, ci = _iota((B, B), 0), _iota((B, B), 1)
        later = (ri > ci).astype(bf16)

        def block(kb, carry, mask):
            acc, acc_lo, run = carry
            off = pl.multiple_of(kb * B, B)
            _, sp, w = _sb_weights(q, k_ref[pl.ds(off, B), :], run, later, mask)
            vblk = v_ref[pl.ds(off, B), :]
            w_hi = w.astype(bf16)
            acc = acc + _dg(w_hi, vblk, 1, 0)
            acc_lo = acc_lo + _dg(w - w_hi.astype(f32), vblk, 1, 0)
            return acc, acc_lo, run + jnp.sum(sp, axis=1, keepdims=True)

        zero = jnp.zeros((B, HD), f32)
        carry = block(qb, (zero, zero, jnp.zeros((B, 1), f32)), ci < ri)
        carry = lax.fori_loop(0, qb, lambda i, cr: block(qb - 1 - i, cr, None), carry)
        o_ref[...] = carry[0].astype(o_ref.dtype)
        ox_ref[...] = carry[0] + carry[1]

    blk = pl.BlockSpec((B, HD), lambda h, i: (i, h))
    return pl.pallas_call(
        body, grid=(SB_HEADS, nq),
        in_specs=[blk, pl.BlockSpec((T, HD), lambda h, i: (0, h)), pl.BlockSpec((T, HD), lambda h, i: (0, 2 * SB_HEADS + h))],
        out_specs=(blk, blk), out_shape=(jax.ShapeDtypeStruct((T, SB_W), bf16), jax.ShapeDtypeStruct((T, SB_W), f32)),
        name=name, compiler_params=_params(2))(qn, kn, proj)


def _sb_bwd(qn, kn, proj, o, dycat, name):
    T = qn.shape[0]
    B = SB_BLK
    nq = T // B

    def body(q_ref, k_ref, v_ref, o_ref, do_ref, dq_ref, dk_ref, dv_ref):
        qb = pl.program_id(1)

        @pl.when(qb == 0)
        def _():
            dk_ref[...] = jnp.zeros_like(dk_ref)
            dv_ref[...] = jnp.zeros_like(dv_ref)

        q = q_ref[...]
        do = do_ref[...].astype(f32)
        do_b = do.astype(bf16)
        gtot = jnp.sum(do_b.astype(f32) * o_ref[...], axis=1, keepdims=True)
        ri, ci = _iota((B, B), 0), _iota((B, B), 1)
        later = (ri > ci).astype(bf16)
        from_here = (ri >= ci).astype(bf16)

        def block(kb, carry, mask):
            dq, run, rung = carry
            off = pl.multiple_of(kb * B, B)
            kblk = k_ref[pl.ds(off, B), :]
            z, sp, w = _sb_weights(q, kblk, run, later, mask)
            g = w * _dg(do_b, v_ref[pl.ds(off, B), :], 1, 1)
            before = gtot - rung - _split_dot(g, from_here)
            sig = jax.nn.sigmoid(z)
            dz = (g * (1.0 - sig) - sig * before) * SB_SCALE
            if mask is not None:
                dz = jnp.where(mask, dz, 0.0)
            dz_b = dz.astype(bf16)
            dv_ref[pl.ds(off, B), :] += _dg(w, do_b, 0, 0)
            dk_ref[pl.ds(off, B), :] += _dg(dz_b, q, 0, 0)
            dq = dq + _dg(dz_b, kblk, 1, 0)
            return dq, run + jnp.sum(sp, axis=1, keepdims=True), rung + jnp.sum(g, axis=1, keepdims=True)

        zero = jnp.zeros((B, 1), f32)
        carry = block(qb, (jnp.zeros((B, HD), f32), zero, zero), ci < ri)
        carry = lax.fori_loop(0, qb, lambda i, cr: block(qb - 1 - i, cr, None), carry)
        dq_ref[...] = carry[0]

    blk = pl.BlockSpec((B, HD), lambda h, i: (i, h))
    full = pl.BlockSpec((T, HD), lambda h, i: (0, h))
    sds = jax.ShapeDtypeStruct((T, SB_W), f32)
    return pl.pallas_call(
        body, grid=(SB_HEADS, nq),
        in_specs=[blk, full, pl.BlockSpec((T, HD), lambda h, i: (0, 2 * SB_HEADS + h)), blk, blk],
        out_specs=(blk, full, full), out_shape=(sds, sds, sds), name=name, compiler_params=_params(2))(
            qn, kn, proj, o, dycat)


_BIG = (("mem_w_kv", (4, 128, 1024), 1), ("ssd_w_in", (2, 1024, 579), 2), ("ssd_w_out", (2, 256, 1024), 1),
        ("sb_w_in", (2, 1024, 640), 2), ("sb_w_out", (2, 256, 1024), 1), ("ffn_w_gate_up", (4, 1024, 704), 2),
        ("ffn_w_down", (4, 352, 1024), 1))
_BIG_ROWS = tuple(math.prod(s) // LANES for _, s, _ in _BIG)
_BIG_TOTAL = 66560
_ADAM_TILE = 1024


def _pack_rows(parts, total):
    rows = sum(p.shape[-2] for p in parts)
    pad = jnp.zeros(parts[0].shape[:-2] + (total - rows, LANES), parts[0].dtype)
    return jnp.concatenate(list(parts) + [pad], axis=-2)


def _full_from_slots(slots, shard_shape, axis):
    n = shard_shape[0]
    s = slots.reshape((N_DEV,) + shard_shape)
    if axis == 1:
        return s.transpose(1, 0, 2, 3).reshape(n, N_DEV * shard_shape[1], shard_shape[2])
    return s.transpose(1, 2, 0, 3).reshape(n, shard_shape[1], N_DEV * shard_shape[2])


def _slots_from_full(full, shard_shape, axis):
    n = shard_shape[0]
    if axis == 1:
        s = full.reshape(n, N_DEV, shard_shape[1], shard_shape[2]).transpose(1, 0, 2, 3)
    else:
        s = full.reshape(n, shard_shape[1], N_DEV, shard_shape[2]).transpose(2, 0, 1, 3)
    return s.reshape(N_DEV, -1, LANES)


def _ssd_in_cols(w):
    pad = jnp.zeros(w.shape[:-1] + (SSD_IN_PAD - SSD_IN,), w.dtype)
    return jnp.concatenate([w[..., :4096], w[..., 4120:4632], w[..., 4096:4120], pad], axis=-1)


def _ssd_in_cols_back(w):
    return jnp.concatenate([w[..., :4096], w[..., 4608:4632], w[..., 4096:4608]], axis=-1)


def _lane_rows(a):
    flat = a.reshape(-1)
    n = -(-flat.shape[0] // (8 * LANES)) * (8 * LANES)
    return jnp.pad(flat, (0, n - flat.shape[0])).reshape(-1, LANES)


def _pad128(a):
    return jnp.pad(a, ((0, 0), (0, LANES - a.shape[1])))


def kernel(x, mem, mix_norm_g, ffn_norm_g, mem_norm_g, mem_w_kv, mem_q_norm_g, mem_k_norm_g, ssd_w_in, ssd_conv_w, ssd_conv_b, ssd_dt_bias, ssd_a_log, ssd_d, ssd_norm_g, ssd_w_out, sb_w_in, sb_q_norm_g, sb_k_norm_g, sb_w_out, ffn_w_gate_up, ffn_w_down, loss_target, m_mix_norm_g, m_ffn_norm_g, m_mem_norm_g, m_mem_w_kv, m_mem_q_norm_g, m_mem_k_norm_g, m_ssd_w_in, m_ssd_conv_w, m_ssd_conv_b, m_ssd_dt_bias, m_ssd_a_log, m_ssd_d, m_ssd_norm_g, m_ssd_w_out, m_sb_w_in, m_sb_q_norm_g, m_sb_k_norm_g, m_sb_w_out, m_ffn_w_gate_up, m_ffn_w_down, v_mix_norm_g, v_ffn_norm_g, v_mem_norm_g, v_mem_w_kv, v_mem_q_norm_g, v_mem_k_norm_g, v_ssd_w_in, v_ssd_conv_w, v_ssd_conv_b, v_ssd_dt_bias, v_ssd_a_log, v_ssd_d, v_ssd_norm_g, v_ssd_w_out, v_sb_w_in, v_sb_q_norm_g, v_sb_k_norm_g, v_sb_w_out, v_ffn_w_gate_up, v_ffn_w_down):
    W = dict(mix_norm_g=mix_norm_g, ffn_norm_g=ffn_norm_g, mem_norm_g=mem_norm_g, mem_w_kv=mem_w_kv, mem_q_norm_g=mem_q_norm_g, mem_k_norm_g=mem_k_norm_g, ssd_w_in=ssd_w_in, ssd_conv_w=ssd_conv_w, ssd_conv_b=ssd_conv_b, ssd_dt_bias=ssd_dt_bias, ssd_a_log=ssd_a_log, ssd_d=ssd_d, ssd_norm_g=ssd_norm_g, ssd_w_out=ssd_w_out, sb_w_in=sb_w_in, sb_q_norm_g=sb_q_norm_g, sb_k_norm_g=sb_k_norm_g, sb_w_out=sb_w_out, ffn_w_gate_up=ffn_w_gate_up, ffn_w_down=ffn_w_down)
    M = dict(mix_norm_g=m_mix_norm_g, ffn_norm_g=m_ffn_norm_g, mem_norm_g=m_mem_norm_g, mem_w_kv=m_mem_w_kv, mem_q_norm_g=m_mem_q_norm_g, mem_k_norm_g=m_mem_k_norm_g, ssd_w_in=m_ssd_w_in, ssd_conv_w=m_ssd_conv_w, ssd_conv_b=m_ssd_conv_b, ssd_dt_bias=m_ssd_dt_bias, ssd_a_log=m_ssd_a_log, ssd_d=m_ssd_d, ssd_norm_g=m_ssd_norm_g, ssd_w_out=m_ssd_w_out, sb_w_in=m_sb_w_in, sb_q_norm_g=m_sb_q_norm_g, sb_k_norm_g=m_sb_k_norm_g, sb_w_out=m_sb_w_out, ffn_w_gate_up=m_ffn_w_gate_up, ffn_w_down=m_ffn_w_down)
    V = dict(mix_norm_g=v_mix_norm_g, ffn_norm_g=v_ffn_norm_g, mem_norm_g=v_mem_norm_g, mem_w_kv=v_mem_w_kv, mem_q_norm_g=v_mem_q_norm_g, mem_k_norm_g=v_mem_k_norm_g, ssd_w_in=v_ssd_w_in, ssd_conv_w=v_ssd_conv_w, ssd_conv_b=v_ssd_conv_b, ssd_dt_bias=v_ssd_dt_bias, ssd_a_log=v_ssd_a_log, ssd_d=v_ssd_d, ssd_norm_g=v_ssd_norm_g, ssd_w_out=v_ssd_w_out, sb_w_in=v_sb_w_in, sb_q_norm_g=v_sb_q_norm_g, sb_k_norm_g=v_sb_k_norm_g, sb_w_out=v_sb_w_out, ffn_w_gate_up=v_ffn_w_gate_up, ffn_w_down=v_ffn_w_down)
    names = list(W)
    T = x.shape[1]
    x0 = x.reshape(T, D)
    mem2 = mem.reshape(MEM_LEN, D)
    target = loss_target.reshape(T, D)
    my_dev = 4 * lax.axis_index("x") + 2 * lax.axis_index("y") + lax.axis_index("c")

    w_flat = _pack_rows([W[n].reshape(-1, LANES) for n, _, _ in _BIG], _BIG_TOTAL)
    slots = _allgather_hbm(w_flat.astype(bf16), "allgather_weights")
    full, off = {}, 0
    for (n, shp, ax), rows in zip(_BIG, _BIG_ROWS):
        full[n] = _full_from_slots(slots[:, off:off + rows], shp, ax)
        off += rows
    full["ssd_w_in"] = _ssd_in_cols(full["ssd_w_in"])
    conv_slots = _allgather_vmem(_lane_rows(ssd_conv_w), "allgather_conv_w")
    conv_w = _full_from_slots(conv_slots[:, :20], (2, 4, 320), 2)

    mem_g = mem_norm_g.reshape(1, D)

    saved = []
    xc = x0
    for i in range(DEPTH):
        j = i // 2
        ssd = i % 2 == 0
        L = f"l{i}_"
        mix_g = mix_norm_g[i:i + 1]
        h = _rmsnorm_fwd(xc, mix_g, L + "mix_norm")
        w_in = full["ssd_w_in"][j] if ssd else full["sb_w_in"][j]
        proj = _matmul(h, w_in, tm=256 if ssd else 512, tn=w_in.shape[1] if ssd else 512, tk=D, name=L + "in_proj")
        k_mem, v_mem = _memkv_fwd(mem2, mem_g, full["mem_w_kv"][i], mem_k_norm_g[i:i + 1], L + "mem_kv")
        q_col = 32 if ssd else 36
        o_mem = _memattn_fwd(proj, q_col, k_mem, v_mem, mem_q_norm_g[i:i + 1], L + "mem_attn")
        st = dict(x_in=xc, h=h, proj=proj, k_mem=k_mem, v_mem=v_mem)
        if ssd:
            xbc = _conv_fwd(proj, conv_w[j], ssd_conv_b[j:j + 1], L + "conv")
            dtb, alog = _pad128(ssd_dt_bias[j:j + 1]), _pad128(ssd_a_log[j:j + 1])
            dsk = jnp.repeat(ssd_d[j], SSD_P).reshape(1, SSD_INNER)
            y, hs = _ssd_fwd(xbc, proj, dtb, alog, dsk, ssd_norm_g[j:j + 1], L + "ssd_scan")
            st.update(xbc=xbc, hs=hs, dtb=dtb, alog=alog, dsk=dsk)
            w_out = full["ssd_w_out"][j]
        else:
            qn, kn = _sb_qknorm_fwd(proj, sb_q_norm_g[j:j + 1], sb_k_norm_g[j:j + 1], L + "qk_norm")
            y, o_exact = _sb_fwd(qn, kn, proj, L + "sb_attn")
            st.update(qn=qn, kn=kn, o=o_exact)
            w_out = full["sb_w_out"][j]
        ycat = jnp.concatenate([y, o_mem], axis=1)
        x_mid = _matmul(ycat, w_out, tm=512, tn=D, tk=2048, res=xc, name=L + "out_proj")
        h2 = _rmsnorm_fwd(x_mid, ffn_norm_g[i:i + 1], L + "ffn_norm")
        gu = _matmul(h2, full["ffn_w_gate_up"][i], tm=512, tn=512, tk=D, name=L + "ffn_up")
        act = _ffn_act_fwd(gu, L + "ffn_act")
        xc = _matmul(act, full["ffn_w_down"][i], tm=512, tn=D, tk=FFN_H, res=x_mid, name=L + "ffn_down")
        st.update(ycat=ycat, x_mid=x_mid, h2=h2, gu=gu, act=act, w_in=w_in, w_out=w_out)
        saved.append(st)

    dx, loss_part = _loss_head(xc, target, "loss_head")
    loss = lax.psum(jnp.sum(loss_part), ("x", "y", "c"))

    G = {n: [None] * W[n].shape[0] for n in names if W[n].ndim > 1}
    d_mem_g = jnp.zeros((1, D), f32)
    for i in reversed(range(DEPTH)):
        j = i // 2
        ssd = i % 2 == 0
        L = f"l{i}_b_"
        st = saved[i]
        proj = st["proj"]
        dact = _matmul(dx, full["ffn_w_down"][i], tb=True, tm=512, tn=FFN_H, tk=D, name=L + "d_act")
        G["ffn_w_down"][i] = _matmul(st["act"], dx, ta=True, tm=256, tn=D, tk=512, name=L + "dw_down")
        dgu = _ffn_act_bwd(st["gu"], dact, L + "d_gu")
        dh2 = _matmul(dgu, full["ffn_w_gate_up"][i], tb=True, tm=512, tn=D, tk=2 * FFN_H, name=L + "d_h2")
        G["ffn_w_gate_up"][i] = _matmul(st["h2"], dgu, ta=True, tm=D, tn=512, tk=512, name=L + "dw_up")
        dx, G["ffn_norm_g"][i] = _rmsnorm_bwd(st["x_mid"], ffn_norm_g[i:i + 1], dh2, dx, L + "d_ffn_norm")
        dycat = _matmul(dx, st["w_out"], tb=True, tm=512, tn=2048, tk=D, name=L + "d_ycat")
        g_out = _matmul(st["ycat"], dx, ta=True, tm=512, tn=D, tk=512, name=L + "dw_out")
        q_col = 32 if ssd else 36
        dq_mem, dk_mem, dv_mem, dqg = _memattn_bwd(proj, q_col, st["k_mem"], st["v_mem"], mem_q_norm_g[i:i + 1], dycat, 12, L + "d_mem_attn")
        G["mem_q_norm_g"][i] = jnp.sum(dqg, axis=0)
        dmg, G["mem_w_kv"][i], G["mem_k_norm_g"][i] = _memkv_bwd(mem2, mem_g, full["mem_w_kv"][i], mem_k_norm_g[i:i + 1], dk_mem, dv_mem, L + "d_mem_kv")
        d_mem_g = d_mem_g + dmg
        if ssd:
            G["ssd_w_out"][j] = g_out
            dxs, dbm, dcm, dz, ddt, ddtb, dalog, ddsk, dng = _ssd_bwd(
                st["xbc"], proj, st["dtb"], st["alog"], st["dsk"], ssd_norm_g[j:j + 1], st["hs"], dycat, L + "d_ssd_scan")
            G["ssd_dt_bias"][j] = jnp.sum(ddtb, axis=0)[:, :SSD_HEADS]
            G["ssd_a_log"][j] = jnp.sum(dalog, axis=0)[:, :SSD_HEADS]
            G["ssd_d"][j] = jnp.sum(ddsk.reshape(SSD_HEADS, SSD_P), axis=1).reshape(1, SSD_HEADS)
            G["ssd_norm_g"][j] = dng
            dxbc_act = jnp.concatenate([dxs, dbm, dcm], axis=1)
            dpre, G["ssd_conv_w"][j], G["ssd_conv_b"][j] = _conv_bwd_pre(proj, conv_w[j], ssd_conv_b[j:j + 1], dxbc_act, L + "d_conv_pre")
            dxbc = _conv_bwd_in(dpre, conv_w[j], L + "d_conv_in")
            ddt_all = jnp.sum(ddt, axis=0).astype(bf16)
            dproj = jnp.concatenate([dz, dxbc, dq_mem, ddt_all], axis=1)
        else:
            G["sb_w_out"][j] = g_out
            dqn, dkn, dv = _sb_bwd(st["qn"], st["kn"], proj, st["o"], dycat, L + "d_sb_attn")
            dq, dk, dqg2, dkg2 = _sb_qknorm_bwd(proj, sb_q_norm_g[j:j + 1], sb_k_norm_g[j:j + 1], dqn, dkn, L + "d_qk_norm")
            G["sb_q_norm_g"][j] = jnp.sum(dqg2, axis=0)
            G["sb_k_norm_g"][j] = jnp.sum(dkg2, axis=0)
            dproj = jnp.concatenate([dq, dk, dv.astype(bf16), dq_mem], axis=1)
        n_in = dproj.shape[1]
        dh = _matmul(dproj, st["w_in"], tb=True, tm=512, tn=D, tk=n_in, name=L + "d_h")
        g_in = _matmul(st["h"], dproj, ta=True, tm=256 if ssd else D, tn=n_in if ssd else 512, tk=512, name=L + "dw_in")
        if ssd:
            G["ssd_w_in"][j] = _ssd_in_cols_back(g_in)
        else:
            G["sb_w_in"][j] = g_in
        dx, G["mix_norm_g"][i] = _rmsnorm_bwd(st["x_in"], mix_norm_g[i:i + 1], dh, dx, L + "d_mix_norm")

    grad_x = dx.reshape(x.shape)

    g_slots = _pack_rows([_slots_from_full(jnp.stack(G[n]), shp, ax).astype(bf16) for n, shp, ax in _BIG], _BIG_TOTAL)
    parts = _exchange_slices(g_slots, "exchange_grads")
    m_flat = _pack_rows([M[n].reshape(-1, LANES) for n, _, _ in _BIG], _BIG_TOTAL)
    v_flat = _pack_rows([V[n].reshape(-1, LANES) for n, _, _ in _BIG], _BIG_TOTAL)
    res_big = _adamw_reduce(parts, w_flat, m_flat, v_flat, _ADAM_TILE, "adamw_big")
    out = {}
    off = 0
    for (n, shp, _), rows in zip(_BIG, _BIG_ROWS):
        out[n] = tuple(r[off:off + rows].reshape(shp) for r in res_big)
        off += rows

    small = [n for n in names if n not in out and n != "ssd_conv_w"]
    G["mem_norm_g"] = d_mem_g.reshape(D)
    small_grads = [_lane_rows(G[n] if n == "mem_norm_g" else jnp.concatenate(G[n], axis=0)) for n in small]
    conv_grad = _lane_rows(jnp.stack(G["ssd_conv_w"]))
    sm_rows = [g.shape[0] for g in small_grads]
    n_small = sum(sm_rows)
    sm_total = -(-(n_small + conv_grad.shape[0]) // 8) * 8
    gathered = _allgather_vmem(_pack_rows(small_grads + [conv_grad], sm_total), "allgather_small_grads")
    whole = lambda s: pl.BlockSpec(s, lambda i: (0,) * len(s))
    g_sum = _fn_call(_sum_slots, (gathered,), [whole((N_DEV, sm_total, LANES))],
                     jax.ShapeDtypeStruct((sm_total, LANES), f32), whole((sm_total, LANES)), (1,), "sum_small_grads")
    conv_full = g_sum[n_small:n_small + 160].reshape(2, 4, SSD_CONV_DIM)
    conv_mine = lax.dynamic_slice_in_dim(conv_full, my_dev * 320, 320, axis=2)
    ad_total = n_small + 24
    pack = lambda d: _pack_rows([_lane_rows(d[n]) for n in small] + [_lane_rows(d["ssd_conv_w"])], ad_total)
    g_pack = _pack_rows([g_sum[:n_small], _lane_rows(conv_mine)], ad_total)
    blk = whole((ad_total, LANES))
    res_small = _fn_call(lambda g, w, m, v: _adamw_math(w, g, m, v), (g_pack, pack(W), pack(M), pack(V)), [blk] * 4,
                         (jax.ShapeDtypeStruct((ad_total, LANES), f32),) * 3, (blk,) * 3, (1,), "adamw_small")
    res_small = (g_pack,) + tuple(res_small)
    off = 0
    for n, rows in zip(small + ["ssd_conv_w"], sm_rows + [24]):
        size = W[n].size
        out[n] = tuple(r[off:off + rows].reshape(-1)[:size].reshape(W[n].shape) for r in res_small)
        off += rows

    return (loss, grad_x, *[out[n][0] for n in names], *[out[n][1] for n in names],
            *[out[n][2] for n in names], *[out[n][3] for n in names])
```

```python
import functools
import math

import jax
import jax.numpy as jnp
from jax import lax
from jax.experimental import pallas as pl
from jax.experimental.pallas import tpu as pltpu

f32, bf16 = jnp.float32, jnp.bfloat16
MESH = pl.DeviceIdType.MESH

N_DEV = 8
D = 1024
DEPTH = 4
EPS = 1e-6
MEM_LEN, MEM_HEADS, MEM_W, HD = 256, 4, 512, 128
SSD_INNER, SSD_HEADS, SSD_G, SSD_P, SSD_N, SSD_L = 1536, 24, 4, 64, 128, 128
SSD_GW = SSD_INNER // SSD_G
SSD_CONV_DIM = 2560
SSD_IN = 4632
SSD_IN_PAD = 4736
SB_W, SB_HEADS, SB_IN = 1536, 12, 5120
SB_BLK = 256
SB_SCALE = HD ** -0.5
FFN_H = 2816
LANES = 128
VMEM_LIMIT = 56 * 1024 * 1024

ADAM_LR, ADAM_B1, ADAM_B2, ADAM_EPS, ADAM_WD, ADAM_STEP = 0.001, 0.9, 0.999, 1e-08, 0.01, 10

HIGHEST = lax.Precision.HIGHEST


def _params(n_grid):
    return pltpu.CompilerParams(dimension_semantics=("arbitrary",) * n_grid, vmem_limit_bytes=VMEM_LIMIT)


def _dg(a, b, ca, cb):
    return lax.dot_general(a.astype(bf16), b.astype(bf16), (((ca,), (cb,)), ((), ())), preferred_element_type=f32)


@jax.custom_vjp
def bdot_nn(a, b):
    return _dg(a, b, 1, 0)


def _nn_fwd(a, b):
    return _dg(a, b, 1, 0), (a, b)


def _nn_bwd(res, ct):
    a, b = res
    return _dg(ct, b, 1, 1).astype(a.dtype), _dg(a, ct, 0, 0).astype(b.dtype)


bdot_nn.defvjp(_nn_fwd, _nn_bwd)


@jax.custom_vjp
def bdot_nt(a, b):
    return _dg(a, b, 1, 1)


def _nt_fwd(a, b):
    return _dg(a, b, 1, 1), (a, b)


def _nt_bwd(res, ct):
    a, b = res
    return _dg(ct, b, 1, 0).astype(a.dtype), _dg(ct, a, 0, 0).astype(b.dtype)


bdot_nt.defvjp(_nt_fwd, _nt_bwd)


@jax.custom_vjp
def bdot_tn(a, b):
    return _dg(a, b, 0, 0)


def _tn_fwd(a, b):
    return _dg(a, b, 0, 0), (a, b)


def _tn_bwd(res, ct):
    a, b = res
    return _dg(b, ct, 1, 1).astype(a.dtype), _dg(a, ct, 1, 0).astype(b.dtype)


bdot_tn.defvjp(_tn_fwd, _tn_bwd)


def _rms(x, g):
    return x * lax.rsqrt(jnp.mean(x * x, axis=-1, keepdims=True) + EPS) * g


def _iota(shape, axis):
    return lax.broadcasted_iota(jnp.int32, shape, axis)


def _fn_call(fn, args, in_specs, out_shapes, out_specs, grid, name, acc=None):
    n_in = len(args)
    acc = acc or {}
    n_grid = len(grid)

    def body(*refs):
        ins, outs = refs[:n_in], refs[n_in:]
        res = fn(*[r[...] for r in ins])
        if not isinstance(res, (tuple, list)):
            res = (res,)
        for k, (o, r) in enumerate(zip(outs, res)):
            mode = acc.get(k)
            if mode is None:
                o[...] = r.astype(o.dtype)
                continue
            if mode == "last":
                first = pl.program_id(n_grid - 1) == 0
            else:
                first = functools.reduce(jnp.logical_and, [pl.program_id(d) == 0 for d in range(n_grid)])

            @pl.when(first)
            def _(o=o, r=r):
                o[...] = r.astype(o.dtype)

            @pl.when(jnp.logical_not(first))
            def _(o=o, r=r):
                o[...] += r.astype(o.dtype)

    return pl.pallas_call(
        body, grid=grid, in_specs=in_specs, out_specs=out_specs, out_shape=out_shapes, name=name,
        compiler_params=_params(n_grid))(*args)


def _matmul(a, b, *, ta=False, tb=False, out_dtype=f32, tm, tn, tk, res=None, name):
    M, K = (a.shape[1], a.shape[0]) if ta else a.shape
    N = b.shape[0] if tb else b.shape[1]
    tm, tn, tk = min(tm, M), min(tn, N), min(tk, K)
    assert M % tm == 0 and N % tn == 0 and K % tk == 0, (name, M, N, K, tm, tn, tk)
    nk = K // tk
    a_spec = pl.BlockSpec((tk, tm), lambda i, j, k: (k, i)) if ta else pl.BlockSpec((tm, tk), lambda i, j, k: (i, k))
    b_spec = pl.BlockSpec((tn, tk), lambda i, j, k: (j, k)) if tb else pl.BlockSpec((tk, tn), lambda i, j, k: (k, j))
    o_spec = pl.BlockSpec((tm, tn), lambda i, j, k: (i, j))
    ca, cb = (0 if ta else 1), (1 if tb else 0)

    def body(*refs):
        if res is None:
            a_ref, b_ref, o_ref, acc_ref = refs
            r_ref = None
        else:
            a_ref, b_ref, r_ref, o_ref, acc_ref = refs
        k = pl.program_id(2)
        part = _dg(a_ref[...], b_ref[...], ca, cb)

        @pl.when(k == 0)
        def _():
            acc_ref[...] = part

        @pl.when(k > 0)
        def _():
            acc_ref[...] += part

        @pl.when(k == nk - 1)
        def _():
            out = acc_ref[...]
            if r_ref is not None:
                out = out + r_ref[...].astype(f32)
            o_ref[...] = out.astype(o_ref.dtype)

    args = (a, b) if res is None else (a, b, res)
    in_specs = [a_spec, b_spec] + ([] if res is None else [o_spec])
    return pl.pallas_call(
        body, grid=(M // tm, N // tn, nk), in_specs=in_specs, out_specs=o_spec,
        out_shape=jax.ShapeDtypeStruct((M, N), out_dtype), name=name,
        scratch_shapes=[pltpu.VMEM((tm, tn), f32)], compiler_params=_params(3))(*args)


def _row_tile(T):
    return min(T, 512)


def _my_pos():
    return lax.axis_index("x"), lax.axis_index("y"), lax.axis_index("c")


def _allgather_hbm(xs, name):
    R, C = xs.shape

    def body(x_ref, out_ref, send_sems, recv_sems, local_sem):
        x, y, c = _my_pos()
        me, sibling = (x, y, c), (x, y, 1 - c)
        chips = [(1 - x, y), (x, 1 - y), (1 - x, 1 - y)]

        def slot(px, py, pc):
            return out_ref.at[4 * px + 2 * py + pc]

        def copy(k, block, to, src=None):
            return pltpu.make_async_remote_copy(
                src_ref=slot(*block) if src is None else src, dst_ref=slot(*block),
                send_sem=send_sems.at[k], recv_sem=recv_sems.at[k], device_id=to, device_id_type=MESH)

        mine = pltpu.make_async_copy(x_ref, slot(*me), local_sem)
        mine.start()
        first = [copy(0, me, sibling, src=x_ref)]
        first += [copy(1 + j, me, (*chip, c), src=x_ref) for j, chip in enumerate(chips)]
        for cp in first:
            cp.start()
        passed = [copy(4 + j, (*chip, c), sibling) for j, chip in enumerate(chips)]
        for j, chip in enumerate(chips):
            copy(1 + j, (*chip, c), me).wait_recv()
            passed[j].start()
        copy(0, sibling, me).wait_recv()
        for j, chip in enumerate(chips):
            copy(4 + j, (*chip, 1 - c), me).wait_recv()
        for cp in first + passed:
            cp.wait_send()
        mine.wait()

    return pl.pallas_call(
        body, out_shape=jax.ShapeDtypeStruct((N_DEV, R, C), xs.dtype),
        in_specs=[pl.BlockSpec(memory_space=pl.ANY)], out_specs=pl.BlockSpec(memory_space=pl.ANY),
        scratch_shapes=[pltpu.SemaphoreType.DMA((7,)), pltpu.SemaphoreType.DMA((7,)), pltpu.SemaphoreType.DMA],
        name=name)(xs)


def _allgather_vmem(xs, name):
    R, C = xs.shape

    def body(x_ref, out_ref, send_sems, recv_sems):
        x, y, c = _my_pos()
        me = 4 * x + 2 * y + c
        out_ref[me] = x_ref[...]
        copies = []
        for k in range(1, N_DEV):
            px = 1 - x if k & 4 else x
            py = 1 - y if k & 2 else y
            pc = 1 - c if k & 1 else c
            cp = pltpu.make_async_remote_copy(
                src_ref=x_ref, dst_ref=out_ref.at[me], send_sem=send_sems.at[k - 1], recv_sem=recv_sems.at[k - 1],
                device_id=(px, py, pc), device_id_type=MESH)
            cp.start()
            copies.append(cp)
        for cp in copies:
            cp.wait()

    return pl.pallas_call(
        body, out_shape=jax.ShapeDtypeStruct((N_DEV, R, C), xs.dtype),
        in_specs=[pl.BlockSpec(memory_space=pltpu.VMEM)], out_specs=pl.BlockSpec(memory_space=pltpu.VMEM),
        scratch_shapes=[pltpu.SemaphoreType.DMA((7,)), pltpu.SemaphoreType.DMA((7,))], name=name)(xs)


def _exchange_slices(g, name):
    _, R, C = g.shape

    def body(g_ref, out_ref, send_sems, recv_sems, local_sem):
        x, y, c = _my_pos()
        me = 4 * x + 2 * y + c
        mine = pltpu.make_async_copy(g_ref.at[me], out_ref.at[me], local_sem)
        mine.start()
        copies = []
        for k in range(1, N_DEV):
            px = 1 - x if k & 4 else x
            py = 1 - y if k & 2 else y
            pc = 1 - c if k & 1 else c
            cp = pltpu.make_async_remote_copy(
                src_ref=g_ref.at[4 * px + 2 * py + pc], dst_ref=out_ref.at[me],
                send_sem=send_sems.at[k - 1], recv_sem=recv_sems.at[k - 1],
                device_id=(px, py, pc), device_id_type=MESH)
            cp.start()
            copies.append(cp)
        for cp in copies:
            cp.wait()
        mine.wait()

    return pl.pallas_call(
        body, out_shape=jax.ShapeDtypeStruct(g.shape, g.dtype),
        in_specs=[pl.BlockSpec(memory_space=pl.ANY)], out_specs=pl.BlockSpec(memory_space=pl.ANY),
        scratch_shapes=[pltpu.SemaphoreType.DMA((7,)), pltpu.SemaphoreType.DMA((7,)), pltpu.SemaphoreType.DMA],
        name=name)(g)


def _adamw_math(w, g, m, v):
    m = ADAM_B1 * m + (1.0 - ADAM_B1) * g
    v = ADAM_B2 * v + (1.0 - ADAM_B2) * jnp.square(g)
    m_hat = m / (1.0 - ADAM_B1 ** ADAM_STEP)
    v_hat = v / (1.0 - ADAM_B2 ** ADAM_STEP)
    delta = -ADAM_LR * (m_hat / (jnp.sqrt(v_hat) + ADAM_EPS) + ADAM_WD * w)
    return delta, m, v


def _sum_slots(parts):
    g = parts[0].astype(f32)
    for i in range(1, N_DEV):
        g = g + parts[i].astype(f32)
    return g


def _adamw_reduce(parts, w, m, v, tr, name):
    R = w.shape[0]
    assert R % tr == 0

    def fn(p, w, m, v):
        g = _sum_slots(p)
        return (g,) + _adamw_math(w, g, m, v)

    row = pl.BlockSpec((tr, LANES), lambda i: (i, 0))
    sds = jax.ShapeDtypeStruct((R, LANES), f32)
    return _fn_call(fn, (parts, w, m, v), [pl.BlockSpec((N_DEV, tr, LANES), lambda i: (0, i, 0)), row, row, row],
                    (sds,) * 4, (row,) * 4, (R // tr,), name)


def _rmsnorm_fwd(x, g, name):
    T = x.shape[0]
    tm = _row_tile(T)
    row = pl.BlockSpec((tm, D), lambda i: (i, 0))
    par = pl.BlockSpec((1, D), lambda i: (0, 0))
    return _fn_call(lambda x, g: _rms(x, g), (x, g), [row, par], jax.ShapeDtypeStruct((T, D), bf16), row, (T // tm,), name)


def _rmsnorm_bwd(x, g, dh, dres, name):
    T = x.shape[0]
    tm = _row_tile(T)
    row = pl.BlockSpec((tm, D), lambda i: (i, 0))
    par = pl.BlockSpec((1, D), lambda i: (0, 0))

    def fn(x, g, dh, dres):
        _, vjp = jax.vjp(_rms, x, g)
        dx, dg = vjp(dh.astype(f32))
        return dx + dres, dg

    return _fn_call(fn, (x, g, dh, dres), [row, par, row, row],
                    (jax.ShapeDtypeStruct((T, D), f32), jax.ShapeDtypeStruct((1, D), f32)), (row, par),
                    (T // tm,), name, acc={1: "all"})


def _swiglu_act(g, u):
    return jax.nn.silu(g) * u


def _ffn_act_fwd(gu, name):
    T = gu.shape[0]
    tm, tc = _row_tile(T), 256
    nc = FFN_H // tc
    return _fn_call(_swiglu_act, (gu, gu),
                    [pl.BlockSpec((tm, tc), lambda i, j: (i, j)), pl.BlockSpec((tm, tc), lambda i, j: (i, nc + j))],
                    jax.ShapeDtypeStruct((T, FFN_H), bf16), pl.BlockSpec((tm, tc), lambda i, j: (i, j)),
                    (T // tm, nc), name)


def _ffn_act_bwd(gu, dact, name):
    T = gu.shape[0]
    tm, tc = _row_tile(T), 256
    nc = FFN_H // tc

    def fn(g, u, da):
        _, vjp = jax.vjp(_swiglu_act, g, u)
        return vjp(da.astype(f32))

    lo = pl.BlockSpec((tm, tc), lambda i, j: (i, j))
    hi = pl.BlockSpec((tm, tc), lambda i, j: (i, nc + j))
    dg, du = _fn_call(fn, (gu, gu, dact), [lo, hi, lo],
                      (jax.ShapeDtypeStruct((T, FFN_H), bf16),) * 2, (lo, lo), (T // tm, nc), name)
    return jnp.concatenate([dg, du], axis=1)


def _loss_head(x, target, name):
    T = x.shape[0]
    tm = _row_tile(T)
    row = pl.BlockSpec((tm, D), lambda i: (i, 0))
    par = pl.BlockSpec((1, LANES), lambda i: (0, 0))

    def fn(x, t):
        e = x - t
        s = jnp.sum(e * e, axis=0, keepdims=True)
        part = s[:, 0:LANES]
        for k in range(1, D // LANES):
            part = part + s[:, k * LANES:(k + 1) * LANES]
        return e * (1.0 / D), part * (0.5 / D)

    return _fn_call(fn, (x, target), [row, row],
                    (jax.ShapeDtypeStruct((T, D), f32), jax.ShapeDtypeStruct((1, LANES), f32)), (row, par),
                    (T // tm,), name, acc={1: "all"})


def _memkv_fn(mem, mg, wkv, kg):
    mn = _rms(mem, mg)
    kv = bdot_nn(mn, wkv)
    ks = [_rms(kv[:, h * HD:(h + 1) * HD], kg) for h in range(MEM_HEADS)]
    return jnp.concatenate(ks, axis=1), kv[:, MEM_W:]


def _memkv_fwd(mem, mg, wkv, kg, name):
    whole = lambda s: pl.BlockSpec(s, lambda i: (0,) * len(s))
    sds = jax.ShapeDtypeStruct((MEM_LEN, MEM_W), f32)
    return _fn_call(lambda m, g, w, k: _memkv_fn(m, g, w.astype(f32), k), (mem, mg, wkv, kg),
                    [whole((MEM_LEN, D)), whole((1, D)), whole((D, 2 * MEM_W)), whole((1, HD))],
                    (sds, sds), (whole((MEM_LEN, MEM_W)),) * 2, (1,), name)


def _memkv_bwd(mem, mg, wkv, kg, dk, dv, name):
    whole = lambda s: pl.BlockSpec(s, lambda i: (0,) * len(s))

    def fn(m, g, w, k, dk, dv):
        _, vjp = jax.vjp(lambda g, w, k: _memkv_fn(m, g, w, k), g, w.astype(f32), k)
        return vjp((dk, dv))

    return _fn_call(fn, (mem, mg, wkv, kg, dk, dv),
                    [whole((MEM_LEN, D)), whole((1, D)), whole((D, 2 * MEM_W)), whole((1, HD)),
                     whole((MEM_LEN, MEM_W)), whole((MEM_LEN, MEM_W))],
                    (jax.ShapeDtypeStruct((1, D), f32), jax.ShapeDtypeStruct((D, 2 * MEM_W), f32),
                     jax.ShapeDtypeStruct((1, HD), f32)),
                    (whole((1, D)), whole((D, 2 * MEM_W)), whole((1, HD))), (1,), name)


def _memattn_fn(q, k, v, qg):
    qn = _rms(q, qg)
    s = bdot_nt(qn, k) * (HD ** -0.5)
    s = s - jnp.max(s, axis=-1, keepdims=True)
    p = jnp.exp(s)
    p = p / jnp.sum(p, axis=-1, keepdims=True)
    return bdot_nn(p, v)


def _memattn_fwd(proj, q_col, k, v, qg, name):
    T = proj.shape[0]
    tm = _row_tile(T)
    return _fn_call(_memattn_fn, (proj, k, v, qg),
                    [pl.BlockSpec((tm, HD), lambda h, i: (i, q_col + h)), pl.BlockSpec((MEM_LEN, HD), lambda h, i: (0, h)),
                     pl.BlockSpec((MEM_LEN, HD), lambda h, i: (0, h)), pl.BlockSpec((1, HD), lambda h, i: (0, 0))],
                    jax.ShapeDtypeStruct((T, MEM_W), bf16), pl.BlockSpec((tm, HD), lambda h, i: (i, h)),
                    (MEM_HEADS, T // tm), name)


def _memattn_bwd(proj, q_col, k, v, qg, dycat, do_col, name):
    T = proj.shape[0]
    tm = _row_tile(T)

    def fn(q, k, v, qg, do):
        _, vjp = jax.vjp(_memattn_fn, q, k, v, qg)
        dq, dk, dv, dg = vjp(do.astype(f32))
        return dq, dk, dv, dg[None]

    kv_spec = pl.BlockSpec((MEM_LEN, HD), lambda h, i: (0, h))
    kv_sds = jax.ShapeDtypeStruct((MEM_LEN, MEM_W), f32)
    return _fn_call(fn, (proj, k, v, qg, dycat),
                    [pl.BlockSpec((tm, HD), lambda h, i: (i, q_col + h)), kv_spec, kv_spec,
                     pl.BlockSpec((1, HD), lambda h, i: (0, 0)), pl.BlockSpec((tm, HD), lambda h, i: (i, do_col + h))],
                    (jax.ShapeDtypeStruct((T, MEM_W), bf16), kv_sds, kv_sds, jax.ShapeDtypeStruct((MEM_HEADS, 1, HD), f32)),
                    (pl.BlockSpec((tm, HD), lambda h, i: (i, h)), kv_spec, kv_spec,
                     pl.BlockSpec((1, 1, HD), lambda h, i: (h, 0, 0))),
                    (MEM_HEADS, T // tm), name, acc={1: "last", 2: "last", 3: "last"})


def _conv_taps(xp, w, first, tm):
    out = w[0:1, :] * xp[first:first + tm, :]
    for k in range(1, 4):
        out = out + w[k:k + 1, :] * xp[first + k:first + k + tm, :]
    return out


def _conv_blocks(T):
    tm, tc = _row_tile(T), 512
    nt = T // tm
    cur = pl.BlockSpec((tm, tc), lambda j, i: (i, 3 + j))
    prev = pl.BlockSpec((8, tc), lambda j, i: (jnp.maximum(i * (tm // 8) - 1, 0), 3 + j))
    par4 = pl.BlockSpec((4, tc), lambda j, i: (0, j))
    par1 = pl.BlockSpec((1, tc), lambda j, i: (0, j))
    out = pl.BlockSpec((tm, tc), lambda j, i: (i, j))
    return tm, tc, nt, cur, prev, par4, par1, out


def _conv_fwd(proj, w, b, name):
    T = proj.shape[0]
    tm, tc, nt, cur, prev, par4, par1, out = _conv_blocks(T)

    def body(prev_ref, cur_ref, w_ref, b_ref, o_ref):
        halo = jnp.where(pl.program_id(1) == 0, 0.0, prev_ref[...])
        xp = jnp.concatenate([halo, cur_ref[...]], axis=0)
        o_ref[...] = jax.nn.silu(_conv_taps(xp, w_ref[...], 5, tm) + b_ref[...])

    return pl.pallas_call(body, grid=(SSD_CONV_DIM // tc, nt), in_specs=[prev, cur, par4, par1], out_specs=out,
                          out_shape=jax.ShapeDtypeStruct((T, SSD_CONV_DIM), f32), name=name,
                          compiler_params=_params(2))(proj, proj, w, b)


def _conv_bwd_pre(proj, w, b, dact, name):
    T = proj.shape[0]
    tm, tc, nt, cur, prev, par4, par1, out = _conv_blocks(T)

    def body(prev_ref, cur_ref, w_ref, b_ref, da_ref, dp_ref, dw_ref, db_ref):
        i = pl.program_id(1)
        halo = jnp.where(i == 0, 0.0, prev_ref[...])
        xp = jnp.concatenate([halo, cur_ref[...]], axis=0)
        pre = _conv_taps(xp, w_ref[...], 5, tm) + b_ref[...]
        sig = jax.nn.sigmoid(pre)
        dpre = da_ref[...] * (sig * (1.0 + pre * (1.0 - sig)))
        dp_ref[...] = dpre
        dw = jnp.concatenate([jnp.sum(dpre * xp[5 + k:5 + k + tm, :], axis=0, keepdims=True) for k in range(4)], axis=0)
        db = jnp.sum(dpre, axis=0, keepdims=True)

        @pl.when(i == 0)
        def _():
            dw_ref[...] = dw
            db_ref[...] = db

        @pl.when(i > 0)
        def _():
            dw_ref[...] += dw
            db_ref[...] += db

    return pl.pallas_call(
        body, grid=(SSD_CONV_DIM // tc, nt), in_specs=[prev, cur, par4, par1, out], out_specs=(out, par4, par1),
        out_shape=(jax.ShapeDtypeStruct((T, SSD_CONV_DIM), f32), jax.ShapeDtypeStruct((4, SSD_CONV_DIM), f32),
                   jax.ShapeDtypeStruct((1, SSD_CONV_DIM), f32)),
        name=name, compiler_params=_params(2))(proj, proj, w, b, dact)


def _conv_bwd_in(dpre, w, name):
    T = dpre.shape[0]
    tm, tc, nt, _, _, par4, _, out = _conv_blocks(T)
    nxt = pl.BlockSpec((8, tc), lambda j, i: (jnp.minimum((i + 1) * (tm // 8), T // 8 - 1), j))

    def body(cur_ref, nxt_ref, w_ref, o_ref):
        halo = jnp.where(pl.program_id(1) == nt - 1, 0.0, nxt_ref[...])
        xp = jnp.concatenate([cur_ref[...], halo], axis=0)
        w = w_ref[...]
        acc = w[3:4, :] * xp[0:tm, :]
        for k in range(3):
            acc = acc + w[k:k + 1, :] * xp[3 - k:3 - k + tm, :]
        o_ref[...] = acc.astype(o_ref.dtype)

    return pl.pallas_call(body, grid=(SSD_CONV_DIM // tc, nt), in_specs=[out, nxt, par4], out_specs=out,
                          out_shape=jax.ShapeDtypeStruct((T, SSD_CONV_DIM), bf16), name=name,
                          compiler_params=_params(2))(dpre, dpre, w)


def _ssd_chunk(hbase, xs, bm, cm, z, dtr, dtb, alog, dsk, ng, ht):
    L = SSD_L
    dt = jax.nn.softplus(dtr + dtb)
    da = dt * (-jnp.exp(alog))
    li, si = _iota((L, L), 0), _iota((L, L), 1)
    causal = li >= si
    cs = jnp.dot(causal.astype(f32), da, precision=HIGHEST, preferred_element_type=f32)
    cs_t = cs.T
    expand = (_iota((LANES, SSD_GW), 0) == hbase + _iota((LANES, SSD_GW), 1) // SSD_P).astype(f32)
    cs_e = jnp.dot(cs, expand, precision=HIGHEST, preferred_element_type=f32)
    dt_e = jnp.dot(dt, expand, precision=HIGHEST, preferred_element_type=f32)
    xdt = xs * dt_e
    cb = bdot_nt(cm, bm)
    chan_head = _iota((1, SSD_GW), 1) // SSD_P
    y = jnp.zeros((L, SSD_GW), f32)
    for r in range(SSD_GW // SSD_P):
        cs_col = jnp.sum(cs * (_iota((1, LANES), 1) == hbase + r).astype(f32), axis=1, keepdims=True)
        cs_row = jnp.sum(cs_t * (_iota((LANES, 1), 0) == hbase + r).astype(f32), axis=0, keepdims=True)
        decay = jnp.where(causal, jnp.exp(jnp.where(causal, cs_col - cs_row, 0.0)), 0.0)
        y = y + bdot_nn(cb * decay, xdt * (chan_head == r).astype(f32))
    y = y + jnp.exp(cs_e) * bdot_nn(cm, ht)
    cs_last = jnp.sum(cs_e * (_iota((L, 1), 0) == L - 1).astype(f32), axis=0, keepdims=True)
    ht_new = ht * jnp.exp(cs_last) + bdot_tn(bm, xdt * jnp.exp(cs_last - cs_e))
    y = (y + dsk * xs) * jax.nn.silu(z)
    return _rms(y, ng), ht_new


def _ssd_specs(T, rev):
    nc = T // SSD_L
    cidx = (lambda c: nc - 1 - c) if rev else (lambda c: c)
    return nc, dict(
        xs=pl.BlockSpec((SSD_L, SSD_GW), lambda g, c: (cidx(c), g)),
        bm=pl.BlockSpec((SSD_L, SSD_N), lambda g, c: (cidx(c), 12 + g)),
        cm=pl.BlockSpec((SSD_L, SSD_N), lambda g, c: (cidx(c), 16 + g)),
        z=pl.BlockSpec((SSD_L, SSD_GW), lambda g, c: (cidx(c), g)),
        dt=pl.BlockSpec((SSD_L, LANES), lambda g, c: (cidx(c), 36)),
        p128=pl.BlockSpec((1, LANES), lambda g, c: (0, 0)),
        pgw=pl.BlockSpec((1, SSD_GW), lambda g, c: (0, g)),
        hs=pl.BlockSpec((None, None, SSD_N, SSD_GW), lambda g, c: (g, cidx(c), 0, 0)),
        grp=pl.BlockSpec((SSD_L, SSD_N), lambda g, c: (cidx(c), g)),
    )


def _ssd_fwd(xbc, proj, dtb, alog, dsk, ng, name):
    T = proj.shape[0]
    nc, s = _ssd_specs(T, False)

    def body(xs_ref, bm_ref, cm_ref, z_ref, dt_ref, dtb_ref, alog_ref, dsk_ref, ng_ref, y_ref, hs_ref, h_scr):
        @pl.when(pl.program_id(1) == 0)
        def _():
            h_scr[...] = jnp.zeros_like(h_scr)

        ht = h_scr[...]
        hs_ref[...] = ht
        y, ht_new = _ssd_chunk(pl.program_id(0) * (SSD_GW // SSD_P), xs_ref[...], bm_ref[...], cm_ref[...], z_ref[...],
                               dt_ref[...], dtb_ref[...], alog_ref[...], dsk_ref[...], ng_ref[...], ht)
        y_ref[...] = y.astype(y_ref.dtype)
        h_scr[...] = ht_new

    return pl.pallas_call(
        body, grid=(SSD_G, nc),
        in_specs=[s["xs"], s["bm"], s["cm"], s["z"], s["dt"], s["p128"], s["p128"], s["pgw"], s["pgw"]],
        out_specs=(s["xs"], s["hs"]),
        out_shape=(jax.ShapeDtypeStruct((T, SSD_INNER), bf16), jax.ShapeDtypeStruct((SSD_G, nc, SSD_N, SSD_GW), f32)),
        scratch_shapes=[pltpu.VMEM((SSD_N, SSD_GW), f32)], name=name, compiler_params=_params(2))(
            xbc, xbc, xbc, proj, proj, dtb, alog, dsk, ng)


def _ssd_bwd(xbc, proj, dtb, alog, dsk, ng, hs, dycat, name):
    T = proj.shape[0]
    nc, s = _ssd_specs(T, True)

    def body(xs_ref, bm_ref, cm_ref, z_ref, dt_ref, dtb_ref, alog_ref, dsk_ref, ng_ref, hs_ref, dy_ref,
             dxs_ref, dbm_ref, dcm_ref, dz_ref, ddt_ref, ddtb_ref, dalog_ref, ddsk_ref, dng_ref, dh_scr):
        c = pl.program_id(1)

        @pl.when(c == 0)
        def _():
            dh_scr[...] = jnp.zeros_like(dh_scr)

        hbase = pl.program_id(0) * (SSD_GW // SSD_P)
        _, vjp = jax.vjp(functools.partial(_ssd_chunk, hbase), xs_ref[...], bm_ref[...], cm_ref[...], z_ref[...],
                         dt_ref[...], dtb_ref[...], alog_ref[...], dsk_ref[...], ng_ref[...], hs_ref[...])
        dxs, dbm, dcm, dz, ddt, ddtb, dalog, ddsk, dng, dht = vjp((dy_ref[...].astype(f32), dh_scr[...]))
        dxs_ref[...] = dxs
        dbm_ref[...] = dbm
        dcm_ref[...] = dcm
        dz_ref[...] = dz.astype(dz_ref.dtype)
        ddt_ref[...] = ddt
        dh_scr[...] = dht

        @pl.when(c == 0)
        def _():
            ddtb_ref[...] = ddtb
            dalog_ref[...] = dalog
            ddsk_ref[...] = ddsk
            dng_ref[...] = dng

        @pl.when(c > 0)
        def _():
            ddtb_ref[...] += ddtb
            dalog_ref[...] += dalog
            ddsk_ref[...] += ddsk
            dng_ref[...] += dng

    cidx = lambda c: nc - 1 - c
    g128 = pl.BlockSpec((None, 1, LANES), lambda g, c: (g, 0, 0))
    return pl.pallas_call(
        body, grid=(SSD_G, nc),
        in_specs=[s["xs"], s["bm"], s["cm"], s["z"], s["dt"], s["p128"], s["p128"], s["pgw"], s["pgw"], s["hs"], s["xs"]],
        out_specs=(s["xs"], s["grp"], s["grp"], s["xs"],
                   pl.BlockSpec((None, SSD_L, LANES), lambda g, c: (g, cidx(c), 0)), g128, g128, s["pgw"], s["pgw"]),
        out_shape=(jax.ShapeDtypeStruct((T, SSD_INNER), f32), jax.ShapeDtypeStruct((T, SSD_G * SSD_N), f32),
                   jax.ShapeDtypeStruct((T, SSD_G * SSD_N), f32), jax.ShapeDtypeStruct((T, SSD_INNER), bf16),
                   jax.ShapeDtypeStruct((SSD_G, T, LANES), f32), jax.ShapeDtypeStruct((SSD_G, 1, LANES), f32),
                   jax.ShapeDtypeStruct((SSD_G, 1, LANES), f32), jax.ShapeDtypeStruct((1, SSD_INNER), f32),
                   jax.ShapeDtypeStruct((1, SSD_INNER), f32)),
        scratch_shapes=[pltpu.VMEM((SSD_N, SSD_GW), f32)], name=name, compiler_params=_params(2))(
            xbc, xbc, xbc, proj, proj, dtb, alog, dsk, ng, hs, dycat)


def _qk_norm_fn(q, k, qg, kg):
    return _rms(q, qg), _rms(k, kg)


def _sb_qknorm_fwd(proj, qg, kg, name):
    T = proj.shape[0]
    tm = _row_tile(T)
    par = pl.BlockSpec((1, HD), lambda h, i: (0, 0))
    out = pl.BlockSpec((tm, HD), lambda h, i: (i, h))
    sds = jax.ShapeDtypeStruct((T, SB_W), bf16)
    return _fn_call(_qk_norm_fn, (proj, proj, qg, kg),
                    [out, pl.BlockSpec((tm, HD), lambda h, i: (i, SB_HEADS + h)), par, par],
                    (sds, sds), (out, out), (SB_HEADS, T // tm), name)


def _sb_qknorm_bwd(proj, qg, kg, dqn, dkn, name):
    T = proj.shape[0]
    tm = _row_tile(T)
    par = pl.BlockSpec((1, HD), lambda h, i: (0, 0))
    out = pl.BlockSpec((tm, HD), lambda h, i: (i, h))
    gout = pl.BlockSpec((1, 1, HD), lambda h, i: (h, 0, 0))

    def fn(q, k, qg, kg, dqn, dkn):
        _, vjp = jax.vjp(_qk_norm_fn, q, k, qg, kg)
        dq, dk, dqg, dkg = vjp((dqn, dkn))
        return dq, dk, dqg[None], dkg[None]

    sds = jax.ShapeDtypeStruct((T, SB_W), bf16)
    gsds = jax.ShapeDtypeStruct((SB_HEADS, 1, HD), f32)
    return _fn_call(fn, (proj, proj, qg, kg, dqn, dkn),
                    [out, pl.BlockSpec((tm, HD), lambda h, i: (i, SB_HEADS + h)), par, par, out, out],
                    (sds, sds, gsds, gsds), (out, out, gout, gout), (SB_HEADS, T // tm), name, acc={2: "last", 3: "last"})


def _split_dot(a, tri):
    hi = a.astype(bf16)
    lo = (a - hi.astype(f32)).astype(bf16)
    return jnp.dot(hi, tri, preferred_element_type=f32) + jnp.dot(lo, tri, preferred_element_type=f32)


def _sb_weights(q, kblk, run, later, mask):
    z = _dg(q, kblk, 1, 1) * SB_SCALE
    t = jnp.log(1.0 + jnp.exp(-jnp.abs(z)))
    sp = jnp.maximum(z, 0.0) + t
    log_beta = jnp.minimum(z, 0.0) - t
    if mask is not None:
        sp = jnp.where(mask, sp, 0.0)
    w = jnp.exp(log_beta - _split_dot(sp, later) - run)
    if mask is not None:
        w = jnp.where(mask, w, 0.0)
    return jnp.exp(log_beta), sp, w


def _sb_fwd(qn, kn, proj, name):
    T = qn.shape[0]
    B = min(SB_BLK, T)
    nq = T // B

    def body(q_ref, k_ref, v_ref, o_ref, ox_ref):
        qb = pl.program_id(1)
        q = q_ref[...]
        ri, ci = _iota((B, B), 0), _iota((B, B), 1)
        later = (ri > ci).astype(bf16)

        def block(kb, carry, mask):
            acc, acc_lo, run = carry
            off = pl.multiple_of(kb * B, B)
            _, sp, w = _sb_weights(q, k_ref[pl.ds(off, B), :], run, later, mask)
            vblk = v_ref[pl.ds(off, B), :]
            w_hi = w.astype(bf16)
            acc = acc + _dg(w_hi, vblk, 1, 0)
            acc_lo = acc_lo + _dg(w - w_hi.astype(f32), vblk, 1, 0)
            return acc, acc_lo, run + jnp.sum(sp, axis=1, keepdims=True)

        zero = jnp.zeros((B, HD), f32)
        carry = block(qb, (zero, zero, jnp.zeros((B, 1), f32)), ci < ri)
        carry = lax.fori_loop(0, qb, lambda i, cr: block(qb - 1 - i, cr, None), carry)
        o_ref[...] = carry[0].astype(o_ref.dtype)
        ox_ref[...] = carry[0] + carry[1]

    blk = pl.BlockSpec((B, HD), lambda h, i: (i, h))
    return pl.pallas_call(
        body, grid=(SB_HEADS, nq),
        in_specs=[blk, pl.BlockSpec((T, HD), lambda h, i: (0, h)), pl.BlockSpec((T, HD), lambda h, i: (0, 2 * SB_HEADS + h))],
        out_specs=(blk, blk), out_shape=(jax.ShapeDtypeStruct((T, SB_W), bf16), jax.ShapeDtypeStruct((T, SB_W), f32)),
        name=name, compiler_params=_params(2))(qn, kn, proj)


def _sb_bwd(qn, kn, proj, o, dycat, name):
    T = qn.shape[0]
    B = min(SB_BLK, T)
    nq = T // B

    def body(q_ref, k_ref, v_ref, o_ref, do_ref, dq_ref, dk_ref, dv_ref):
        qb = pl.program_id(1)

        @pl.when(qb == 0)
        def _():
            dk_ref[...] = jnp.zeros_like(dk_ref)
            dv_ref[...] = jnp.zeros_like(dv_ref)

        q = q_ref[...]
        do = do_ref[...].astype(f32)
        do_b = do.astype(bf16)
        gtot = jnp.sum(do_b.astype(f32) * o_ref[...], axis=1, keepdims=True)
        ri, ci = _iota((B, B), 0), _iota((B, B), 1)
        later = (ri > ci).astype(bf16)
        from_here = (ri >= ci).astype(bf16)

        def block(kb, carry, mask):
            dq, run, rung = carry
            off = pl.multiple_of(kb * B, B)
            kblk = k_ref[pl.ds(off, B), :]
            sig, sp, w = _sb_weights(q, kblk, run, later, mask)
            g = w * _dg(do_b, v_ref[pl.ds(off, B), :], 1, 1)
            before = gtot - rung - _split_dot(g, from_here)
            dz = (g * (1.0 - sig) - sig * before) * SB_SCALE
            if mask is not None:
                dz = jnp.where(mask, dz, 0.0)
            dz_b = dz.astype(bf16)
            dv_ref[pl.ds(off, B), :] += _dg(w, do_b, 0, 0)
            dk_ref[pl.ds(off, B), :] += _dg(dz_b, q, 0, 0)
            dq = dq + _dg(dz_b, kblk, 1, 0)
            return dq, run + jnp.sum(sp, axis=1, keepdims=True), rung + jnp.sum(g, axis=1, keepdims=True)

        zero = jnp.zeros((B, 1), f32)
        carry = block(qb, (jnp.zeros((B, HD), f32), zero, zero), ci < ri)
        carry = lax.fori_loop(0, qb, lambda i, cr: block(qb - 1 - i, cr, None), carry)
        dq_ref[...] = carry[0]

    blk = pl.BlockSpec((B, HD), lambda h, i: (i, h))
    full = pl.BlockSpec((T, HD), lambda h, i: (0, h))
    sds = jax.ShapeDtypeStruct((T, SB_W), f32)
    return pl.pallas_call(
        body, grid=(SB_HEADS, nq),
        in_specs=[blk, full, pl.BlockSpec((T, HD), lambda h, i: (0, 2 * SB_HEADS + h)), blk, blk],
        out_specs=(blk, full, full), out_shape=(sds, sds, sds), name=name, compiler_params=_params(2))(
            qn, kn, proj, o, dycat)


_BIG = (("mem_w_kv", (4, 128, 1024), 1), ("ssd_w_in", (2, 1024, 579), 2), ("ssd_w_out", (2, 256, 1024), 1),
        ("sb_w_in", (2, 1024, 640), 2), ("sb_w_out", (2, 256, 1024), 1), ("ffn_w_gate_up", (4, 1024, 704), 2),
        ("ffn_w_down", (4, 352, 1024), 1))
_BIG_ROWS = tuple(math.prod(s) // LANES for _, s, _ in _BIG)
_BIG_TOTAL = 66560
_ADAM_TILE = 1024


def _pack_rows(parts, total):
    rows = sum(p.shape[-2] for p in parts)
    pad = jnp.zeros(parts[0].shape[:-2] + (total - rows, LANES), parts[0].dtype)
    return jnp.concatenate(list(parts) + [pad], axis=-2)


def _full_from_slots(slots, shard_shape, axis):
    n = shard_shape[0]
    s = slots.reshape((N_DEV,) + shard_shape)
    if axis == 1:
        return s.transpose(1, 0, 2, 3).reshape(n, N_DEV * shard_shape[1], shard_shape[2])
    return s.transpose(1, 2, 0, 3).reshape(n, shard_shape[1], N_DEV * shard_shape[2])


def _slots_from_full(full, shard_shape, axis):
    n = shard_shape[0]
    if axis == 1:
        s = full.reshape(n, N_DEV, shard_shape[1], shard_shape[2]).transpose(1, 0, 2, 3)
    else:
        s = full.reshape(n, shard_shape[1], N_DEV, shard_shape[2]).transpose(2, 0, 1, 3)
    return s.reshape(N_DEV, -1, LANES)


def _ssd_in_cols(w):
    pad = jnp.zeros(w.shape[:-1] + (SSD_IN_PAD - SSD_IN,), w.dtype)
    return jnp.concatenate([w[..., :4096], w[..., 4120:4632], w[..., 4096:4120], pad], axis=-1)


def _ssd_in_cols_back(w):
    return jnp.concatenate([w[..., :4096], w[..., 4608:4632], w[..., 4096:4608]], axis=-1)


def _lane_rows(a):
    flat = a.reshape(-1)
    n = -(-flat.shape[0] // (8 * LANES)) * (8 * LANES)
    return jnp.pad(flat, (0, n - flat.shape[0])).reshape(-1, LANES)


def _pad128(a):
    return jnp.pad(a, ((0, 0), (0, LANES - a.shape[1])))


def kernel(x, mem, mix_norm_g, ffn_norm_g, mem_norm_g, mem_w_kv, mem_q_norm_g, mem_k_norm_g, ssd_w_in, ssd_conv_w, ssd_conv_b, ssd_dt_bias, ssd_a_log, ssd_d, ssd_norm_g, ssd_w_out, sb_w_in, sb_q_norm_g, sb_k_norm_g, sb_w_out, ffn_w_gate_up, ffn_w_down, loss_target, m_mix_norm_g, m_ffn_norm_g, m_mem_norm_g, m_mem_w_kv, m_mem_q_norm_g, m_mem_k_norm_g, m_ssd_w_in, m_ssd_conv_w, m_ssd_conv_b, m_ssd_dt_bias, m_ssd_a_log, m_ssd_d, m_ssd_norm_g, m_ssd_w_out, m_sb_w_in, m_sb_q_norm_g, m_sb_k_norm_g, m_sb_w_out, m_ffn_w_gate_up, m_ffn_w_down, v_mix_norm_g, v_ffn_norm_g, v_mem_norm_g, v_mem_w_kv, v_mem_q_norm_g, v_mem_k_norm_g, v_ssd_w_in, v_ssd_conv_w, v_ssd_conv_b, v_ssd_dt_bias, v_ssd_a_log, v_ssd_d, v_ssd_norm_g, v_ssd_w_out, v_sb_w_in, v_sb_q_norm_g, v_sb_k_norm_g, v_sb_w_out, v_ffn_w_gate_up, v_ffn_w_down):
    W = dict(mix_norm_g=mix_norm_g, ffn_norm_g=ffn_norm_g, mem_norm_g=mem_norm_g, mem_w_kv=mem_w_kv, mem_q_norm_g=mem_q_norm_g, mem_k_norm_g=mem_k_norm_g, ssd_w_in=ssd_w_in, ssd_conv_w=ssd_conv_w, ssd_conv_b=ssd_conv_b, ssd_dt_bias=ssd_dt_bias, ssd_a_log=ssd_a_log, ssd_d=ssd_d, ssd_norm_g=ssd_norm_g, ssd_w_out=ssd_w_out, sb_w_in=sb_w_in, sb_q_norm_g=sb_q_norm_g, sb_k_norm_g=sb_k_norm_g, sb_w_out=sb_w_out, ffn_w_gate_up=ffn_w_gate_up, ffn_w_down=ffn_w_down)
    M = dict(mix_norm_g=m_mix_norm_g, ffn_norm_g=m_ffn_norm_g, mem_norm_g=m_mem_norm_g, mem_w_kv=m_mem_w_kv, mem_q_norm_g=m_mem_q_norm_g, mem_k_norm_g=m_mem_k_norm_g, ssd_w_in=m_ssd_w_in, ssd_conv_w=m_ssd_conv_w, ssd_conv_b=m_ssd_conv_b, ssd_dt_bias=m_ssd_dt_bias, ssd_a_log=m_ssd_a_log, ssd_d=m_ssd_d, ssd_norm_g=m_ssd_norm_g, ssd_w_out=m_ssd_w_out, sb_w_in=m_sb_w_in, sb_q_norm_g=m_sb_q_norm_g, sb_k_norm_g=m_sb_k_norm_g, sb_w_out=m_sb_w_out, ffn_w_gate_up=m_ffn_w_gate_up, ffn_w_down=m_ffn_w_down)
    V = dict(mix_norm_g=v_mix_norm_g, ffn_norm_g=v_ffn_norm_g, mem_norm_g=v_mem_norm_g, mem_w_kv=v_mem_w_kv, mem_q_norm_g=v_mem_q_norm_g, mem_k_norm_g=v_mem_k_norm_g, ssd_w_in=v_ssd_w_in, ssd_conv_w=v_ssd_conv_w, ssd_conv_b=v_ssd_conv_b, ssd_dt_bias=v_ssd_dt_bias, ssd_a_log=v_ssd_a_log, ssd_d=v_ssd_d, ssd_norm_g=v_ssd_norm_g, ssd_w_out=v_ssd_w_out, sb_w_in=v_sb_w_in, sb_q_norm_g=v_sb_q_norm_g, sb_k_norm_g=v_sb_k_norm_g, sb_w_out=v_sb_w_out, ffn_w_gate_up=v_ffn_w_gate_up, ffn_w_down=v_ffn_w_down)
    names = list(W)
    T = x.shape[1]
    x0 = x.reshape(T, D)
    mem2 = mem.reshape(MEM_LEN, D)
    target = loss_target.reshape(T, D)
    my_dev = 4 * lax.axis_index("x") + 2 * lax.axis_index("y") + lax.axis_index("c")

    w_flat = _pack_rows([W[n].reshape(-1, LANES) for n, _, _ in _BIG], _BIG_TOTAL)
    slots = _allgather_hbm(w_flat.astype(bf16), "allgather_weights")
    full, off = {}, 0
    for (n, shp, ax), rows in zip(_BIG, _BIG_ROWS):
        full[n] = _full_from_slots(slots[:, off:off + rows], shp, ax)
        off += rows
    full["ssd_w_in"] = _ssd_in_cols(full["ssd_w_in"])
    conv_slots = _allgather_vmem(_lane_rows(ssd_conv_w), "allgather_conv_w")
    conv_w = _full_from_slots(conv_slots[:, :20], (2, 4, 320), 2)

    mem_g = mem_norm_g.reshape(1, D)

    saved = []
    xc = x0
    for i in range(DEPTH):
        j = i // 2
        ssd = i % 2 == 0
        L = f"l{i}_"
        mix_g = mix_norm_g[i:i + 1]
        h = _rmsnorm_fwd(xc, mix_g, L + "mix_norm")
        w_in = full["ssd_w_in"][j] if ssd else full["sb_w_in"][j]
        proj = _matmul(h, w_in, tm=256 if ssd else 512, tn=w_in.shape[1] if ssd else 512, tk=D, name=L + "in_proj")
        k_mem, v_mem = _memkv_fwd(mem2, mem_g, full["mem_w_kv"][i], mem_k_norm_g[i:i + 1], L + "mem_kv")
        q_col = 32 if ssd else 36
        o_mem = _memattn_fwd(proj, q_col, k_mem, v_mem, mem_q_norm_g[i:i + 1], L + "mem_attn")
        st = dict(x_in=xc, h=h, proj=proj, k_mem=k_mem, v_mem=v_mem)
        if ssd:
            xbc = _conv_fwd(proj, conv_w[j], ssd_conv_b[j:j + 1], L + "conv")
            dtb, alog = _pad128(ssd_dt_bias[j:j + 1]), _pad128(ssd_a_log[j:j + 1])
            dsk = jnp.repeat(ssd_d[j], SSD_P).reshape(1, SSD_INNER)
            y, hs = _ssd_fwd(xbc, proj, dtb, alog, dsk, ssd_norm_g[j:j + 1], L + "ssd_scan")
            st.update(xbc=xbc, hs=hs, dtb=dtb, alog=alog, dsk=dsk)
            w_out = full["ssd_w_out"][j]
        else:
            qn, kn = _sb_qknorm_fwd(proj, sb_q_norm_g[j:j + 1], sb_k_norm_g[j:j + 1], L + "qk_norm")
            y, o_exact = _sb_fwd(qn, kn, proj, L + "sb_attn")
            st.update(qn=qn, kn=kn, o=o_exact)
            w_out = full["sb_w_out"][j]
        ycat = jnp.concatenate([y, o_mem], axis=1)
        x_mid = _matmul(ycat, w_out, tm=512, tn=D, tk=2048, res=xc, name=L + "out_proj")
        h2 = _rmsnorm_fwd(x_mid, ffn_norm_g[i:i + 1], L + "ffn_norm")
        gu = _matmul(h2, full["ffn_w_gate_up"][i], tm=512, tn=512, tk=D, name=L + "ffn_up")
        act = _ffn_act_fwd(gu, L + "ffn_act")
        xc = _matmul(act, full["ffn_w_down"][i], tm=512, tn=D, tk=FFN_H, res=x_mid, name=L + "ffn_down")
        st.update(ycat=ycat, x_mid=x_mid, h2=h2, gu=gu, act=act, w_in=w_in, w_out=w_out)
        saved.append(st)

    dx, loss_part = _loss_head(xc, target, "loss_head")
    loss = lax.psum(jnp.sum(loss_part), ("x", "y", "c"))

    G = {n: [None] * W[n].shape[0] for n in names if W[n].ndim > 1}
    d_mem_g = jnp.zeros((1, D), f32)
    for i in reversed(range(DEPTH)):
        j = i // 2
        ssd = i % 2 == 0
        L = f"l{i}_b_"
        st = saved[i]
        proj = st["proj"]
        dact = _matmul(dx, full["ffn_w_down"][i], tb=True, tm=512, tn=FFN_H, tk=D, name=L + "d_act")
        G["ffn_w_down"][i] = _matmul(st["act"], dx, ta=True, tm=256, tn=D, tk=512, name=L + "dw_down")
        dgu = _ffn_act_bwd(st["gu"], dact, L + "d_gu")
        dh2 = _matmul(dgu, full["ffn_w_gate_up"][i], tb=True, tm=512, tn=D, tk=2 * FFN_H, name=L + "d_h2")
        G["ffn_w_gate_up"][i] = _matmul(st["h2"], dgu, ta=True, tm=D, tn=512, tk=512, name=L + "dw_up")
        dx, G["ffn_norm_g"][i] = _rmsnorm_bwd(st["x_mid"], ffn_norm_g[i:i + 1], dh2, dx, L + "d_ffn_norm")
        dycat = _matmul(dx, st["w_out"], tb=True, tm=512, tn=2048, tk=D, name=L + "d_ycat")
        g_out = _matmul(st["ycat"], dx, ta=True, tm=512, tn=D, tk=512, name=L + "dw_out")
        q_col = 32 if ssd else 36
        dq_mem, dk_mem, dv_mem, dqg = _memattn_bwd(proj, q_col, st["k_mem"], st["v_mem"], mem_q_norm_g[i:i + 1], dycat, 12, L + "d_mem_attn")
        G["mem_q_norm_g"][i] = jnp.sum(dqg, axis=0)
        dmg, G["mem_w_kv"][i], G["mem_k_norm_g"][i] = _memkv_bwd(mem2, mem_g, full["mem_w_kv"][i], mem_k_norm_g[i:i + 1], dk_mem, dv_mem, L + "d_mem_kv")
        d_mem_g = d_mem_g + dmg
        if ssd:
            G["ssd_w_out"][j] = g_out
            dxs, dbm, dcm, dz, ddt, ddtb, dalog, ddsk, dng = _ssd_bwd(
                st["xbc"], proj, st["dtb"], st["alog"], st["dsk"], ssd_norm_g[j:j + 1], st["hs"], dycat, L + "d_ssd_scan")
            G["ssd_dt_bias"][j] = jnp.sum(ddtb, axis=0)[:, :SSD_HEADS]
            G["ssd_a_log"][j] = jnp.sum(dalog, axis=0)[:, :SSD_HEADS]
            G["ssd_d"][j] = jnp.sum(ddsk.reshape(SSD_HEADS, SSD_P), axis=1).reshape(1, SSD_HEADS)
            G["ssd_norm_g"][j] = dng
            dxbc_act = jnp.concatenate([dxs, dbm, dcm], axis=1)
            dpre, G["ssd_conv_w"][j], G["ssd_conv_b"][j] = _conv_bwd_pre(proj, conv_w[j], ssd_conv_b[j:j + 1], dxbc_act, L + "d_conv_pre")
            dxbc = _conv_bwd_in(dpre, conv_w[j], L + "d_conv_in")
            ddt_all = jnp.sum(ddt, axis=0).astype(bf16)
            dproj = jnp.concatenate([dz, dxbc, dq_mem, ddt_all], axis=1)
        else:
            G["sb_w_out"][j] = g_out
            dqn, dkn, dv = _sb_bwd(st["qn"], st["kn"], proj, st["o"], dycat, L + "d_sb_attn")
            dq, dk, dqg2, dkg2 = _sb_qknorm_bwd(proj, sb_q_norm_g[j:j + 1], sb_k_norm_g[j:j + 1], dqn, dkn, L + "d_qk_norm")
            G["sb_q_norm_g"][j] = jnp.sum(dqg2, axis=0)
            G["sb_k_norm_g"][j] = jnp.sum(dkg2, axis=0)
            dproj = jnp.concatenate([dq, dk, dv.astype(bf16), dq_mem], axis=1)
        n_in = dproj.shape[1]
        dh = _matmul(dproj, st["w_in"], tb=True, tm=512, tn=D, tk=n_in, name=L + "d_h")
        g_in = _matmul(st["h"], dproj, ta=True, tm=256 if ssd else D, tn=n_in if ssd else 512, tk=512, name=L + "dw_in")
        if ssd:
            G["ssd_w_in"][j] = _ssd_in_cols_back(g_in)
        else:
            G["sb_w_in"][j] = g_in
        dx, G["mix_norm_g"][i] = _rmsnorm_bwd(st["x_in"], mix_norm_g[i:i + 1], dh, dx, L + "d_mix_norm")

    grad_x = dx.reshape(x.shape)

    g_slots = _pack_rows([_slots_from_full(jnp.stack(G[n]), shp, ax).astype(bf16) for n, shp, ax in _BIG], _BIG_TOTAL)
    parts = _exchange_slices(g_slots, "exchange_grads")
    m_flat = _pack_rows([M[n].reshape(-1, LANES) for n, _, _ in _BIG], _BIG_TOTAL)
    v_flat = _pack_rows([V[n].reshape(-1, LANES) for n, _, _ in _BIG], _BIG_TOTAL)
    res_big = _adamw_reduce(parts, w_flat, m_flat, v_flat, _ADAM_TILE, "adamw_big")
    out = {}
    off = 0
    for (n, shp, _), rows in zip(_BIG, _BIG_ROWS):
        out[n] = tuple(r[off:off + rows].reshape(shp) for r in res_big)
        off += rows

    small = [n for n in names if n not in out and n != "ssd_conv_w"]
    G["mem_norm_g"] = d_mem_g.reshape(D)
    small_grads = [_lane_rows(G[n] if n == "mem_norm_g" else jnp.concatenate(G[n], axis=0)) for n in small]
    conv_grad = _lane_rows(jnp.stack(G["ssd_conv_w"]))
    sm_rows = [g.shape[0] for g in small_grads]
    n_small = sum(sm_rows)
    sm_total = -(-(n_small + conv_grad.shape[0]) // 8) * 8
    gathered = _allgather_vmem(_pack_rows(small_grads + [conv_grad], sm_total), "allgather_small_grads")
    whole = lambda s: pl.BlockSpec(s, lambda i: (0,) * len(s))
    g_sum = _fn_call(_sum_slots, (gathered,), [whole((N_DEV, sm_total, LANES))],
                     jax.ShapeDtypeStruct((sm_total, LANES), f32), whole((sm_total, LANES)), (1,), "sum_small_grads")
    conv_full = g_sum[n_small:n_small + 160].reshape(2, 4, SSD_CONV_DIM)
    conv_mine = lax.dynamic_slice_in_dim(conv_full, my_dev * 320, 320, axis=2)
    ad_total = n_small + 24
    pack = lambda d: _pack_rows([_lane_rows(d[n]) for n in small] + [_lane_rows(d["ssd_conv_w"])], ad_total)
    g_pack = _pack_rows([g_sum[:n_small], _lane_rows(conv_mine)], ad_total)
    blk = whole((ad_total, LANES))
    res_small = _fn_call(lambda g, w, m, v: _adamw_math(w, g, m, v), (g_pack, pack(W), pack(M), pack(V)), [blk] * 4,
                         (jax.ShapeDtypeStruct((ad_total, LANES), f32),) * 3, (blk,) * 3, (1,), "adamw_small")
    res_small = (g_pack,) + tuple(res_small)
    off = 0
    for n, rows in zip(small + ["ssd_conv_w"], sm_rows + [24]):
        size = W[n].size
        out[n] = tuple(r[off:off + rows].reshape(-1)[:size].reshape(W[n].shape) for r in res_small)
        off += rows

    return (loss, grad_x, *[out[n][0] for n in names], *[out[n][1] for n in names],
            *[out[n][2] for n in names], *[out[n][3] for n in names])
```

```python
import functools
import math

import jax
import jax.numpy as jnp
from jax import lax
from jax.experimental import pallas as pl
from jax.experimental.pallas import tpu as pltpu

f32, bf16 = jnp.float32, jnp.bfloat16
MESH = pl.DeviceIdType.MESH

N_DEV = 8
D = 1024
DEPTH = 4
EPS = 1e-6
MEM_LEN, MEM_HEADS, MEM_W, HD = 256, 4, 512, 128
SSD_INNER, SSD_HEADS, SSD_G, SSD_P, SSD_N, SSD_L = 1536, 24, 4, 64, 128, 128
SSD_GW = SSD_INNER // SSD_G
SSD_CONV_DIM = 2560
SSD_IN = 4632
SSD_IN_PAD = 4736
SB_W, SB_HEADS, SB_IN = 1536, 12, 5120
SB_BLK = 256
SB_SCALE = HD ** -0.5
SB_DEAD = 105.0
FFN_H = 2816
LANES = 128
VMEM_LIMIT = 56 * 1024 * 1024

ADAM_LR, ADAM_B1, ADAM_B2, ADAM_EPS, ADAM_WD, ADAM_STEP = 0.001, 0.9, 0.999, 1e-08, 0.01, 10

HIGHEST = lax.Precision.HIGHEST


def _params(n_grid):
    return pltpu.CompilerParams(dimension_semantics=("arbitrary",) * n_grid, vmem_limit_bytes=VMEM_LIMIT)


def _dg(a, b, ca, cb):
    return lax.dot_general(a.astype(bf16), b.astype(bf16), (((ca,), (cb,)), ((), ())), preferred_element_type=f32)


@jax.custom_vjp
def bdot_nn(a, b):
    return _dg(a, b, 1, 0)


def _nn_fwd(a, b):
    return _dg(a, b, 1, 0), (a, b)


def _nn_bwd(res, ct):
    a, b = res
    return _dg(ct, b, 1, 1).astype(a.dtype), _dg(a, ct, 0, 0).astype(b.dtype)


bdot_nn.defvjp(_nn_fwd, _nn_bwd)


@jax.custom_vjp
def bdot_nt(a, b):
    return _dg(a, b, 1, 1)


def _nt_fwd(a, b):
    return _dg(a, b, 1, 1), (a, b)


def _nt_bwd(res, ct):
    a, b = res
    return _dg(ct, b, 1, 0).astype(a.dtype), _dg(ct, a, 0, 0).astype(b.dtype)


bdot_nt.defvjp(_nt_fwd, _nt_bwd)


@jax.custom_vjp
def bdot_tn(a, b):
    return _dg(a, b, 0, 0)


def _tn_fwd(a, b):
    return _dg(a, b, 0, 0), (a, b)


def _tn_bwd(res, ct):
    a, b = res
    return _dg(b, ct, 1, 1).astype(a.dtype), _dg(a, ct, 1, 0).astype(b.dtype)


bdot_tn.defvjp(_tn_fwd, _tn_bwd)


def _rms(x, g):
    return x * lax.rsqrt(jnp.mean(x * x, axis=-1, keepdims=True) + EPS) * g


def _iota(shape, axis):
    return lax.broadcasted_iota(jnp.int32, shape, axis)


def _fn_call(fn, args, in_specs, out_shapes, out_specs, grid, name, acc=None):
    n_in = len(args)
    acc = acc or {}
    n_grid = len(grid)

    def body(*refs):
        ins, outs = refs[:n_in], refs[n_in:]
        res = fn(*[r[...] for r in ins])
        if not isinstance(res, (tuple, list)):
            res = (res,)
        for k, (o, r) in enumerate(zip(outs, res)):
            mode = acc.get(k)
            if mode is None:
                o[...] = r.astype(o.dtype)
                continue
            if mode == "last":
                first = pl.program_id(n_grid - 1) == 0
            else:
                first = functools.reduce(jnp.logical_and, [pl.program_id(d) == 0 for d in range(n_grid)])

            @pl.when(first)
            def _(o=o, r=r):
                o[...] = r.astype(o.dtype)

            @pl.when(jnp.logical_not(first))
            def _(o=o, r=r):
                o[...] += r.astype(o.dtype)

    return pl.pallas_call(
        body, grid=grid, in_specs=in_specs, out_specs=out_specs, out_shape=out_shapes, name=name,
        compiler_params=_params(n_grid))(*args)


def _matmul(a, b, *, ta=False, tb=False, out_dtype=f32, tm, tn, tk, res=None, name):
    M, K = (a.shape[1], a.shape[0]) if ta else a.shape
    N = b.shape[0] if tb else b.shape[1]
    tm, tn, tk = min(tm, M), min(tn, N), min(tk, K)
    assert M % tm == 0 and N % tn == 0 and K % tk == 0, (name, M, N, K, tm, tn, tk)
    nk = K // tk
    a_spec = pl.BlockSpec((tk, tm), lambda i, j, k: (k, i)) if ta else pl.BlockSpec((tm, tk), lambda i, j, k: (i, k))
    b_spec = pl.BlockSpec((tn, tk), lambda i, j, k: (j, k)) if tb else pl.BlockSpec((tk, tn), lambda i, j, k: (k, j))
    o_spec = pl.BlockSpec((tm, tn), lambda i, j, k: (i, j))
    ca, cb = (0 if ta else 1), (1 if tb else 0)

    def body(*refs):
        if res is None:
            a_ref, b_ref, o_ref, acc_ref = refs
            r_ref = None
        else:
            a_ref, b_ref, r_ref, o_ref, acc_ref = refs
        k = pl.program_id(2)
        part = _dg(a_ref[...], b_ref[...], ca, cb)

        @pl.when(k == 0)
        def _():
            acc_ref[...] = part

        @pl.when(k > 0)
        def _():
            acc_ref[...] += part

        @pl.when(k == nk - 1)
        def _():
            out = acc_ref[...]
            if r_ref is not None:
                out = out + r_ref[...].astype(f32)
            o_ref[...] = out.astype(o_ref.dtype)

    args = (a, b) if res is None else (a, b, res)
    in_specs = [a_spec, b_spec] + ([] if res is None else [o_spec])
    return pl.pallas_call(
        body, grid=(M // tm, N // tn, nk), in_specs=in_specs, out_specs=o_spec,
        out_shape=jax.ShapeDtypeStruct((M, N), out_dtype), name=name,
        scratch_shapes=[pltpu.VMEM((tm, tn), f32)], compiler_params=_params(3))(*args)


def _row_tile(T):
    return min(T, 512)


def _my_pos():
    return lax.axis_index("x"), lax.axis_index("y"), lax.axis_index("c")


def _allgather_hbm(xs, name):
    R, C = xs.shape

    def body(x_ref, out_ref, send_sems, recv_sems, local_sem):
        x, y, c = _my_pos()
        me, sibling = (x, y, c), (x, y, 1 - c)
        chips = [(1 - x, y), (x, 1 - y), (1 - x, 1 - y)]

        def slot(px, py, pc):
            return out_ref.at[4 * px + 2 * py + pc]

        def copy(k, block, to, src=None):
            return pltpu.make_async_remote_copy(
                src_ref=slot(*block) if src is None else src, dst_ref=slot(*block),
                send_sem=send_sems.at[k], recv_sem=recv_sems.at[k], device_id=to, device_id_type=MESH)

        mine = pltpu.make_async_copy(x_ref, slot(*me), local_sem)
        mine.start()
        first = [copy(0, me, sibling, src=x_ref)]
        first += [copy(1 + j, me, (*chip, c), src=x_ref) for j, chip in enumerate(chips)]
        for cp in first:
            cp.start()
        passed = [copy(4 + j, (*chip, c), sibling) for j, chip in enumerate(chips)]
        for j, chip in enumerate(chips):
            copy(1 + j, (*chip, c), me).wait_recv()
            passed[j].start()
        copy(0, sibling, me).wait_recv()
        for j, chip in enumerate(chips):
            copy(4 + j, (*chip, 1 - c), me).wait_recv()
        for cp in first + passed:
            cp.wait_send()
        mine.wait()

    return pl.pallas_call(
        body, out_shape=jax.ShapeDtypeStruct((N_DEV, R, C), xs.dtype),
        in_specs=[pl.BlockSpec(memory_space=pl.ANY)], out_specs=pl.BlockSpec(memory_space=pl.ANY),
        scratch_shapes=[pltpu.SemaphoreType.DMA((7,)), pltpu.SemaphoreType.DMA((7,)), pltpu.SemaphoreType.DMA],
        name=name)(xs)


def _allgather_vmem(xs, name):
    R, C = xs.shape

    def body(x_ref, out_ref, send_sems, recv_sems):
        x, y, c = _my_pos()
        me = 4 * x + 2 * y + c
        out_ref[me] = x_ref[...]
        copies = []
        for k in range(1, N_DEV):
            px = 1 - x if k & 4 else x
            py = 1 - y if k & 2 else y
            pc = 1 - c if k & 1 else c
            cp = pltpu.make_async_remote_copy(
                src_ref=x_ref, dst_ref=out_ref.at[me], send_sem=send_sems.at[k - 1], recv_sem=recv_sems.at[k - 1],
                device_id=(px, py, pc), device_id_type=MESH)
            cp.start()
            copies.append(cp)
        for cp in copies:
            cp.wait()

    return pl.pallas_call(
        body, out_shape=jax.ShapeDtypeStruct((N_DEV, R, C), xs.dtype),
        in_specs=[pl.BlockSpec(memory_space=pltpu.VMEM)], out_specs=pl.BlockSpec(memory_space=pltpu.VMEM),
        scratch_shapes=[pltpu.SemaphoreType.DMA((7,)), pltpu.SemaphoreType.DMA((7,))], name=name)(xs)


def _exchange_slices(g, name):
    _, R, C = g.shape

    def body(g_ref, out_ref, send_sems, recv_sems, local_sem):
        x, y, c = _my_pos()
        me = 4 * x + 2 * y + c
        mine = pltpu.make_async_copy(g_ref.at[me], out_ref.at[me], local_sem)
        mine.start()
        copies = []
        for k in range(1, N_DEV):
            px = 1 - x if k & 4 else x
            py = 1 - y if k & 2 else y
            pc = 1 - c if k & 1 else c
            cp = pltpu.make_async_remote_copy(
                src_ref=g_ref.at[4 * px + 2 * py + pc], dst_ref=out_ref.at[me],
                send_sem=send_sems.at[k - 1], recv_sem=recv_sems.at[k - 1],
                device_id=(px, py, pc), device_id_type=MESH)
            cp.start()
            copies.append(cp)
        for cp in copies:
            cp.wait()
        mine.wait()

    return pl.pallas_call(
        body, out_shape=jax.ShapeDtypeStruct(g.shape, g.dtype),
        in_specs=[pl.BlockSpec(memory_space=pl.ANY)], out_specs=pl.BlockSpec(memory_space=pl.ANY),
        scratch_shapes=[pltpu.SemaphoreType.DMA((7,)), pltpu.SemaphoreType.DMA((7,)), pltpu.SemaphoreType.DMA],
        name=name)(g)


def _adamw_math(w, g, m, v):
    m = ADAM_B1 * m + (1.0 - ADAM_B1) * g
    v = ADAM_B2 * v + (1.0 - ADAM_B2) * jnp.square(g)
    m_hat = m / (1.0 - ADAM_B1 ** ADAM_STEP)
    v_hat = v / (1.0 - ADAM_B2 ** ADAM_STEP)
    delta = -ADAM_LR * (m_hat / (jnp.sqrt(v_hat) + ADAM_EPS) + ADAM_WD * w)
    return delta, m, v


def _sum_slots(parts):
    g = parts[0].astype(f32)
    for i in range(1, N_DEV):
        g = g + parts[i].astype(f32)
    return g


def _adamw_reduce(parts, w, m, v, tr, name):
    R = w.shape[0]
    assert R % tr == 0

    def fn(p, w, m, v):
        g = _sum_slots(p)
        return (g,) + _adamw_math(w, g, m, v)

    row = pl.BlockSpec((tr, LANES), lambda i: (i, 0))
    sds = jax.ShapeDtypeStruct((R, LANES), f32)
    return _fn_call(fn, (parts, w, m, v), [pl.BlockSpec((N_DEV, tr, LANES), lambda i: (0, i, 0)), row, row, row],
                    (sds,) * 4, (row,) * 4, (R // tr,), name)


def _rmsnorm_fwd(x, g, name):
    T = x.shape[0]
    tm = _row_tile(T)
    row = pl.BlockSpec((tm, D), lambda i: (i, 0))
    par = pl.BlockSpec((1, D), lambda i: (0, 0))
    return _fn_call(lambda x, g: _rms(x, g), (x, g), [row, par], jax.ShapeDtypeStruct((T, D), bf16), row, (T // tm,), name)


def _rmsnorm_bwd(x, g, dh, dres, name):
    T = x.shape[0]
    tm = _row_tile(T)
    row = pl.BlockSpec((tm, D), lambda i: (i, 0))
    par = pl.BlockSpec((1, D), lambda i: (0, 0))

    def fn(x, g, dh, dres):
        _, vjp = jax.vjp(_rms, x, g)
        dx, dg = vjp(dh.astype(f32))
        return dx + dres, dg

    return _fn_call(fn, (x, g, dh, dres), [row, par, row, row],
                    (jax.ShapeDtypeStruct((T, D), f32), jax.ShapeDtypeStruct((1, D), f32)), (row, par),
                    (T // tm,), name, acc={1: "all"})


def _swiglu_act(g, u):
    return jax.nn.silu(g) * u


def _ffn_act_fwd(gu, name):
    T = gu.shape[0]
    tm, tc = _row_tile(T), 256
    nc = FFN_H // tc
    return _fn_call(_swiglu_act, (gu, gu),
                    [pl.BlockSpec((tm, tc), lambda i, j: (i, j)), pl.BlockSpec((tm, tc), lambda i, j: (i, nc + j))],
                    jax.ShapeDtypeStruct((T, FFN_H), bf16), pl.BlockSpec((tm, tc), lambda i, j: (i, j)),
                    (T // tm, nc), name)


def _ffn_act_bwd(gu, dact, name):
    T = gu.shape[0]
    tm, tc = _row_tile(T), 256
    nc = FFN_H // tc

    def fn(g, u, da):
        _, vjp = jax.vjp(_swiglu_act, g, u)
        return vjp(da.astype(f32))

    lo = pl.BlockSpec((tm, tc), lambda i, j: (i, j))
    hi = pl.BlockSpec((tm, tc), lambda i, j: (i, nc + j))
    dg, du = _fn_call(fn, (gu, gu, dact), [lo, hi, lo],
                      (jax.ShapeDtypeStruct((T, FFN_H), bf16),) * 2, (lo, lo), (T // tm, nc), name)
    return jnp.concatenate([dg, du], axis=1)


def _loss_head(x, target, name):
    T = x.shape[0]
    tm = _row_tile(T)
    row = pl.BlockSpec((tm, D), lambda i: (i, 0))
    par = pl.BlockSpec((1, LANES), lambda i: (0, 0))

    def fn(x, t):
        e = x - t
        s = jnp.sum(e * e, axis=0, keepdims=True)
        part = s[:, 0:LANES]
        for k in range(1, D // LANES):
            part = part + s[:, k * LANES:(k + 1) * LANES]
        return e * (1.0 / D), part * (0.5 / D)

    return _fn_call(fn, (x, target), [row, row],
                    (jax.ShapeDtypeStruct((T, D), f32), jax.ShapeDtypeStruct((1, LANES), f32)), (row, par),
                    (T // tm,), name, acc={1: "all"})


def _memkv_fn(mem, mg, wkv, kg):
    mn = _rms(mem, mg)
    kv = bdot_nn(mn, wkv)
    ks = [_rms(kv[:, h * HD:(h + 1) * HD], kg) for h in range(MEM_HEADS)]
    return jnp.concatenate(ks, axis=1), kv[:, MEM_W:]


def _memkv_fwd(mem, mg, wkv, kg, name):
    whole = lambda s: pl.BlockSpec(s, lambda i: (0,) * len(s))
    sds = jax.ShapeDtypeStruct((MEM_LEN, MEM_W), f32)
    return _fn_call(lambda m, g, w, k: _memkv_fn(m, g, w.astype(f32), k), (mem, mg, wkv, kg),
                    [whole((MEM_LEN, D)), whole((1, D)), whole((D, 2 * MEM_W)), whole((1, HD))],
                    (sds, sds), (whole((MEM_LEN, MEM_W)),) * 2, (1,), name)


def _memkv_bwd(mem, mg, wkv, kg, dk, dv, name):
    whole = lambda s: pl.BlockSpec(s, lambda i: (0,) * len(s))

    def fn(m, g, w, k, dk, dv):
        _, vjp = jax.vjp(lambda g, w, k: _memkv_fn(m, g, w, k), g, w.astype(f32), k)
        return vjp((dk, dv))

    return _fn_call(fn, (mem, mg, wkv, kg, dk, dv),
                    [whole((MEM_LEN, D)), whole((1, D)), whole((D, 2 * MEM_W)), whole((1, HD)),
                     whole((MEM_LEN, MEM_W)), whole((MEM_LEN, MEM_W))],
                    (jax.ShapeDtypeStruct((1, D), f32), jax.ShapeDtypeStruct((D, 2 * MEM_W), f32),
                     jax.ShapeDtypeStruct((1, HD), f32)),
                    (whole((1, D)), whole((D, 2 * MEM_W)), whole((1, HD))), (1,), name)


def _memattn_fn(q, k, v, qg):
    qn = _rms(q, qg)
    s = bdot_nt(qn, k) * (HD ** -0.5)
    s = s - jnp.max(s, axis=-1, keepdims=True)
    p = jnp.exp(s)
    p = p / jnp.sum(p, axis=-1, keepdims=True)
    return bdot_nn(p, v)


def _memattn_fwd(proj, q_col, k, v, qg, name):
    T = proj.shape[0]
    tm = _row_tile(T)
    return _fn_call(_memattn_fn, (proj, k, v, qg),
                    [pl.BlockSpec((tm, HD), lambda h, i: (i, q_col + h)), pl.BlockSpec((MEM_LEN, HD), lambda h, i: (0, h)),
                     pl.BlockSpec((MEM_LEN, HD), lambda h, i: (0, h)), pl.BlockSpec((1, HD), lambda h, i: (0, 0))],
                    jax.ShapeDtypeStruct((T, MEM_W), bf16), pl.BlockSpec((tm, HD), lambda h, i: (i, h)),
                    (MEM_HEADS, T // tm), name)


def _memattn_bwd(proj, q_col, k, v, qg, dycat, do_col, name):
    T = proj.shape[0]
    tm = _row_tile(T)

    def fn(q, k, v, qg, do):
        _, vjp = jax.vjp(_memattn_fn, q, k, v, qg)
        dq, dk, dv, dg = vjp(do.astype(f32))
        return dq, dk, dv, dg[None]

    kv_spec = pl.BlockSpec((MEM_LEN, HD), lambda h, i: (0, h))
    kv_sds = jax.ShapeDtypeStruct((MEM_LEN, MEM_W), f32)
    return _fn_call(fn, (proj, k, v, qg, dycat),
                    [pl.BlockSpec((tm, HD), lambda h, i: (i, q_col + h)), kv_spec, kv_spec,
                     pl.BlockSpec((1, HD), lambda h, i: (0, 0)), pl.BlockSpec((tm, HD), lambda h, i: (i, do_col + h))],
                    (jax.ShapeDtypeStruct((T, MEM_W), bf16), kv_sds, kv_sds, jax.ShapeDtypeStruct((MEM_HEADS, 1, HD), f32)),
                    (pl.BlockSpec((tm, HD), lambda h, i: (i, h)), kv_spec, kv_spec,
                     pl.BlockSpec((1, 1, HD), lambda h, i: (h, 0, 0))),
                    (MEM_HEADS, T // tm), name, acc={1: "last", 2: "last", 3: "last"})


def _conv_taps(xp, w, first, tm):
    out = w[0:1, :] * xp[first:first + tm, :]
    for k in range(1, 4):
        out = out + w[k:k + 1, :] * xp[first + k:first + k + tm, :]
    return out


def _conv_blocks(T):
    tm, tc = _row_tile(T), 512
    nt = T // tm
    cur = pl.BlockSpec((tm, tc), lambda j, i: (i, 3 + j))
    prev = pl.BlockSpec((8, tc), lambda j, i: (jnp.maximum(i * (tm // 8) - 1, 0), 3 + j))
    par4 = pl.BlockSpec((4, tc), lambda j, i: (0, j))
    par1 = pl.BlockSpec((1, tc), lambda j, i: (0, j))
    out = pl.BlockSpec((tm, tc), lambda j, i: (i, j))
    return tm, tc, nt, cur, prev, par4, par1, out


def _conv_fwd(proj, w, b, name):
    T = proj.shape[0]
    tm, tc, nt, cur, prev, par4, par1, out = _conv_blocks(T)

    def body(prev_ref, cur_ref, w_ref, b_ref, o_ref):
        halo = jnp.where(pl.program_id(1) == 0, 0.0, prev_ref[...])
        xp = jnp.concatenate([halo, cur_ref[...]], axis=0)
        o_ref[...] = jax.nn.silu(_conv_taps(xp, w_ref[...], 5, tm) + b_ref[...])

    return pl.pallas_call(body, grid=(SSD_CONV_DIM // tc, nt), in_specs=[prev, cur, par4, par1], out_specs=out,
                          out_shape=jax.ShapeDtypeStruct((T, SSD_CONV_DIM), f32), name=name,
                          compiler_params=_params(2))(proj, proj, w, b)


def _conv_bwd_pre(proj, w, b, dact, name):
    T = proj.shape[0]
    tm, tc, nt, cur, prev, par4, par1, out = _conv_blocks(T)

    def body(prev_ref, cur_ref, w_ref, b_ref, da_ref, dp_ref, dw_ref, db_ref):
        i = pl.program_id(1)
        halo = jnp.where(i == 0, 0.0, prev_ref[...])
        xp = jnp.concatenate([halo, cur_ref[...]], axis=0)
        pre = _conv_taps(xp, w_ref[...], 5, tm) + b_ref[...]
        sig = jax.nn.sigmoid(pre)
        dpre = da_ref[...] * (sig * (1.0 + pre * (1.0 - sig)))
        dp_ref[...] = dpre
        dw = jnp.concatenate([jnp.sum(dpre * xp[5 + k:5 + k + tm, :], axis=0, keepdims=True) for k in range(4)], axis=0)
        db = jnp.sum(dpre, axis=0, keepdims=True)

        @pl.when(i == 0)
        def _():
            dw_ref[...] = dw
            db_ref[...] = db

        @pl.when(i > 0)
        def _():
            dw_ref[...] += dw
            db_ref[...] += db

    return pl.pallas_call(
        body, grid=(SSD_CONV_DIM // tc, nt), in_specs=[prev, cur, par4, par1, out], out_specs=(out, par4, par1),
        out_shape=(jax.ShapeDtypeStruct((T, SSD_CONV_DIM), f32), jax.ShapeDtypeStruct((4, SSD_CONV_DIM), f32),
                   jax.ShapeDtypeStruct((1, SSD_CONV_DIM), f32)),
        name=name, compiler_params=_params(2))(proj, proj, w, b, dact)


def _conv_bwd_in(dpre, w, name):
    T = dpre.shape[0]
    tm, tc, nt, _, _, par4, _, out = _conv_blocks(T)
    nxt = pl.BlockSpec((8, tc), lambda j, i: (jnp.minimum((i + 1) * (tm // 8), T // 8 - 1), j))

    def body(cur_ref, nxt_ref, w_ref, o_ref):
        halo = jnp.where(pl.program_id(1) == nt - 1, 0.0, nxt_ref[...])
        xp = jnp.concatenate([cur_ref[...], halo], axis=0)
        w = w_ref[...]
        acc = w[3:4, :] * xp[0:tm, :]
        for k in range(3):
            acc = acc + w[k:k + 1, :] * xp[3 - k:3 - k + tm, :]
        o_ref[...] = acc.astype(o_ref.dtype)

    return pl.pallas_call(body, grid=(SSD_CONV_DIM // tc, nt), in_specs=[out, nxt, par4], out_specs=out,
                          out_shape=jax.ShapeDtypeStruct((T, SSD_CONV_DIM), bf16), name=name,
                          compiler_params=_params(2))(dpre, dpre, w)


def _ssd_chunk(hbase, xs, bm, cm, z, dtr, dtb, alog, dsk, ng, ht):
    L = SSD_L
    dt = jax.nn.softplus(dtr + dtb)
    da = dt * (-jnp.exp(alog))
    li, si = _iota((L, L), 0), _iota((L, L), 1)
    causal = li >= si
    cs = jnp.dot(causal.astype(f32), da, precision=HIGHEST, preferred_element_type=f32)
    cs_t = cs.T
    expand = (_iota((LANES, SSD_GW), 0) == hbase + _iota((LANES, SSD_GW), 1) // SSD_P).astype(f32)
    cs_e = jnp.dot(cs, expand, precision=HIGHEST, preferred_element_type=f32)
    dt_e = jnp.dot(dt, expand, precision=HIGHEST, preferred_element_type=f32)
    xdt = xs * dt_e
    cb = bdot_nt(cm, bm)
    chan_head = _iota((1, SSD_GW), 1) // SSD_P
    y = jnp.zeros((L, SSD_GW), f32)
    for r in range(SSD_GW // SSD_P):
        cs_col = jnp.sum(cs * (_iota((1, LANES), 1) == hbase + r).astype(f32), axis=1, keepdims=True)
        cs_row = jnp.sum(cs_t * (_iota((LANES, 1), 0) == hbase + r).astype(f32), axis=0, keepdims=True)
        decay = jnp.where(causal, jnp.exp(jnp.where(causal, cs_col - cs_row, 0.0)), 0.0)
        y = y + bdot_nn(cb * decay, xdt * (chan_head == r).astype(f32))
    y = y + jnp.exp(cs_e) * bdot_nn(cm, ht)
    cs_last = jnp.sum(cs_e * (_iota((L, 1), 0) == L - 1).astype(f32), axis=0, keepdims=True)
    ht_new = ht * jnp.exp(cs_last) + bdot_tn(bm, xdt * jnp.exp(cs_last - cs_e))
    y = (y + dsk * xs) * jax.nn.silu(z)
    return _rms(y, ng), ht_new


def _ssd_specs(T, rev):
    nc = T // SSD_L
    cidx = (lambda c: nc - 1 - c) if rev else (lambda c: c)
    return nc, dict(
        xs=pl.BlockSpec((SSD_L, SSD_GW), lambda g, c: (cidx(c), g)),
        bm=pl.BlockSpec((SSD_L, SSD_N), lambda g, c: (cidx(c), 12 + g)),
        cm=pl.BlockSpec((SSD_L, SSD_N), lambda g, c: (cidx(c), 16 + g)),
        z=pl.BlockSpec((SSD_L, SSD_GW), lambda g, c: (cidx(c), g)),
        dt=pl.BlockSpec((SSD_L, LANES), lambda g, c: (cidx(c), 36)),
        p128=pl.BlockSpec((1, LANES), lambda g, c: (0, 0)),
        pgw=pl.BlockSpec((1, SSD_GW), lambda g, c: (0, g)),
        hs=pl.BlockSpec((None, None, SSD_N, SSD_GW), lambda g, c: (g, cidx(c), 0, 0)),
        grp=pl.BlockSpec((SSD_L, SSD_N), lambda g, c: (cidx(c), g)),
    )


def _ssd_fwd(xbc, proj, dtb, alog, dsk, ng, name):
    T = proj.shape[0]
    nc, s = _ssd_specs(T, False)

    def body(xs_ref, bm_ref, cm_ref, z_ref, dt_ref, dtb_ref, alog_ref, dsk_ref, ng_ref, y_ref, hs_ref, h_scr):
        @pl.when(pl.program_id(1) == 0)
        def _():
            h_scr[...] = jnp.zeros_like(h_scr)

        ht = h_scr[...]
        hs_ref[...] = ht
        y, ht_new = _ssd_chunk(pl.program_id(0) * (SSD_GW // SSD_P), xs_ref[...], bm_ref[...], cm_ref[...], z_ref[...],
                               dt_ref[...], dtb_ref[...], alog_ref[...], dsk_ref[...], ng_ref[...], ht)
        y_ref[...] = y.astype(y_ref.dtype)
        h_scr[...] = ht_new

    return pl.pallas_call(
        body, grid=(SSD_G, nc),
        in_specs=[s["xs"], s["bm"], s["cm"], s["z"], s["dt"], s["p128"], s["p128"], s["pgw"], s["pgw"]],
        out_specs=(s["xs"], s["hs"]),
        out_shape=(jax.ShapeDtypeStruct((T, SSD_INNER), bf16), jax.ShapeDtypeStruct((SSD_G, nc, SSD_N, SSD_GW), f32)),
        scratch_shapes=[pltpu.VMEM((SSD_N, SSD_GW), f32)], name=name, compiler_params=_params(2))(
            xbc, xbc, xbc, proj, proj, dtb, alog, dsk, ng)


def _ssd_bwd(xbc, proj, dtb, alog, dsk, ng, hs, dycat, name):
    T = proj.shape[0]
    nc, s = _ssd_specs(T, True)

    def body(xs_ref, bm_ref, cm_ref, z_ref, dt_ref, dtb_ref, alog_ref, dsk_ref, ng_ref, hs_ref, dy_ref,
             dxs_ref, dbm_ref, dcm_ref, dz_ref, ddt_ref, ddtb_ref, dalog_ref, ddsk_ref, dng_ref, dh_scr):
        c = pl.program_id(1)

        @pl.when(c == 0)
        def _():
            dh_scr[...] = jnp.zeros_like(dh_scr)

        hbase = pl.program_id(0) * (SSD_GW // SSD_P)
        _, vjp = jax.vjp(functools.partial(_ssd_chunk, hbase), xs_ref[...], bm_ref[...], cm_ref[...], z_ref[...],
                         dt_ref[...], dtb_ref[...], alog_ref[...], dsk_ref[...], ng_ref[...], hs_ref[...])
        dxs, dbm, dcm, dz, ddt, ddtb, dalog, ddsk, dng, dht = vjp((dy_ref[...].astype(f32), dh_scr[...]))
        dxs_ref[...] = dxs
        dbm_ref[...] = dbm
        dcm_ref[...] = dcm
        dz_ref[...] = dz.astype(dz_ref.dtype)
        ddt_ref[...] = ddt
        dh_scr[...] = dht

        @pl.when(c == 0)
        def _():
            ddtb_ref[...] = ddtb
            dalog_ref[...] = dalog
            ddsk_ref[...] = ddsk
            dng_ref[...] = dng

        @pl.when(c > 0)
        def _():
            ddtb_ref[...] += ddtb
            dalog_ref[...] += dalog
            ddsk_ref[...] += ddsk
            dng_ref[...] += dng

    cidx = lambda c: nc - 1 - c
    g128 = pl.BlockSpec((None, 1, LANES), lambda g, c: (g, 0, 0))
    return pl.pallas_call(
        body, grid=(SSD_G, nc),
        in_specs=[s["xs"], s["bm"], s["cm"], s["z"], s["dt"], s["p128"], s["p128"], s["pgw"], s["pgw"], s["hs"], s["xs"]],
        out_specs=(s["xs"], s["grp"], s["grp"], s["xs"],
                   pl.BlockSpec((None, SSD_L, LANES), lambda g, c: (g, cidx(c), 0)), g128, g128, s["pgw"], s["pgw"]),
        out_shape=(jax.ShapeDtypeStruct((T, SSD_INNER), f32), jax.ShapeDtypeStruct((T, SSD_G * SSD_N), f32),
                   jax.ShapeDtypeStruct((T, SSD_G * SSD_N), f32), jax.ShapeDtypeStruct((T, SSD_INNER), bf16),
                   jax.ShapeDtypeStruct((SSD_G, T, LANES), f32), jax.ShapeDtypeStruct((SSD_G, 1, LANES), f32),
                   jax.ShapeDtypeStruct((SSD_G, 1, LANES), f32), jax.ShapeDtypeStruct((1, SSD_INNER), f32),
                   jax.ShapeDtypeStruct((1, SSD_INNER), f32)),
        scratch_shapes=[pltpu.VMEM((SSD_N, SSD_GW), f32)], name=name, compiler_params=_params(2))(
            xbc, xbc, xbc, proj, proj, dtb, alog, dsk, ng, hs, dycat)


def _qk_norm_fn(q, k, qg, kg):
    return _rms(q, qg), _rms(k, kg)


def _sb_qknorm_fwd(proj, qg, kg, name):
    T = proj.shape[0]
    tm = _row_tile(T)
    par = pl.BlockSpec((1, HD), lambda h, i: (0, 0))
    out = pl.BlockSpec((tm, HD), lambda h, i: (i, h))
    sds = jax.ShapeDtypeStruct((T, SB_W), bf16)
    return _fn_call(_qk_norm_fn, (proj, proj, qg, kg),
                    [out, pl.BlockSpec((tm, HD), lambda h, i: (i, SB_HEADS + h)), par, par],
                    (sds, sds), (out, out), (SB_HEADS, T // tm), name)


def _sb_qknorm_bwd(proj, qg, kg, dqn, dkn, name):
    T = proj.shape[0]
    tm = _row_tile(T)
    par = pl.BlockSpec((1, HD), lambda h, i: (0, 0))
    out = pl.BlockSpec((tm, HD), lambda h, i: (i, h))
    gout = pl.BlockSpec((1, 1, HD), lambda h, i: (h, 0, 0))

    def fn(q, k, qg, kg, dqn, dkn):
        _, vjp = jax.vjp(_qk_norm_fn, q, k, qg, kg)
        dq, dk, dqg, dkg = vjp((dqn, dkn))
        return dq, dk, dqg[None], dkg[None]

    sds = jax.ShapeDtypeStruct((T, SB_W), bf16)
    gsds = jax.ShapeDtypeStruct((SB_HEADS, 1, HD), f32)
    return _fn_call(fn, (proj, proj, qg, kg, dqn, dkn),
                    [out, pl.BlockSpec((tm, HD), lambda h, i: (i, SB_HEADS + h)), par, par, out, out],
                    (sds, sds, gsds, gsds), (out, out, gout, gout), (SB_HEADS, T // tm), name, acc={2: "last", 3: "last"})


def _split_dot(a, tri):
    hi = a.astype(bf16)
    lo = (a - hi.astype(f32)).astype(bf16)
    return jnp.dot(hi, tri, preferred_element_type=f32) + jnp.dot(lo, tri, preferred_element_type=f32)


def _sb_weights(q, kblk, run, later, mask):
    z = _dg(q, kblk, 1, 1) * SB_SCALE
    t = jnp.log(1.0 + jnp.exp(-jnp.abs(z)))
    sp = jnp.maximum(z, 0.0) + t
    log_beta = jnp.minimum(z, 0.0) - t
    if mask is not None:
        sp = jnp.where(mask, sp, 0.0)
    w = jnp.exp(log_beta - _split_dot(sp, later) - run)
    if mask is not None:
        w = jnp.where(mask, w, 0.0)
    return jnp.exp(log_beta), sp, w


def _sb_older_blocks(qb, carry, step, run_of):
    def cond(state):
        i, cr = state
        return jnp.logical_and(i < qb, jnp.min(run_of(cr)) < SB_DEAD)

    def body(state):
        i, cr = state
        return i + 1, step(qb - 1 - i, cr)

    return lax.while_loop(cond, body, (jnp.int32(0), carry))[1]


def _sb_fwd(qn, kn, proj, name):
    T = qn.shape[0]
    B = min(SB_BLK, T)
    nq = T // B

    def body(q_ref, k_ref, v_ref, o_ref, ox_ref):
        qb = pl.program_id(1)
        q = q_ref[...]
        ri, ci = _iota((B, B), 0), _iota((B, B), 1)
        later = (ri > ci).astype(bf16)

        def block(kb, carry, mask):
            acc, acc_lo, run = carry
            off = pl.multiple_of(kb * B, B)
            _, sp, w = _sb_weights(q, k_ref[pl.ds(off, B), :], run, later, mask)
            vblk = v_ref[pl.ds(off, B), :]
            w_hi = w.astype(bf16)
            acc = acc + _dg(w_hi, vblk, 1, 0)
            acc_lo = acc_lo + _dg(w - w_hi.astype(f32), vblk, 1, 0)
            return acc, acc_lo, run + jnp.sum(sp, axis=1, keepdims=True)

        zero = jnp.zeros((B, HD), f32)
        carry = block(qb, (zero, zero, jnp.zeros((B, 1), f32)), ci < ri)
        carry = _sb_older_blocks(qb, carry, lambda kb, cr: block(kb, cr, None), lambda cr: cr[2])
        o_ref[...] = carry[0].astype(o_ref.dtype)
        ox_ref[...] = carry[0] + carry[1]

    blk = pl.BlockSpec((B, HD), lambda h, i: (i, h))
    return pl.pallas_call(
        body, grid=(SB_HEADS, nq),
        in_specs=[blk, pl.BlockSpec((T, HD), lambda h, i: (0, h)), pl.BlockSpec((T, HD), lambda h, i: (0, 2 * SB_HEADS + h))],
        out_specs=(blk, blk), out_shape=(jax.ShapeDtypeStruct((T, SB_W), bf16), jax.ShapeDtypeStruct((T, SB_W), f32)),
        name=name, compiler_params=_params(2))(qn, kn, proj)


def _sb_bwd(qn, kn, proj, o, dycat, name):
    T = qn.shape[0]
    B = min(SB_BLK, T)
    nq = T // B

    def body(q_ref, k_ref, v_ref, o_ref, do_ref, dq_ref, dk_ref, dv_ref):
        qb = pl.program_id(1)

        @pl.when(qb == 0)
        def _():
            dk_ref[...] = jnp.zeros_like(dk_ref)
            dv_ref[...] = jnp.zeros_like(dv_ref)

        q = q_ref[...]
        do = do_ref[...].astype(f32)
        do_b = do.astype(bf16)
        gtot = jnp.sum(do_b.astype(f32) * o_ref[...], axis=1, keepdims=True)
        ri, ci = _iota((B, B), 0), _iota((B, B), 1)
        later = (ri > ci).astype(bf16)
        from_here = (ri >= ci).astype(bf16)

        def block(kb, carry, mask):
            dq, run, rung = carry
            off = pl.multiple_of(kb * B, B)
            kblk = k_ref[pl.ds(off, B), :]
            sig, sp, w = _sb_weights(q, kblk, run, later, mask)
            g = w * _dg(do_b, v_ref[pl.ds(off, B), :], 1, 1)
            before = gtot - rung - _split_dot(g, from_here)
            dz = (g * (1.0 - sig) - sig * before) * SB_SCALE
            if mask is not None:
                dz = jnp.where(mask, dz, 0.0)
            dz_b = dz.astype(bf16)
            dv_ref[pl.ds(off, B), :] += _dg(w, do_b, 0, 0)
            dk_ref[pl.ds(off, B), :] += _dg(dz_b, q, 0, 0)
            dq = dq + _dg(dz_b, kblk, 1, 0)
            return dq, run + jnp.sum(sp, axis=1, keepdims=True), rung + jnp.sum(g, axis=1, keepdims=True)

        zero = jnp.zeros((B, 1), f32)
        carry = block(qb, (jnp.zeros((B, HD), f32), zero, zero), ci < ri)
        carry = _sb_older_blocks(qb, carry, lambda kb, cr: block(kb, cr, None), lambda cr: cr[1])
        dq_ref[...] = carry[0]

    blk = pl.BlockSpec((B, HD), lambda h, i: (i, h))
    full = pl.BlockSpec((T, HD), lambda h, i: (0, h))
    sds = jax.ShapeDtypeStruct((T, SB_W), f32)
    return pl.pallas_call(
        body, grid=(SB_HEADS, nq),
        in_specs=[blk, full, pl.BlockSpec((T, HD), lambda h, i: (0, 2 * SB_HEADS + h)), blk, blk],
        out_specs=(blk, full, full), out_shape=(sds, sds, sds), name=name, compiler_params=_params(2))(
            qn, kn, proj, o, dycat)


_BIG = (("mem_w_kv", (4, 128, 1024), 1), ("ssd_w_in", (2, 1024, 579), 2), ("ssd_w_out", (2, 256, 1024), 1),
        ("sb_w_in", (2, 1024, 640), 2), ("sb_w_out", (2, 256, 1024), 1), ("ffn_w_gate_up", (4, 1024, 704), 2),
        ("ffn_w_down", (4, 352, 1024), 1))
_BIG_ROWS = tuple(math.prod(s) // LANES for _, s, _ in _BIG)
_BIG_TOTAL = 66560
_ADAM_TILE = 1024


def _pack_rows(parts, total):
    rows = sum(p.shape[-2] for p in parts)
    pad = jnp.zeros(parts[0].shape[:-2] + (total - rows, LANES), parts[0].dtype)
    return jnp.concatenate(list(parts) + [pad], axis=-2)


def _full_from_slots(slots, shard_shape, axis):
    n = shard_shape[0]
    s = slots.reshape((N_DEV,) + shard_shape)
    if axis == 1:
        return s.transpose(1, 0, 2, 3).reshape(n, N_DEV * shard_shape[1], shard_shape[2])
    return s.transpose(1, 2, 0, 3).reshape(n, shard_shape[1], N_DEV * shard_shape[2])


def _slots_from_full(full, shard_shape, axis):
    n = shard_shape[0]
    if axis == 1:
        s = full.reshape(n, N_DEV, shard_shape[1], shard_shape[2]).transpose(1, 0, 2, 3)
    else:
        s = full.reshape(n, shard_shape[1], N_DEV, shard_shape[2]).transpose(2, 0, 1, 3)
    return s.reshape(N_DEV, -1, LANES)


def _ssd_in_cols(w):
    pad = jnp.zeros(w.shape[:-1] + (SSD_IN_PAD - SSD_IN,), w.dtype)
    return jnp.concatenate([w[..., :4096], w[..., 4120:4632], w[..., 4096:4120], pad], axis=-1)


def _ssd_in_cols_back(w):
    return jnp.concatenate([w[..., :4096], w[..., 4608:4632], w[..., 4096:4608]], axis=-1)


def _lane_rows(a):
    flat = a.reshape(-1)
    n = -(-flat.shape[0] // (8 * LANES)) * (8 * LANES)
    return jnp.pad(flat, (0, n - flat.shape[0])).reshape(-1, LANES)


def _pad128(a):
    return jnp.pad(a, ((0, 0), (0, LANES - a.shape[1])))


def kernel(x, mem, mix_norm_g, ffn_norm_g, mem_norm_g, mem_w_kv, mem_q_norm_g, mem_k_norm_g, ssd_w_in, ssd_conv_w, ssd_conv_b, ssd_dt_bias, ssd_a_log, ssd_d, ssd_norm_g, ssd_w_out, sb_w_in, sb_q_norm_g, sb_k_norm_g, sb_w_out, ffn_w_gate_up, ffn_w_down, loss_target, m_mix_norm_g, m_ffn_norm_g, m_mem_norm_g, m_mem_w_kv, m_mem_q_norm_g, m_mem_k_norm_g, m_ssd_w_in, m_ssd_conv_w, m_ssd_conv_b, m_ssd_dt_bias, m_ssd_a_log, m_ssd_d, m_ssd_norm_g, m_ssd_w_out, m_sb_w_in, m_sb_q_norm_g, m_sb_k_norm_g, m_sb_w_out, m_ffn_w_gate_up, m_ffn_w_down, v_mix_norm_g, v_ffn_norm_g, v_mem_norm_g, v_mem_w_kv, v_mem_q_norm_g, v_mem_k_norm_g, v_ssd_w_in, v_ssd_conv_w, v_ssd_conv_b, v_ssd_dt_bias, v_ssd_a_log, v_ssd_d, v_ssd_norm_g, v_ssd_w_out, v_sb_w_in, v_sb_q_norm_g, v_sb_k_norm_g, v_sb_w_out, v_ffn_w_gate_up, v_ffn_w_down):
    W = dict(mix_norm_g=mix_norm_g, ffn_norm_g=ffn_norm_g, mem_norm_g=mem_norm_g, mem_w_kv=mem_w_kv, mem_q_norm_g=mem_q_norm_g, mem_k_norm_g=mem_k_norm_g, ssd_w_in=ssd_w_in, ssd_conv_w=ssd_conv_w, ssd_conv_b=ssd_conv_b, ssd_dt_bias=ssd_dt_bias, ssd_a_log=ssd_a_log, ssd_d=ssd_d, ssd_norm_g=ssd_norm_g, ssd_w_out=ssd_w_out, sb_w_in=sb_w_in, sb_q_norm_g=sb_q_norm_g, sb_k_norm_g=sb_k_norm_g, sb_w_out=sb_w_out, ffn_w_gate_up=ffn_w_gate_up, ffn_w_down=ffn_w_down)
    M = dict(mix_norm_g=m_mix_norm_g, ffn_norm_g=m_ffn_norm_g, mem_norm_g=m_mem_norm_g, mem_w_kv=m_mem_w_kv, mem_q_norm_g=m_mem_q_norm_g, mem_k_norm_g=m_mem_k_norm_g, ssd_w_in=m_ssd_w_in, ssd_conv_w=m_ssd_conv_w, ssd_conv_b=m_ssd_conv_b, ssd_dt_bias=m_ssd_dt_bias, ssd_a_log=m_ssd_a_log, ssd_d=m_ssd_d, ssd_norm_g=m_ssd_norm_g, ssd_w_out=m_ssd_w_out, sb_w_in=m_sb_w_in, sb_q_norm_g=m_sb_q_norm_g, sb_k_norm_g=m_sb_k_norm_g, sb_w_out=m_sb_w_out, ffn_w_gate_up=m_ffn_w_gate_up, ffn_w_down=m_ffn_w_down)
    V = dict(mix_norm_g=v_mix_norm_g, ffn_norm_g=v_ffn_norm_g, mem_norm_g=v_mem_norm_g, mem_w_kv=v_mem_w_kv, mem_q_norm_g=v_mem_q_norm_g, mem_k_norm_g=v_mem_k_norm_g, ssd_w_in=v_ssd_w_in, ssd_conv_w=v_ssd_conv_w, ssd_conv_b=v_ssd_conv_b, ssd_dt_bias=v_ssd_dt_bias, ssd_a_log=v_ssd_a_log, ssd_d=v_ssd_d, ssd_norm_g=v_ssd_norm_g, ssd_w_out=v_ssd_w_out, sb_w_in=v_sb_w_in, sb_q_norm_g=v_sb_q_norm_g, sb_k_norm_g=v_sb_k_norm_g, sb_w_out=v_sb_w_out, ffn_w_gate_up=v_ffn_w_gate_up, ffn_w_down=v_ffn_w_down)
    names = list(W)
    T = x.shape[1]
    x0 = x.reshape(T, D)
    mem2 = mem.reshape(MEM_LEN, D)
    target = loss_target.reshape(T, D)
    my_dev = 4 * lax.axis_index("x") + 2 * lax.axis_index("y") + lax.axis_index("c")

    w_flat = _pack_rows([W[n].reshape(-1, LANES) for n, _, _ in _BIG], _BIG_TOTAL)
    slots = _allgather_hbm(w_flat.astype(bf16), "allgather_weights")
    full, off = {}, 0
    for (n, shp, ax), rows in zip(_BIG, _BIG_ROWS):
        full[n] = _full_from_slots(slots[:, off:off + rows], shp, ax)
        off += rows
    full["ssd_w_in"] = _ssd_in_cols(full["ssd_w_in"])
    conv_slots = _allgather_vmem(_lane_rows(ssd_conv_w), "allgather_conv_w")
    conv_w = _full_from_slots(conv_slots[:, :20], (2, 4, 320), 2)

    mem_g = mem_norm_g.reshape(1, D)

    saved = []
    xc = x0
    for i in range(DEPTH):
        j = i // 2
        ssd = i % 2 == 0
        L = f"l{i}_"
        mix_g = mix_norm_g[i:i + 1]
        h = _rmsnorm_fwd(xc, mix_g, L + "mix_norm")
        w_in = full["ssd_w_in"][j] if ssd else full["sb_w_in"][j]
        proj = _matmul(h, w_in, tm=256 if ssd else 512, tn=w_in.shape[1] if ssd else 512, tk=D, name=L + "in_proj")
        k_mem, v_mem = _memkv_fwd(mem2, mem_g, full["mem_w_kv"][i], mem_k_norm_g[i:i + 1], L + "mem_kv")
        q_col = 32 if ssd else 36
        o_mem = _memattn_fwd(proj, q_col, k_mem, v_mem, mem_q_norm_g[i:i + 1], L + "mem_attn")
        st = dict(x_in=xc, h=h, proj=proj, k_mem=k_mem, v_mem=v_mem)
        if ssd:
            xbc = _conv_fwd(proj, conv_w[j], ssd_conv_b[j:j + 1], L + "conv")
            dtb, alog = _pad128(ssd_dt_bias[j:j + 1]), _pad128(ssd_a_log[j:j + 1])
            dsk = jnp.repeat(ssd_d[j], SSD_P).reshape(1, SSD_INNER)
            y, hs = _ssd_fwd(xbc, proj, dtb, alog, dsk, ssd_norm_g[j:j + 1], L + "ssd_scan")
            st.update(xbc=xbc, hs=hs, dtb=dtb, alog=alog, dsk=dsk)
            w_out = full["ssd_w_out"][j]
        else:
            qn, kn = _sb_qknorm_fwd(proj, sb_q_norm_g[j:j + 1], sb_k_norm_g[j:j + 1], L + "qk_norm")
            y, o_exact = _sb_fwd(qn, kn, proj, L + "sb_attn")
            st.update(qn=qn, kn=kn, o=o_exact)
            w_out = full["sb_w_out"][j]
        ycat = jnp.concatenate([y, o_mem], axis=1)
        x_mid = _matmul(ycat, w_out, tm=512, tn=D, tk=2048, res=xc, name=L + "out_proj")
        h2 = _rmsnorm_fwd(x_mid, ffn_norm_g[i:i + 1], L + "ffn_norm")
        gu = _matmul(h2, full["ffn_w_gate_up"][i], tm=512, tn=512, tk=D, name=L + "ffn_up")
        act = _ffn_act_fwd(gu, L + "ffn_act")
        xc = _matmul(act, full["ffn_w_down"][i], tm=512, tn=D, tk=FFN_H, res=x_mid, name=L + "ffn_down")
        st.update(ycat=ycat, x_mid=x_mid, h2=h2, gu=gu, act=act, w_in=w_in, w_out=w_out)
        saved.append(st)

    dx, loss_part = _loss_head(xc, target, "loss_head")
    loss = lax.psum(jnp.sum(loss_part), ("x", "y", "c"))

    G = {n: [None] * W[n].shape[0] for n in names if W[n].ndim > 1}
    d_mem_g = jnp.zeros((1, D), f32)
    for i in reversed(range(DEPTH)):
        j = i // 2
        ssd = i % 2 == 0
        L = f"l{i}_b_"
        st = saved[i]
        proj = st["proj"]
        dact = _matmul(dx, full["ffn_w_down"][i], tb=True, tm=512, tn=FFN_H, tk=D, name=L + "d_act")
        G["ffn_w_down"][i] = _matmul(st["act"], dx, ta=True, tm=256, tn=D, tk=512, name=L + "dw_down")
        dgu = _ffn_act_bwd(st["gu"], dact, L + "d_gu")
        dh2 = _matmul(dgu, full["ffn_w_gate_up"][i], tb=True, tm=512, tn=D, tk=2 * FFN_H, name=L + "d_h2")
        G["ffn_w_gate_up"][i] = _matmul(st["h2"], dgu, ta=True, tm=D, tn=512, tk=512, name=L + "dw_up")
        dx, G["ffn_norm_g"][i] = _rmsnorm_bwd(st["x_mid"], ffn_norm_g[i:i + 1], dh2, dx, L + "d_ffn_norm")
        dycat = _matmul(dx, st["w_out"], tb=True, tm=512, tn=2048, tk=D, name=L + "d_ycat")
        g_out = _matmul(st["ycat"], dx, ta=True, tm=512, tn=D, tk=512, name=L + "dw_out")
        q_col = 32 if ssd else 36
        dq_mem, dk_mem, dv_mem, dqg = _memattn_bwd(proj, q_col, st["k_mem"], st["v_mem"], mem_q_norm_g[i:i + 1], dycat, 12, L + "d_mem_attn")
        G["mem_q_norm_g"][i] = jnp.sum(dqg, axis=0)
        dmg, G["mem_w_kv"][i], G["mem_k_norm_g"][i] = _memkv_bwd(mem2, mem_g, full["mem_w_kv"][i], mem_k_norm_g[i:i + 1], dk_mem, dv_mem, L + "d_mem_kv")
        d_mem_g = d_mem_g + dmg
        if ssd:
            G["ssd_w_out"][j] = g_out
            dxs, dbm, dcm, dz, ddt, ddtb, dalog, ddsk, dng = _ssd_bwd(
                st["xbc"], proj, st["dtb"], st["alog"], st["dsk"], ssd_norm_g[j:j + 1], st["hs"], dycat, L + "d_ssd_scan")
            G["ssd_dt_bias"][j] = jnp.sum(ddtb, axis=0)[:, :SSD_HEADS]
            G["ssd_a_log"][j] = jnp.sum(dalog, axis=0)[:, :SSD_HEADS]
            G["ssd_d"][j] = jnp.sum(ddsk.reshape(SSD_HEADS, SSD_P), axis=1).reshape(1, SSD_HEADS)
            G["ssd_norm_g"][j] = dng
            dxbc_act = jnp.concatenate([dxs, dbm, dcm], axis=1)
            dpre, G["ssd_conv_w"][j], G["ssd_conv_b"][j] = _conv_bwd_pre(proj, conv_w[j], ssd_conv_b[j:j + 1], dxbc_act, L + "d_conv_pre")
            dxbc = _conv_bwd_in(dpre, conv_w[j], L + "d_conv_in")
            ddt_all = jnp.sum(ddt, axis=0).astype(bf16)
            dproj = jnp.concatenate([dz, dxbc, dq_mem, ddt_all], axis=1)
        else:
            G["sb_w_out"][j] = g_out
            dqn, dkn, dv = _sb_bwd(st["qn"], st["kn"], proj, st["o"], dycat, L + "d_sb_attn")
            dq, dk, dqg2, dkg2 = _sb_qknorm_bwd(proj, sb_q_norm_g[j:j + 1], sb_k_norm_g[j:j + 1], dqn, dkn, L + "d_qk_norm")
            G["sb_q_norm_g"][j] = jnp.sum(dqg2, axis=0)
            G["sb_k_norm_g"][j] = jnp.sum(dkg2, axis=0)
            dproj = jnp.concatenate([dq, dk, dv.astype(bf16), dq_mem], axis=1)
        n_in = dproj.shape[1]
        dh = _matmul(dproj, st["w_in"], tb=True, tm=512, tn=D, tk=n_in, name=L + "d_h")
        g_in = _matmul(st["h"], dproj, ta=True, tm=256 if ssd else D, tn=n_in if ssd else 512, tk=512, name=L + "dw_in")
        if ssd:
            G["ssd_w_in"][j] = _ssd_in_cols_back(g_in)
        else:
            G["sb_w_in"][j] = g_in
        dx, G["mix_norm_g"][i] = _rmsnorm_bwd(st["x_in"], mix_norm_g[i:i + 1], dh, dx, L + "d_mix_norm")

    grad_x = dx.reshape(x.shape)

    g_slots = _pack_rows([_slots_from_full(jnp.stack(G[n]), shp, ax).astype(bf16) for n, shp, ax in _BIG], _BIG_TOTAL)
    parts = _exchange_slices(g_slots, "exchange_grads")
    m_flat = _pack_rows([M[n].reshape(-1, LANES) for n, _, _ in _BIG], _BIG_TOTAL)
    v_flat = _pack_rows([V[n].reshape(-1, LANES) for n, _, _ in _BIG], _BIG_TOTAL)
    res_big = _adamw_reduce(parts, w_flat, m_flat, v_flat, _ADAM_TILE, "adamw_big")
    out = {}
    off = 0
    for (n, shp, _), rows in zip(_BIG, _BIG_ROWS):
        out[n] = tuple(r[off:off + rows].reshape(shp) for r in res_big)
        off += rows

    small = [n for n in names if n not in out and n != "ssd_conv_w"]
    G["mem_norm_g"] = d_mem_g.reshape(D)
    small_grads = [_lane_rows(G[n] if n == "mem_norm_g" else jnp.concatenate(G[n], axis=0)) for n in small]
    conv_grad = _lane_rows(jnp.stack(G["ssd_conv_w"]))
    sm_rows = [g.shape[0] for g in small_grads]
    n_small = sum(sm_rows)
    sm_total = -(-(n_small + conv_grad.shape[0]) // 8) * 8
    gathered = _allgather_vmem(_pack_rows(small_grads + [conv_grad], sm_total), "allgather_small_grads")
    whole = lambda s: pl.BlockSpec(s, lambda i: (0,) * len(s))
    g_sum = _fn_call(_sum_slots, (gathered,), [whole((N_DEV, sm_total, LANES))],
                     jax.ShapeDtypeStruct((sm_total, LANES), f32), whole((sm_total, LANES)), (1,), "sum_small_grads")
    conv_full = g_sum[n_small:n_small + 160].reshape(2, 4, SSD_CONV_DIM)
    conv_mine = lax.dynamic_slice_in_dim(conv_full, my_dev * 320, 320, axis=2)
    ad_total = n_small + 24
    pack = lambda d: _pack_rows([_lane_rows(d[n]) for n in small] + [_lane_rows(d["ssd_conv_w"])], ad_total)
    g_pack = _pack_rows([g_sum[:n_small], _lane_rows(conv_mine)], ad_total)
    blk = whole((ad_total, LANES))
    res_small = _fn_call(lambda g, w, m, v: _adamw_math(w, g, m, v), (g_pack, pack(W), pack(M), pack(V)), [blk] * 4,
                         (jax.ShapeDtypeStruct((ad_total, LANES), f32),) * 3, (blk,) * 3, (1,), "adamw_small")
    res_small = (g_pack,) + tuple(res_small)
    off = 0
    for n, rows in zip(small + ["ssd_conv_w"], sm_rows + [24]):
        size = W[n].size
        out[n] = tuple(r[off:off + rows].reshape(-1)[:size].reshape(W[n].shape) for r in res_small)
        off += rows

    return (loss, grad_x, *[out[n][0] for n in names], *[out[n][1] for n in names],
            *[out[n][2] for n in names], *[out[n][3] for n in names])
```

```python
import functools
import math

import jax
import jax.numpy as jnp
from jax import lax
from jax.experimental import pallas as pl
from jax.experimental.pallas import tpu as pltpu

f32, bf16 = jnp.float32, jnp.bfloat16
MESH = pl.DeviceIdType.MESH

N_DEV = 8
D = 1024
DEPTH = 4
EPS = 1e-6
MEM_LEN, MEM_HEADS, MEM_W, HD = 256, 4, 512, 128
SSD_INNER, SSD_HEADS, SSD_G, SSD_P, SSD_N, SSD_L = 1536, 24, 4, 64, 128, 128
SSD_GW = SSD_INNER // SSD_G
SSD_CONV_DIM = 2560
SSD_IN = 4632
SSD_IN_PAD = 4736
SB_W, SB_HEADS, SB_IN = 1536, 12, 5120
SB_BLK = 256
SB_SCALE = HD ** -0.5
SB_DEAD = 105.0
FFN_H = 2816
LANES = 128
VMEM_LIMIT = 56 * 1024 * 1024

ADAM_LR, ADAM_B1, ADAM_B2, ADAM_EPS, ADAM_WD, ADAM_STEP = 0.001, 0.9, 0.999, 1e-08, 0.01, 10

HIGHEST = lax.Precision.HIGHEST


def _params(n_grid):
    return pltpu.CompilerParams(dimension_semantics=("arbitrary",) * n_grid, vmem_limit_bytes=VMEM_LIMIT)


def _dg(a, b, ca, cb):
    return lax.dot_general(a.astype(bf16), b.astype(bf16), (((ca,), (cb,)), ((), ())), preferred_element_type=f32)


@jax.custom_vjp
def bdot_nn(a, b):
    return _dg(a, b, 1, 0)


def _nn_fwd(a, b):
    return _dg(a, b, 1, 0), (a, b)


def _nn_bwd(res, ct):
    a, b = res
    return _dg(ct, b, 1, 1).astype(a.dtype), _dg(a, ct, 0, 0).astype(b.dtype)


bdot_nn.defvjp(_nn_fwd, _nn_bwd)


@jax.custom_vjp
def bdot_nt(a, b):
    return _dg(a, b, 1, 1)


def _nt_fwd(a, b):
    return _dg(a, b, 1, 1), (a, b)


def _nt_bwd(res, ct):
    a, b = res
    return _dg(ct, b, 1, 0).astype(a.dtype), _dg(ct, a, 0, 0).astype(b.dtype)


bdot_nt.defvjp(_nt_fwd, _nt_bwd)


@jax.custom_vjp
def bdot_tn(a, b):
    return _dg(a, b, 0, 0)


def _tn_fwd(a, b):
    return _dg(a, b, 0, 0), (a, b)


def _tn_bwd(res, ct):
    a, b = res
    return _dg(b, ct, 1, 1).astype(a.dtype), _dg(a, ct, 1, 0).astype(b.dtype)


bdot_tn.defvjp(_tn_fwd, _tn_bwd)


def _rms(x, g):
    return x * lax.rsqrt(jnp.mean(x * x, axis=-1, keepdims=True) + EPS) * g


def _iota(shape, axis):
    return lax.broadcasted_iota(jnp.int32, shape, axis)


def _fn_call(fn, args, in_specs, out_shapes, out_specs, grid, name, acc=None):
    n_in = len(args)
    acc = acc or {}
    n_grid = len(grid)

    def body(*refs):
        ins, outs = refs[:n_in], refs[n_in:]
        res = fn(*[r[...] for r in ins])
        if not isinstance(res, (tuple, list)):
            res = (res,)
        for k, (o, r) in enumerate(zip(outs, res)):
            mode = acc.get(k)
            if mode is None:
                o[...] = r.astype(o.dtype)
                continue
            if mode == "last":
                first = pl.program_id(n_grid - 1) == 0
            else:
                first = functools.reduce(jnp.logical_and, [pl.program_id(d) == 0 for d in range(n_grid)])

            @pl.when(first)
            def _(o=o, r=r):
                o[...] = r.astype(o.dtype)

            @pl.when(jnp.logical_not(first))
            def _(o=o, r=r):
                o[...] += r.astype(o.dtype)

    return pl.pallas_call(
        body, grid=grid, in_specs=in_specs, out_specs=out_specs, out_shape=out_shapes, name=name,
        compiler_params=_params(n_grid))(*args)


def _matmul(a, b, *, ta=False, tb=False, out_dtype=f32, tm, tn, tk, res=None, name):
    M, K = (a.shape[1], a.shape[0]) if ta else a.shape
    N = b.shape[0] if tb else b.shape[1]
    tm, tn, tk = min(tm, M), min(tn, N), min(tk, K)
    assert M % tm == 0 and N % tn == 0 and K % tk == 0, (name, M, N, K, tm, tn, tk)
    nk = K // tk
    a_spec = pl.BlockSpec((tk, tm), lambda i, j, k: (k, i)) if ta else pl.BlockSpec((tm, tk), lambda i, j, k: (i, k))
    b_spec = pl.BlockSpec((tn, tk), lambda i, j, k: (j, k)) if tb else pl.BlockSpec((tk, tn), lambda i, j, k: (k, j))
    o_spec = pl.BlockSpec((tm, tn), lambda i, j, k: (i, j))
    ca, cb = (0 if ta else 1), (1 if tb else 0)

    def body(*refs):
        a_ref, b_ref = refs[:2]
        r_ref = None if res is None else refs[2]
        o_ref = refs[2 if res is None else 3]
        part = _dg(a_ref[...], b_ref[...], ca, cb)

        def finish(out):
            if r_ref is not None:
                out = out + r_ref[...].astype(f32)
            o_ref[...] = out.astype(o_ref.dtype)

        if nk == 1:
            finish(part)
            return
        acc_ref = refs[-1]
        k = pl.program_id(2)

        @pl.when(k == 0)
        def _():
            acc_ref[...] = part

        @pl.when(k > 0)
        def _():
            acc_ref[...] += part

        @pl.when(k == nk - 1)
        def _():
            finish(acc_ref[...])

    args = (a, b) if res is None else (a, b, res)
    in_specs = [a_spec, b_spec] + ([] if res is None else [o_spec])
    return pl.pallas_call(
        body, grid=(M // tm, N // tn, nk), in_specs=in_specs, out_specs=o_spec,
        out_shape=jax.ShapeDtypeStruct((M, N), out_dtype), name=name,
        scratch_shapes=[] if nk == 1 else [pltpu.VMEM((tm, tn), f32)], compiler_params=_params(3))(*args)


def _row_tile(T):
    return min(T, 512)


def _my_pos():
    return lax.axis_index("x"), lax.axis_index("y"), lax.axis_index("c")


def _allgather_hbm(xs, name):
    R, C = xs.shape

    def body(x_ref, out_ref, send_sems, recv_sems, local_sem):
        x, y, c = _my_pos()
        me, sibling = (x, y, c), (x, y, 1 - c)
        chips = [(1 - x, y), (x, 1 - y), (1 - x, 1 - y)]

        def slot(px, py, pc):
            return out_ref.at[4 * px + 2 * py + pc]

        def copy(k, block, to, src=None):
            return pltpu.make_async_remote_copy(
                src_ref=slot(*block) if src is None else src, dst_ref=slot(*block),
                send_sem=send_sems.at[k], recv_sem=recv_sems.at[k], device_id=to, device_id_type=MESH)

        mine = pltpu.make_async_copy(x_ref, slot(*me), local_sem)
        mine.start()
        first = [copy(0, me, sibling, src=x_ref)]
        first += [copy(1 + j, me, (*chip, c), src=x_ref) for j, chip in enumerate(chips)]
        for cp in first:
            cp.start()
        passed = [copy(4 + j, (*chip, c), sibling) for j, chip in enumerate(chips)]
        for j, chip in enumerate(chips):
            copy(1 + j, (*chip, c), me).wait_recv()
            passed[j].start()
        copy(0, sibling, me).wait_recv()
        for j, chip in enumerate(chips):
            copy(4 + j, (*chip, 1 - c), me).wait_recv()
        for cp in first + passed:
            cp.wait_send()
        mine.wait()

    return pl.pallas_call(
        body, out_shape=jax.ShapeDtypeStruct((N_DEV, R, C), xs.dtype),
        in_specs=[pl.BlockSpec(memory_space=pl.ANY)], out_specs=pl.BlockSpec(memory_space=pl.ANY),
        scratch_shapes=[pltpu.SemaphoreType.DMA((7,)), pltpu.SemaphoreType.DMA((7,)), pltpu.SemaphoreType.DMA],
        name=name)(xs)


def _allgather_vmem(xs, name):
    R, C = xs.shape

    def body(x_ref, out_ref, send_sems, recv_sems):
        x, y, c = _my_pos()
        me = 4 * x + 2 * y + c
        out_ref[me] = x_ref[...]
        copies = []
        for k in range(1, N_DEV):
            px = 1 - x if k & 4 else x
            py = 1 - y if k & 2 else y
            pc = 1 - c if k & 1 else c
            cp = pltpu.make_async_remote_copy(
                src_ref=x_ref, dst_ref=out_ref.at[me], send_sem=send_sems.at[k - 1], recv_sem=recv_sems.at[k - 1],
                device_id=(px, py, pc), device_id_type=MESH)
            cp.start()
            copies.append(cp)
        for cp in copies:
            cp.wait()

    return pl.pallas_call(
        body, out_shape=jax.ShapeDtypeStruct((N_DEV, R, C), xs.dtype),
        in_specs=[pl.BlockSpec(memory_space=pltpu.VMEM)], out_specs=pl.BlockSpec(memory_space=pltpu.VMEM),
        scratch_shapes=[pltpu.SemaphoreType.DMA((7,)), pltpu.SemaphoreType.DMA((7,))], name=name)(xs)


def _exchange_slices(g, name):
    _, R, C = g.shape

    def body(g_ref, out_ref, send_sems, recv_sems, local_sem):
        x, y, c = _my_pos()
        me = 4 * x + 2 * y + c
        mine = pltpu.make_async_copy(g_ref.at[me], out_ref.at[me], local_sem)
        mine.start()
        copies = []
        for k in range(1, N_DEV):
            px = 1 - x if k & 4 else x
            py = 1 - y if k & 2 else y
            pc = 1 - c if k & 1 else c
            cp = pltpu.make_async_remote_copy(
                src_ref=g_ref.at[4 * px + 2 * py + pc], dst_ref=out_ref.at[me],
                send_sem=send_sems.at[k - 1], recv_sem=recv_sems.at[k - 1],
                device_id=(px, py, pc), device_id_type=MESH)
            cp.start()
            copies.append(cp)
        for cp in copies:
            cp.wait()
        mine.wait()

    return pl.pallas_call(
        body, out_shape=jax.ShapeDtypeStruct(g.shape, g.dtype),
        in_specs=[pl.BlockSpec(memory_space=pl.ANY)], out_specs=pl.BlockSpec(memory_space=pl.ANY),
        scratch_shapes=[pltpu.SemaphoreType.DMA((7,)), pltpu.SemaphoreType.DMA((7,)), pltpu.SemaphoreType.DMA],
        name=name)(g)


def _adamw_math(w, g, m, v):
    m = ADAM_B1 * m + (1.0 - ADAM_B1) * g
    v = ADAM_B2 * v + (1.0 - ADAM_B2) * jnp.square(g)
    m_hat = m / (1.0 - ADAM_B1 ** ADAM_STEP)
    v_hat = v / (1.0 - ADAM_B2 ** ADAM_STEP)
    delta = -ADAM_LR * (m_hat / (jnp.sqrt(v_hat) + ADAM_EPS) + ADAM_WD * w)
    return delta, m, v


def _sum_slots(parts):
    g = parts[0].astype(f32)
    for i in range(1, N_DEV):
        g = g + parts[i].astype(f32)
    return g


def _adamw_reduce(parts, w, m, v, tr, name):
    R = w.shape[0]
    assert R % tr == 0

    def fn(p, w, m, v):
        g = _sum_slots(p)
        return (g,) + _adamw_math(w, g, m, v)

    row = pl.BlockSpec((tr, LANES), lambda i: (i, 0))
    sds = jax.ShapeDtypeStruct((R, LANES), f32)
    return _fn_call(fn, (parts, w, m, v), [pl.BlockSpec((N_DEV, tr, LANES), lambda i: (0, i, 0)), row, row, row],
                    (sds,) * 4, (row,) * 4, (R // tr,), name)


def _rmsnorm_fwd(x, g, name):
    T = x.shape[0]
    tm = _row_tile(T)
    row = pl.BlockSpec((tm, D), lambda i: (i, 0))
    par = pl.BlockSpec((1, D), lambda i: (0, 0))
    return _fn_call(lambda x, g: _rms(x, g), (x, g), [row, par], jax.ShapeDtypeStruct((T, D), bf16), row, (T // tm,), name)


def _rmsnorm_bwd(x, g, dh, dres, name):
    T = x.shape[0]
    tm = _row_tile(T)
    row = pl.BlockSpec((tm, D), lambda i: (i, 0))
    par = pl.BlockSpec((1, D), lambda i: (0, 0))

    def fn(x, g, dh, dres):
        _, vjp = jax.vjp(_rms, x, g)
        dx, dg = vjp(dh.astype(f32))
        return dx + dres, dg

    return _fn_call(fn, (x, g, dh, dres), [row, par, row, row],
                    (jax.ShapeDtypeStruct((T, D), f32), jax.ShapeDtypeStruct((1, D), f32)), (row, par),
                    (T // tm,), name, acc={1: "all"})


def _swiglu_act(g, u):
    return jax.nn.silu(g) * u


def _ffn_up_act(h2, w_gu, name):
    T = h2.shape[0]
    tm = min(T, 256)

    def body(h_ref, w_ref, gu_ref, act_ref):
        gu = _dg(h_ref[...], w_ref[...], 1, 0)
        gu_ref[...] = gu
        act_ref[...] = _swiglu_act(gu[:, :FFN_H], gu[:, FFN_H:]).astype(act_ref.dtype)

    return pl.pallas_call(
        body, grid=(T // tm,),
        in_specs=[pl.BlockSpec((tm, D), lambda i: (i, 0)), pl.BlockSpec((D, 2 * FFN_H), lambda i: (0, 0))],
        out_specs=(pl.BlockSpec((tm, 2 * FFN_H), lambda i: (i, 0)), pl.BlockSpec((tm, FFN_H), lambda i: (i, 0))),
        out_shape=(jax.ShapeDtypeStruct((T, 2 * FFN_H), f32), jax.ShapeDtypeStruct((T, FFN_H), bf16)),
        name=name, compiler_params=_params(1))(h2, w_gu)


def _ffn_dgu(dx, w_down, gu, name):
    T = dx.shape[0]
    tm = min(T, 256)

    def body(dx_ref, w_ref, gu_ref, o_ref):
        dact = _dg(dx_ref[...], w_ref[...], 1, 1)
        gu = gu_ref[...]
        _, vjp = jax.vjp(_swiglu_act, gu[:, :FFN_H], gu[:, FFN_H:])
        dg, du = vjp(dact)
        o_ref[:, :FFN_H] = dg.astype(o_ref.dtype)
        o_ref[:, FFN_H:] = du.astype(o_ref.dtype)

    return pl.pallas_call(
        body, grid=(T // tm,),
        in_specs=[pl.BlockSpec((tm, D), lambda i: (i, 0)), pl.BlockSpec((FFN_H, D), lambda i: (0, 0)),
                  pl.BlockSpec((tm, 2 * FFN_H), lambda i: (i, 0))],
        out_specs=pl.BlockSpec((tm, 2 * FFN_H), lambda i: (i, 0)),
        out_shape=jax.ShapeDtypeStruct((T, 2 * FFN_H), bf16), name=name, compiler_params=_params(1))(dx, w_down, gu)


def _loss_head(x, target, name):
    T = x.shape[0]
    tm = _row_tile(T)
    row = pl.BlockSpec((tm, D), lambda i: (i, 0))
    par = pl.BlockSpec((1, LANES), lambda i: (0, 0))

    def fn(x, t):
        e = x - t
        s = jnp.sum(e * e, axis=0, keepdims=True)
        part = s[:, 0:LANES]
        for k in range(1, D // LANES):
            part = part + s[:, k * LANES:(k + 1) * LANES]
        return e * (1.0 / D), part * (0.5 / D)

    return _fn_call(fn, (x, target), [row, row],
                    (jax.ShapeDtypeStruct((T, D), f32), jax.ShapeDtypeStruct((1, LANES), f32)), (row, par),
                    (T // tm,), name, acc={1: "all"})


def _memkv_fn(mem, mg, wkv, kg):
    mn = _rms(mem, mg)
    kv = bdot_nn(mn, wkv)
    ks = [_rms(kv[:, h * HD:(h + 1) * HD], kg) for h in range(MEM_HEADS)]
    return jnp.concatenate(ks, axis=1), kv[:, MEM_W:]


def _memkv_fwd(mem, mg, wkv, kg, name):
    whole = lambda s: pl.BlockSpec(s, lambda i: (0,) * len(s))
    sds = jax.ShapeDtypeStruct((MEM_LEN, MEM_W), f32)
    return _fn_call(lambda m, g, w, k: _memkv_fn(m, g, w.astype(f32), k), (mem, mg, wkv, kg),
                    [whole((MEM_LEN, D)), whole((1, D)), whole((D, 2 * MEM_W)), whole((1, HD))],
                    (sds, sds), (whole((MEM_LEN, MEM_W)),) * 2, (1,), name)


def _memkv_bwd(mem, mg, wkv, kg, dk, dv, name):
    whole = lambda s: pl.BlockSpec(s, lambda i: (0,) * len(s))

    def fn(m, g, w, k, dk, dv):
        _, vjp = jax.vjp(lambda g, w, k: _memkv_fn(m, g, w, k), g, w.astype(f32), k)
        return vjp((dk, dv))

    return _fn_call(fn, (mem, mg, wkv, kg, dk, dv),
                    [whole((MEM_LEN, D)), whole((1, D)), whole((D, 2 * MEM_W)), whole((1, HD)),
                     whole((MEM_LEN, MEM_W)), whole((MEM_LEN, MEM_W))],
                    (jax.ShapeDtypeStruct((1, D), f32), jax.ShapeDtypeStruct((D, 2 * MEM_W), f32),
                     jax.ShapeDtypeStruct((1, HD), f32)),
                    (whole((1, D)), whole((D, 2 * MEM_W)), whole((1, HD))), (1,), name)


def _memattn_fn(q, k, v, qg):
    qn = _rms(q, qg)
    s = bdot_nt(qn, k) * (HD ** -0.5)
    s = s - jnp.max(s, axis=-1, keepdims=True)
    p = jnp.exp(s)
    p = p / jnp.sum(p, axis=-1, keepdims=True)
    return bdot_nn(p, v)


def _memattn_fwd(proj, q_col, k, v, qg, name):
    T = proj.shape[0]
    tm = _row_tile(T)
    return _fn_call(_memattn_fn, (proj, k, v, qg),
                    [pl.BlockSpec((tm, HD), lambda h, i: (i, q_col + h)), pl.BlockSpec((MEM_LEN, HD), lambda h, i: (0, h)),
                     pl.BlockSpec((MEM_LEN, HD), lambda h, i: (0, h)), pl.BlockSpec((1, HD), lambda h, i: (0, 0))],
                    jax.ShapeDtypeStruct((T, MEM_W), bf16), pl.BlockSpec((tm, HD), lambda h, i: (i, h)),
                    (MEM_HEADS, T // tm), name)


def _memattn_bwd(proj, q_col, k, v, qg, dycat, do_col, name):
    T = proj.shape[0]
    tm = _row_tile(T)

    def fn(q, k, v, qg, do):
        _, vjp = jax.vjp(_memattn_fn, q, k, v, qg)
        dq, dk, dv, dg = vjp(do.astype(f32))
        return dq, dk, dv, dg[None]

    kv_spec = pl.BlockSpec((MEM_LEN, HD), lambda h, i: (0, h))
    kv_sds = jax.ShapeDtypeStruct((MEM_LEN, MEM_W), f32)
    return _fn_call(fn, (proj, k, v, qg, dycat),
                    [pl.BlockSpec((tm, HD), lambda h, i: (i, q_col + h)), kv_spec, kv_spec,
                     pl.BlockSpec((1, HD), lambda h, i: (0, 0)), pl.BlockSpec((tm, HD), lambda h, i: (i, do_col + h))],
                    (jax.ShapeDtypeStruct((T, MEM_W), bf16), kv_sds, kv_sds, jax.ShapeDtypeStruct((MEM_HEADS, 1, HD), f32)),
                    (pl.BlockSpec((tm, HD), lambda h, i: (i, h)), kv_spec, kv_spec,
                     pl.BlockSpec((1, 1, HD), lambda h, i: (h, 0, 0))),
                    (MEM_HEADS, T // tm), name, acc={1: "last", 2: "last", 3: "last"})


def _conv_taps(xp, w, first, tm):
    out = w[0:1, :] * xp[first:first + tm, :]
    for k in range(1, 4):
        out = out + w[k:k + 1, :] * xp[first + k:first + k + tm, :]
    return out


def _conv_blocks(T):
    tm, tc = _row_tile(T), 512
    nt = T // tm
    cur = pl.BlockSpec((tm, tc), lambda j, i: (i, 3 + j))
    prev = pl.BlockSpec((8, tc), lambda j, i: (jnp.maximum(i * (tm // 8) - 1, 0), 3 + j))
    par4 = pl.BlockSpec((4, tc), lambda j, i: (0, j))
    par1 = pl.BlockSpec((1, tc), lambda j, i: (0, j))
    out = pl.BlockSpec((tm, tc), lambda j, i: (i, j))
    return tm, tc, nt, cur, prev, par4, par1, out


def _conv_fwd(proj, w, b, name):
    T = proj.shape[0]
    tm, tc, nt, cur, prev, par4, par1, out = _conv_blocks(T)

    def body(prev_ref, cur_ref, w_ref, b_ref, o_ref):
        halo = jnp.where(pl.program_id(1) == 0, 0.0, prev_ref[...])
        xp = jnp.concatenate([halo, cur_ref[...]], axis=0)
        o_ref[...] = jax.nn.silu(_conv_taps(xp, w_ref[...], 5, tm) + b_ref[...])

    return pl.pallas_call(body, grid=(SSD_CONV_DIM // tc, nt), in_specs=[prev, cur, par4, par1], out_specs=out,
                          out_shape=jax.ShapeDtypeStruct((T, SSD_CONV_DIM), f32), name=name,
                          compiler_params=_params(2))(proj, proj, w, b)


def _conv_bwd_pre(proj, w, b, dact, name):
    T = proj.shape[0]
    tm, tc, nt, cur, prev, par4, par1, out = _conv_blocks(T)

    def body(prev_ref, cur_ref, w_ref, b_ref, da_ref, dp_ref, dw_ref, db_ref):
        i = pl.program_id(1)
        halo = jnp.where(i == 0, 0.0, prev_ref[...])
        xp = jnp.concatenate([halo, cur_ref[...]], axis=0)
        pre = _conv_taps(xp, w_ref[...], 5, tm) + b_ref[...]
        sig = jax.nn.sigmoid(pre)
        dpre = da_ref[...] * (sig * (1.0 + pre * (1.0 - sig)))
        dp_ref[...] = dpre
        dw = jnp.concatenate([jnp.sum(dpre * xp[5 + k:5 + k + tm, :], axis=0, keepdims=True) for k in range(4)], axis=0)
        db = jnp.sum(dpre, axis=0, keepdims=True)

        @pl.when(i == 0)
        def _():
            dw_ref[...] = dw
            db_ref[...] = db

        @pl.when(i > 0)
        def _():
            dw_ref[...] += dw
            db_ref[...] += db

    return pl.pallas_call(
        body, grid=(SSD_CONV_DIM // tc, nt), in_specs=[prev, cur, par4, par1, out], out_specs=(out, par4, par1),
        out_shape=(jax.ShapeDtypeStruct((T, SSD_CONV_DIM), f32), jax.ShapeDtypeStruct((4, SSD_CONV_DIM), f32),
                   jax.ShapeDtypeStruct((1, SSD_CONV_DIM), f32)),
        name=name, compiler_params=_params(2))(proj, proj, w, b, dact)


def _conv_bwd_in(dpre, w, name):
    T = dpre.shape[0]
    tm, tc, nt, _, _, par4, _, out = _conv_blocks(T)
    nxt = pl.BlockSpec((8, tc), lambda j, i: (jnp.minimum((i + 1) * (tm // 8), T // 8 - 1), j))

    def body(cur_ref, nxt_ref, w_ref, o_ref):
        halo = jnp.where(pl.program_id(1) == nt - 1, 0.0, nxt_ref[...])
        xp = jnp.concatenate([cur_ref[...], halo], axis=0)
        w = w_ref[...]
        acc = w[3:4, :] * xp[0:tm, :]
        for k in range(3):
            acc = acc + w[k:k + 1, :] * xp[3 - k:3 - k + tm, :]
        o_ref[...] = acc.astype(o_ref.dtype)

    return pl.pallas_call(body, grid=(SSD_CONV_DIM // tc, nt), in_specs=[out, nxt, par4], out_specs=out,
                          out_shape=jax.ShapeDtypeStruct((T, SSD_CONV_DIM), bf16), name=name,
                          compiler_params=_params(2))(dpre, dpre, w)


def _ssd_chunk(hbase, xs, bm, cm, z, dtr, dtb, alog, dsk, ng, ht):
    L = SSD_L
    dt = jax.nn.softplus(dtr + dtb)
    da = dt * (-jnp.exp(alog))
    li, si = _iota((L, L), 0), _iota((L, L), 1)
    causal = li >= si
    cs = jnp.dot(causal.astype(f32), da, precision=HIGHEST, preferred_element_type=f32)
    cs_t = cs.T
    chan_head = _iota((1, SSD_GW), 1) // SSD_P
    heads = range(SSD_GW // SSD_P)
    lane_of = [(_iota((1, LANES), 1) == hbase + r).astype(f32) for r in heads]
    cs_cols = [jnp.sum(cs * lane_of[r], axis=1, keepdims=True) for r in heads]
    dt_cols = [jnp.sum(dt * lane_of[r], axis=1, keepdims=True) for r in heads]
    cs_e = jnp.zeros((L, SSD_GW), f32)
    dt_e = jnp.zeros((L, SSD_GW), f32)
    for r in heads:
        cs_e = jnp.where(chan_head == r, cs_cols[r], cs_e)
        dt_e = jnp.where(chan_head == r, dt_cols[r], dt_e)
    xdt = xs * dt_e
    cb = bdot_nt(cm, bm)
    y = jnp.zeros((L, SSD_GW), f32)
    for r in heads:
        cs_row = jnp.sum(cs_t * (_iota((LANES, 1), 0) == hbase + r).astype(f32), axis=0, keepdims=True)
        decay = jnp.where(causal, jnp.exp(jnp.where(causal, cs_cols[r] - cs_row, 0.0)), 0.0)
        y = y + bdot_nn(cb * decay, xdt * (chan_head == r).astype(f32))
    y = y + jnp.exp(cs_e) * bdot_nn(cm, ht)
    cs_last = jnp.sum(cs_e * (_iota((L, 1), 0) == L - 1).astype(f32), axis=0, keepdims=True)
    ht_new = ht * jnp.exp(cs_last) + bdot_tn(bm, xdt * jnp.exp(cs_last - cs_e))
    y = (y + dsk * xs) * jax.nn.silu(z)
    return _rms(y, ng), ht_new


def _ssd_specs(T, rev):
    nc = T // SSD_L
    cidx = (lambda c: nc - 1 - c) if rev else (lambda c: c)
    return nc, dict(
        xs=pl.BlockSpec((SSD_L, SSD_GW), lambda g, c: (cidx(c), g)),
        bm=pl.BlockSpec((SSD_L, SSD_N), lambda g, c: (cidx(c), 12 + g)),
        cm=pl.BlockSpec((SSD_L, SSD_N), lambda g, c: (cidx(c), 16 + g)),
        z=pl.BlockSpec((SSD_L, SSD_GW), lambda g, c: (cidx(c), g)),
        dt=pl.BlockSpec((SSD_L, LANES), lambda g, c: (cidx(c), 36)),
        p128=pl.BlockSpec((1, LANES), lambda g, c: (0, 0)),
        pgw=pl.BlockSpec((1, SSD_GW), lambda g, c: (0, g)),
        hs=pl.BlockSpec((None, None, SSD_N, SSD_GW), lambda g, c: (g, cidx(c), 0, 0)),
        grp=pl.BlockSpec((SSD_L, SSD_N), lambda g, c: (cidx(c), g)),
    )


def _ssd_fwd(xbc, proj, dtb, alog, dsk, ng, name):
    T = proj.shape[0]
    nc, s = _ssd_specs(T, False)

    def body(xs_ref, bm_ref, cm_ref, z_ref, dt_ref, dtb_ref, alog_ref, dsk_ref, ng_ref, y_ref, hs_ref, h_scr):
        @pl.when(pl.program_id(1) == 0)
        def _():
            h_scr[...] = jnp.zeros_like(h_scr)

        ht = h_scr[...]
        hs_ref[...] = ht
        y, ht_new = _ssd_chunk(pl.program_id(0) * (SSD_GW // SSD_P), xs_ref[...], bm_ref[...], cm_ref[...], z_ref[...],
                               dt_ref[...], dtb_ref[...], alog_ref[...], dsk_ref[...], ng_ref[...], ht)
        y_ref[...] = y.astype(y_ref.dtype)
        h_scr[...] = ht_new

    return pl.pallas_call(
        body, grid=(SSD_G, nc),
        in_specs=[s["xs"], s["bm"], s["cm"], s["z"], s["dt"], s["p128"], s["p128"], s["pgw"], s["pgw"]],
        out_specs=(s["xs"], s["hs"]),
        out_shape=(jax.ShapeDtypeStruct((T, SSD_INNER), bf16), jax.ShapeDtypeStruct((SSD_G, nc, SSD_N, SSD_GW), f32)),
        scratch_shapes=[pltpu.VMEM((SSD_N, SSD_GW), f32)], name=name, compiler_params=_params(2))(
            xbc, xbc, xbc, proj, proj, dtb, alog, dsk, ng)


def _ssd_bwd(xbc, proj, dtb, alog, dsk, ng, hs, dycat, name):
    T = proj.shape[0]
    nc, s = _ssd_specs(T, True)

    def body(xs_ref, bm_ref, cm_ref, z_ref, dt_ref, dtb_ref, alog_ref, dsk_ref, ng_ref, hs_ref, dy_ref,
             dxs_ref, dbm_ref, dcm_ref, dz_ref, ddt_ref, ddtb_ref, dalog_ref, ddsk_ref, dng_ref, dh_scr):
        c = pl.program_id(1)

        @pl.when(c == 0)
        def _():
            dh_scr[...] = jnp.zeros_like(dh_scr)

        hbase = pl.program_id(0) * (SSD_GW // SSD_P)
        _, vjp = jax.vjp(functools.partial(_ssd_chunk, hbase), xs_ref[...], bm_ref[...], cm_ref[...], z_ref[...],
                         dt_ref[...], dtb_ref[...], alog_ref[...], dsk_ref[...], ng_ref[...], hs_ref[...])
        dxs, dbm, dcm, dz, ddt, ddtb, dalog, ddsk, dng, dht = vjp((dy_ref[...].astype(f32), dh_scr[...]))
        dxs_ref[...] = dxs
        dbm_ref[...] = dbm
        dcm_ref[...] = dcm
        dz_ref[...] = dz.astype(dz_ref.dtype)
        ddt_ref[...] = ddt
        dh_scr[...] = dht

        @pl.when(c == 0)
        def _():
            ddtb_ref[...] = ddtb
            dalog_ref[...] = dalog
            ddsk_ref[...] = ddsk
            dng_ref[...] = dng

        @pl.when(c > 0)
        def _():
            ddtb_ref[...] += ddtb
            dalog_ref[...] += dalog
            ddsk_ref[...] += ddsk
            dng_ref[...] += dng

    cidx = lambda c: nc - 1 - c
    g128 = pl.BlockSpec((None, 1, LANES), lambda g, c: (g, 0, 0))
    return pl.pallas_call(
        body, grid=(SSD_G, nc),
        in_specs=[s["xs"], s["bm"], s["cm"], s["z"], s["dt"], s["p128"], s["p128"], s["pgw"], s["pgw"], s["hs"], s["xs"]],
        out_specs=(s["xs"], s["grp"], s["grp"], s["xs"],
                   pl.BlockSpec((None, SSD_L, LANES), lambda g, c: (g, cidx(c), 0)), g128, g128, s["pgw"], s["pgw"]),
        out_shape=(jax.ShapeDtypeStruct((T, SSD_INNER), f32), jax.ShapeDtypeStruct((T, SSD_G * SSD_N), f32),
                   jax.ShapeDtypeStruct((T, SSD_G * SSD_N), f32), jax.ShapeDtypeStruct((T, SSD_INNER), bf16),
                   jax.ShapeDtypeStruct((SSD_G, T, LANES), f32), jax.ShapeDtypeStruct((SSD_G, 1, LANES), f32),
                   jax.ShapeDtypeStruct((SSD_G, 1, LANES), f32), jax.ShapeDtypeStruct((1, SSD_INNER), f32),
                   jax.ShapeDtypeStruct((1, SSD_INNER), f32)),
        scratch_shapes=[pltpu.VMEM((SSD_N, SSD_GW), f32)], name=name, compiler_params=_params(2))(
            xbc, xbc, xbc, proj, proj, dtb, alog, dsk, ng, hs, dycat)


def _qk_norm_fn(q, k, qg, kg):
    return _rms(q, qg), _rms(k, kg)


def _sb_qknorm_fwd(proj, qg, kg, name):
    T = proj.shape[0]
    tm = _row_tile(T)
    par = pl.BlockSpec((1, HD), lambda h, i: (0, 0))
    out = pl.BlockSpec((tm, HD), lambda h, i: (i, h))
    sds = jax.ShapeDtypeStruct((T, SB_W), bf16)
    return _fn_call(_qk_norm_fn, (proj, proj, qg, kg),
                    [out, pl.BlockSpec((tm, HD), lambda h, i: (i, SB_HEADS + h)), par, par],
                    (sds, sds), (out, out), (SB_HEADS, T // tm), name)


def _sb_qknorm_bwd(proj, qg, kg, dqn, dkn, name):
    T = proj.shape[0]
    tm = _row_tile(T)
    par = pl.BlockSpec((1, HD), lambda h, i: (0, 0))
    out = pl.BlockSpec((tm, HD), lambda h, i: (i, h))
    gout = pl.BlockSpec((1, 1, HD), lambda h, i: (h, 0, 0))

    def fn(q, k, qg, kg, dqn, dkn):
        _, vjp = jax.vjp(_qk_norm_fn, q, k, qg, kg)
        dq, dk, dqg, dkg = vjp((dqn, dkn))
        return dq, dk, dqg[None], dkg[None]

    sds = jax.ShapeDtypeStruct((T, SB_W), bf16)
    gsds = jax.ShapeDtypeStruct((SB_HEADS, 1, HD), f32)
    return _fn_call(fn, (proj, proj, qg, kg, dqn, dkn),
                    [out, pl.BlockSpec((tm, HD), lambda h, i: (i, SB_HEADS + h)), par, par, out, out],
                    (sds, sds, gsds, gsds), (out, out, gout, gout), (SB_HEADS, T // tm), name, acc={2: "last", 3: "last"})


def _split_dot(a, tri):
    hi = a.astype(bf16)
    lo = (a - hi.astype(f32)).astype(bf16)
    return jnp.dot(hi, tri, preferred_element_type=f32) + jnp.dot(lo, tri, preferred_element_type=f32)


def _sb_weights(q, kblk, run, later, mask):
    z = _dg(q, kblk, 1, 1) * SB_SCALE
    t = jnp.log(1.0 + jnp.exp(-jnp.abs(z)))
    sp = jnp.maximum(z, 0.0) + t
    log_beta = jnp.minimum(z, 0.0) - t
    if mask is not None:
        sp = jnp.where(mask, sp, 0.0)
    w = jnp.exp(log_beta - _split_dot(sp, later) - run)
    if mask is not None:
        w = jnp.where(mask, w, 0.0)
    return jnp.exp(log_beta), sp, w


def _sb_older_blocks(qb, carry, step, run_of):
    def cond(state):
        i, cr = state
        return jnp.logical_and(i < qb, jnp.min(run_of(cr)) < SB_DEAD)

    def body(state):
        i, cr = state
        return i + 1, step(qb - 1 - i, cr)

    return lax.while_loop(cond, body, (jnp.int32(0), carry))[1]


def _sb_fwd(qn, kn, proj, name):
    T = qn.shape[0]
    B = min(SB_BLK, T)
    nq = T // B

    def body(q_ref, k_ref, v_ref, o_ref, ox_ref):
        qb = pl.program_id(1)
        q = q_ref[...]
        ri, ci = _iota((B, B), 0), _iota((B, B), 1)
        later = (ri > ci).astype(bf16)

        def block(kb, carry, mask):
            acc, acc_lo, run = carry
            off = pl.multiple_of(kb * B, B)
            _, sp, w = _sb_weights(q, k_ref[pl.ds(off, B), :], run, later, mask)
            vblk = v_ref[pl.ds(off, B), :]
            w_hi = w.astype(bf16)
            acc = acc + _dg(w_hi, vblk, 1, 0)
            acc_lo = acc_lo + _dg(w - w_hi.astype(f32), vblk, 1, 0)
            return acc, acc_lo, run + jnp.sum(sp, axis=1, keepdims=True)

        zero = jnp.zeros((B, HD), f32)
        carry = block(qb, (zero, zero, jnp.zeros((B, 1), f32)), ci < ri)
        carry = _sb_older_blocks(qb, carry, lambda kb, cr: block(kb, cr, None), lambda cr: cr[2])
        o_ref[...] = carry[0].astype(o_ref.dtype)
        ox_ref[...] = carry[0] + carry[1]

    blk = pl.BlockSpec((B, HD), lambda h, i: (i, h))
    return pl.pallas_call(
        body, grid=(SB_HEADS, nq),
        in_specs=[blk, pl.BlockSpec((T, HD), lambda h, i: (0, h)), pl.BlockSpec((T, HD), lambda h, i: (0, 2 * SB_HEADS + h))],
        out_specs=(blk, blk), out_shape=(jax.ShapeDtypeStruct((T, SB_W), bf16), jax.ShapeDtypeStruct((T, SB_W), f32)),
        name=name, compiler_params=_params(2))(qn, kn, proj)


def _sb_bwd(qn, kn, proj, o, dycat, name):
    T = qn.shape[0]
    B = min(SB_BLK, T)
    nq = T // B

    def body(q_ref, k_ref, v_ref, o_ref, do_ref, dq_ref, dk_ref, dv_ref):
        qb = pl.program_id(1)

        @pl.when(qb == 0)
        def _():
            dk_ref[...] = jnp.zeros_like(dk_ref)
            dv_ref[...] = jnp.zeros_like(dv_ref)

        q = q_ref[...]
        do = do_ref[...].astype(f32)
        do_b = do.astype(bf16)
        gtot = jnp.sum(do_b.astype(f32) * o_ref[...], axis=1, keepdims=True)
        ri, ci = _iota((B, B), 0), _iota((B, B), 1)
        later = (ri > ci).astype(bf16)
        from_here = (ri >= ci).astype(bf16)

        def block(kb, carry, mask):
            dq, run, rung = carry
            off = pl.multiple_of(kb * B, B)
            kblk = k_ref[pl.ds(off, B), :]
            sig, sp, w = _sb_weights(q, kblk, run, later, mask)
            g = w * _dg(do_b, v_ref[pl.ds(off, B), :], 1, 1)
            before = gtot - rung - _split_dot(g, from_here)
            dz = (g * (1.0 - sig) - sig * before) * SB_SCALE
            if mask is not None:
                dz = jnp.where(mask, dz, 0.0)
            dz_b = dz.astype(bf16)
            dv_ref[pl.ds(off, B), :] += _dg(w, do_b, 0, 0)
            dk_ref[pl.ds(off, B), :] += _dg(dz_b, q, 0, 0)
            dq = dq + _dg(dz_b, kblk, 1, 0)
            return dq, run + jnp.sum(sp, axis=1, keepdims=True), rung + jnp.sum(g, axis=1, keepdims=True)

        zero = jnp.zeros((B, 1), f32)
        carry = block(qb, (jnp.zeros((B, HD), f32), zero, zero), ci < ri)
        carry = _sb_older_blocks(qb, carry, lambda kb, cr: block(kb, cr, None), lambda cr: cr[1])
        dq_ref[...] = carry[0]

    blk = pl.BlockSpec((B, HD), lambda h, i: (i, h))
    full = pl.BlockSpec((T, HD), lambda h, i: (0, h))
    sds = jax.ShapeDtypeStruct((T, SB_W), f32)
    return pl.pallas_call(
        body, grid=(SB_HEADS, nq),
        in_specs=[blk, full, pl.BlockSpec((T, HD), lambda h, i: (0, 2 * SB_HEADS + h)), blk, blk],
        out_specs=(blk, full, full), out_shape=(sds, sds, sds), name=name, compiler_params=_params(2))(
            qn, kn, proj, o, dycat)


_BIG = (("mem_w_kv", (4, 128, 1024), 1), ("ssd_w_in", (2, 1024, 579), 2), ("ssd_w_out", (2, 256, 1024), 1),
        ("sb_w_in", (2, 1024, 640), 2), ("sb_w_out", (2, 256, 1024), 1), ("ffn_w_gate_up", (4, 1024, 704), 2),
        ("ffn_w_down", (4, 352, 1024), 1))
_BIG_ROWS = tuple(math.prod(s) // LANES for _, s, _ in _BIG)
_BIG_TOTAL = 66560
_ADAM_TILE = 1024


def _pack_rows(parts, total):
    rows = sum(p.shape[-2] for p in parts)
    pad = jnp.zeros(parts[0].shape[:-2] + (total - rows, LANES), parts[0].dtype)
    return jnp.concatenate(list(parts) + [pad], axis=-2)


def _full_from_slots(slots, shard_shape, axis):
    n = shard_shape[0]
    s = slots.reshape((N_DEV,) + shard_shape)
    if axis == 1:
        return s.transpose(1, 0, 2, 3).reshape(n, N_DEV * shard_shape[1], shard_shape[2])
    return s.transpose(1, 2, 0, 3).reshape(n, shard_shape[1], N_DEV * shard_shape[2])


def _slots_from_full(full, shard_shape, axis):
    n = shard_shape[0]
    if axis == 1:
        s = full.reshape(n, N_DEV, shard_shape[1], shard_shape[2]).transpose(1, 0, 2, 3)
    else:
        s = full.reshape(n, shard_shape[1], N_DEV, shard_shape[2]).transpose(2, 0, 1, 3)
    return s.reshape(N_DEV, -1, LANES)


def _ssd_in_cols(w):
    pad = jnp.zeros(w.shape[:-1] + (SSD_IN_PAD - SSD_IN,), w.dtype)
    return jnp.concatenate([w[..., :4096], w[..., 4120:4632], w[..., 4096:4120], pad], axis=-1)


def _ssd_in_cols_back(w):
    return jnp.concatenate([w[..., :4096], w[..., 4608:4632], w[..., 4096:4608]], axis=-1)


def _lane_rows(a):
    flat = a.reshape(-1)
    n = -(-flat.shape[0] // (8 * LANES)) * (8 * LANES)
    return jnp.pad(flat, (0, n - flat.shape[0])).reshape(-1, LANES)


def _pad128(a):
    return jnp.pad(a, ((0, 0), (0, LANES - a.shape[1])))


def kernel(x, mem, mix_norm_g, ffn_norm_g, mem_norm_g, mem_w_kv, mem_q_norm_g, mem_k_norm_g, ssd_w_in, ssd_conv_w, ssd_conv_b, ssd_dt_bias, ssd_a_log, ssd_d, ssd_norm_g, ssd_w_out, sb_w_in, sb_q_norm_g, sb_k_norm_g, sb_w_out, ffn_w_gate_up, ffn_w_down, loss_target, m_mix_norm_g, m_ffn_norm_g, m_mem_norm_g, m_mem_w_kv, m_mem_q_norm_g, m_mem_k_norm_g, m_ssd_w_in, m_ssd_conv_w, m_ssd_conv_b, m_ssd_dt_bias, m_ssd_a_log, m_ssd_d, m_ssd_norm_g, m_ssd_w_out, m_sb_w_in, m_sb_q_norm_g, m_sb_k_norm_g, m_sb_w_out, m_ffn_w_gate_up, m_ffn_w_down, v_mix_norm_g, v_ffn_norm_g, v_mem_norm_g, v_mem_w_kv, v_mem_q_norm_g, v_mem_k_norm_g, v_ssd_w_in, v_ssd_conv_w, v_ssd_conv_b, v_ssd_dt_bias, v_ssd_a_log, v_ssd_d, v_ssd_norm_g, v_ssd_w_out, v_sb_w_in, v_sb_q_norm_g, v_sb_k_norm_g, v_sb_w_out, v_ffn_w_gate_up, v_ffn_w_down):
    W = dict(mix_norm_g=mix_norm_g, ffn_norm_g=ffn_norm_g, mem_norm_g=mem_norm_g, mem_w_kv=mem_w_kv, mem_q_norm_g=mem_q_norm_g, mem_k_norm_g=mem_k_norm_g, ssd_w_in=ssd_w_in, ssd_conv_w=ssd_conv_w, ssd_conv_b=ssd_conv_b, ssd_dt_bias=ssd_dt_bias, ssd_a_log=ssd_a_log, ssd_d=ssd_d, ssd_norm_g=ssd_norm_g, ssd_w_out=ssd_w_out, sb_w_in=sb_w_in, sb_q_norm_g=sb_q_norm_g, sb_k_norm_g=sb_k_norm_g, sb_w_out=sb_w_out, ffn_w_gate_up=ffn_w_gate_up, ffn_w_down=ffn_w_down)
    M = dict(mix_norm_g=m_mix_norm_g, ffn_norm_g=m_ffn_norm_g, mem_norm_g=m_mem_norm_g, mem_w_kv=m_mem_w_kv, mem_q_norm_g=m_mem_q_norm_g, mem_k_norm_g=m_mem_k_norm_g, ssd_w_in=m_ssd_w_in, ssd_conv_w=m_ssd_conv_w, ssd_conv_b=m_ssd_conv_b, ssd_dt_bias=m_ssd_dt_bias, ssd_a_log=m_ssd_a_log, ssd_d=m_ssd_d, ssd_norm_g=m_ssd_norm_g, ssd_w_out=m_ssd_w_out, sb_w_in=m_sb_w_in, sb_q_norm_g=m_sb_q_norm_g, sb_k_norm_g=m_sb_k_norm_g, sb_w_out=m_sb_w_out, ffn_w_gate_up=m_ffn_w_gate_up, ffn_w_down=m_ffn_w_down)
    V = dict(mix_norm_g=v_mix_norm_g, ffn_norm_g=v_ffn_norm_g, mem_norm_g=v_mem_norm_g, mem_w_kv=v_mem_w_kv, mem_q_norm_g=v_mem_q_norm_g, mem_k_norm_g=v_mem_k_norm_g, ssd_w_in=v_ssd_w_in, ssd_conv_w=v_ssd_conv_w, ssd_conv_b=v_ssd_conv_b, ssd_dt_bias=v_ssd_dt_bias, ssd_a_log=v_ssd_a_log, ssd_d=v_ssd_d, ssd_norm_g=v_ssd_norm_g, ssd_w_out=v_ssd_w_out, sb_w_in=v_sb_w_in, sb_q_norm_g=v_sb_q_norm_g, sb_k_norm_g=v_sb_k_norm_g, sb_w_out=v_sb_w_out, ffn_w_gate_up=v_ffn_w_gate_up, ffn_w_down=v_ffn_w_down)
    names = list(W)
    T = x.shape[1]
    x0 = x.reshape(T, D)
    mem2 = mem.reshape(MEM_LEN, D)
    target = loss_target.reshape(T, D)
    my_dev = 4 * lax.axis_index("x") + 2 * lax.axis_index("y") + lax.axis_index("c")

    w_flat = _pack_rows([W[n].reshape(-1, LANES) for n, _, _ in _BIG], _BIG_TOTAL)
    slots = _allgather_hbm(w_flat.astype(bf16), "allgather_weights")
    full, off = {}, 0
    for (n, shp, ax), rows in zip(_BIG, _BIG_ROWS):
        full[n] = _full_from_slots(slots[:, off:off + rows], shp, ax)
        off += rows
    full["ssd_w_in"] = _ssd_in_cols(full["ssd_w_in"])
    conv_slots = _allgather_vmem(_lane_rows(ssd_conv_w), "allgather_conv_w")
    conv_w = _full_from_slots(conv_slots[:, :20], (2, 4, 320), 2)

    mem_g = mem_norm_g.reshape(1, D)

    saved = []
    xc = x0
    for i in range(DEPTH):
        j = i // 2
        ssd = i % 2 == 0
        L = f"l{i}_"
        mix_g = mix_norm_g[i:i + 1]
        h = _rmsnorm_fwd(xc, mix_g, L + "mix_norm")
        w_in = full["ssd_w_in"][j] if ssd else full["sb_w_in"][j]
        proj = _matmul(h, w_in, tm=256 if ssd else 1024, tn=w_in.shape[1] if ssd else 1024, tk=D, name=L + "in_proj")
        k_mem, v_mem = _memkv_fwd(mem2, mem_g, full["mem_w_kv"][i], mem_k_norm_g[i:i + 1], L + "mem_kv")
        q_col = 32 if ssd else 36
        o_mem = _memattn_fwd(proj, q_col, k_mem, v_mem, mem_q_norm_g[i:i + 1], L + "mem_attn")
        st = dict(x_in=xc, h=h, proj=proj, k_mem=k_mem, v_mem=v_mem)
        if ssd:
            xbc = _conv_fwd(proj, conv_w[j], ssd_conv_b[j:j + 1], L + "conv")
            dtb, alog = _pad128(ssd_dt_bias[j:j + 1]), _pad128(ssd_a_log[j:j + 1])
            dsk = jnp.repeat(ssd_d[j], SSD_P).reshape(1, SSD_INNER)
            y, hs = _ssd_fwd(xbc, proj, dtb, alog, dsk, ssd_norm_g[j:j + 1], L + "ssd_scan")
            st.update(xbc=xbc, hs=hs, dtb=dtb, alog=alog, dsk=dsk)
            w_out = full["ssd_w_out"][j]
        else:
            qn, kn = _sb_qknorm_fwd(proj, sb_q_norm_g[j:j + 1], sb_k_norm_g[j:j + 1], L + "qk_norm")
            y, o_exact = _sb_fwd(qn, kn, proj, L + "sb_attn")
            st.update(qn=qn, kn=kn, o=o_exact)
            w_out = full["sb_w_out"][j]
        ycat = jnp.concatenate([y, o_mem], axis=1)
        x_mid = _matmul(ycat, w_out, tm=512, tn=D, tk=2048, res=xc, name=L + "out_proj")
        h2 = _rmsnorm_fwd(x_mid, ffn_norm_g[i:i + 1], L + "ffn_norm")
        gu, act = _ffn_up_act(h2, full["ffn_w_gate_up"][i], L + "ffn_up")
        xc = _matmul(act, full["ffn_w_down"][i], tm=512, tn=D, tk=FFN_H, res=x_mid, name=L + "ffn_down")
        st.update(ycat=ycat, x_mid=x_mid, h2=h2, gu=gu, act=act, w_in=w_in, w_out=w_out)
        saved.append(st)

    dx, loss_part = _loss_head(xc, target, "loss_head")
    loss = lax.psum(jnp.sum(loss_part), ("x", "y", "c"))

    G = {n: [None] * W[n].shape[0] for n in names if W[n].ndim > 1}
    d_mem_g = jnp.zeros((1, D), f32)
    for i in reversed(range(DEPTH)):
        j = i // 2
        ssd = i % 2 == 0
        L = f"l{i}_b_"
        st = saved[i]
        proj = st["proj"]
        G["ffn_w_down"][i] = _matmul(st["act"], dx, ta=True, tm=FFN_H // 2, tn=D, tk=512, name=L + "dw_down")
        dgu = _ffn_dgu(dx, full["ffn_w_down"][i], st["gu"], L + "d_gu")
        dh2 = _matmul(dgu, full["ffn_w_gate_up"][i], tb=True, tm=512, tn=D, tk=2 * FFN_H, name=L + "d_h2")
        G["ffn_w_gate_up"][i] = _matmul(st["h2"], dgu, ta=True, tm=D, tn=FFN_H // 2, tk=512, name=L + "dw_up")
        dx, G["ffn_norm_g"][i] = _rmsnorm_bwd(st["x_mid"], ffn_norm_g[i:i + 1], dh2, dx, L + "d_ffn_norm")
        dycat = _matmul(dx, st["w_out"], tb=True, tm=512, tn=2048, tk=D, name=L + "d_ycat")
        g_out = _matmul(st["ycat"], dx, ta=True, tm=D, tn=D, tk=512, name=L + "dw_out")
        q_col = 32 if ssd else 36
        dq_mem, dk_mem, dv_mem, dqg = _memattn_bwd(proj, q_col, st["k_mem"], st["v_mem"], mem_q_norm_g[i:i + 1], dycat, 12, L + "d_mem_attn")
        G["mem_q_norm_g"][i] = jnp.sum(dqg, axis=0)
        dmg, G["mem_w_kv"][i], G["mem_k_norm_g"][i] = _memkv_bwd(mem2, mem_g, full["mem_w_kv"][i], mem_k_norm_g[i:i + 1], dk_mem, dv_mem, L + "d_mem_kv")
        d_mem_g = d_mem_g + dmg
        if ssd:
            G["ssd_w_out"][j] = g_out
            dxs, dbm, dcm, dz, ddt, ddtb, dalog, ddsk, dng = _ssd_bwd(
                st["xbc"], proj, st["dtb"], st["alog"], st["dsk"], ssd_norm_g[j:j + 1], st["hs"], dycat, L + "d_ssd_scan")
            G["ssd_dt_bias"][j] = jnp.sum(ddtb, axis=0)[:, :SSD_HEADS]
            G["ssd_a_log"][j] = jnp.sum(dalog, axis=0)[:, :SSD_HEADS]
            G["ssd_d"][j] = jnp.sum(ddsk.reshape(SSD_HEADS, SSD_P), axis=1).reshape(1, SSD_HEADS)
            G["ssd_norm_g"][j] = dng
            dxbc_act = jnp.concatenate([dxs, dbm, dcm], axis=1)
            dpre, G["ssd_conv_w"][j], G["ssd_conv_b"][j] = _conv_bwd_pre(proj, conv_w[j], ssd_conv_b[j:j + 1], dxbc_act, L + "d_conv_pre")
            dxbc = _conv_bwd_in(dpre, conv_w[j], L + "d_conv_in")
            ddt_all = jnp.sum(ddt, axis=0).astype(bf16)
            dproj = jnp.concatenate([dz, dxbc, dq_mem, ddt_all], axis=1)
        else:
            G["sb_w_out"][j] = g_out
            dqn, dkn, dv = _sb_bwd(st["qn"], st["kn"], proj, st["o"], dycat, L + "d_sb_attn")
            dq, dk, dqg2, dkg2 = _sb_qknorm_bwd(proj, sb_q_norm_g[j:j + 1], sb_k_norm_g[j:j + 1], dqn, dkn, L + "d_qk_norm")
            G["sb_q_norm_g"][j] = jnp.sum(dqg2, axis=0)
            G["sb_k_norm_g"][j] = jnp.sum(dkg2, axis=0)
            dproj = jnp.concatenate([dq, dk, dv.astype(bf16), dq_mem], axis=1)
        n_in = dproj.shape[1]
        dh = _matmul(dproj, st["w_in"], tb=True, tm=512, tn=D, tk=n_in, name=L + "d_h")
        g_in = _matmul(st["h"], dproj, ta=True, tm=256 if ssd else D, tn=n_in if ssd else 1024, tk=512, name=L + "dw_in")
        if ssd:
            G["ssd_w_in"][j] = _ssd_in_cols_back(g_in)
        else:
            G["sb_w_in"][j] = g_in
        dx, G["mix_norm_g"][i] = _rmsnorm_bwd(st["x_in"], mix_norm_g[i:i + 1], dh, dx, L + "d_mix_norm")

    grad_x = dx.reshape(x.shape)

    g_slots = _pack_rows([_slots_from_full(jnp.stack(G[n]), shp, ax).astype(bf16) for n, shp, ax in _BIG], _BIG_TOTAL)
    parts = _exchange_slices(g_slots, "exchange_grads")
    m_flat = _pack_rows([M[n].reshape(-1, LANES) for n, _, _ in _BIG], _BIG_TOTAL)
    v_flat = _pack_rows([V[n].reshape(-1, LANES) for n, _, _ in _BIG], _BIG_TOTAL)
    res_big = _adamw_reduce(parts, w_flat, m_flat, v_flat, _ADAM_TILE, "adamw_big")
    out = {}
    off = 0
    for (n, shp, _), rows in zip(_BIG, _BIG_ROWS):
        out[n] = tuple(r[off:off + rows].reshape(shp) for r in res_big)
        off += rows

    small = [n for n in names if n not in out and n != "ssd_conv_w"]
    G["mem_norm_g"] = d_mem_g.reshape(D)
    small_grads = [_lane_rows(G[n] if n == "mem_norm_g" else jnp.concatenate(G[n], axis=0)) for n in small]
    conv_grad = _lane_rows(jnp.stack(G["ssd_conv_w"]))
    sm_rows = [g.shape[0] for g in small_grads]
    n_small = sum(sm_rows)
    sm_total = -(-(n_small + conv_grad.shape[0]) // 8) * 8
    gathered = _allgather_vmem(_pack_rows(small_grads + [conv_grad], sm_total), "allgather_small_grads")
    whole = lambda s: pl.BlockSpec(s, lambda i: (0,) * len(s))
    g_sum = _fn_call(_sum_slots, (gathered,), [whole((N_DEV, sm_total, LANES))],
                     jax.ShapeDtypeStruct((sm_total, LANES), f32), whole((sm_total, LANES)), (1,), "sum_small_grads")
    conv_full = g_sum[n_small:n_small + 160].reshape(2, 4, SSD_CONV_DIM)
    conv_mine = lax.dynamic_slice_in_dim(conv_full, my_dev * 320, 320, axis=2)
    ad_total = n_small + 24
    pack = lambda d: _pack_rows([_lane_rows(d[n]) for n in small] + [_lane_rows(d["ssd_conv_w"])], ad_total)
    g_pack = _pack_rows([g_sum[:n_small], _lane_rows(conv_mine)], ad_total)
    blk = whole((ad_total, LANES))
    res_small = _fn_call(lambda g, w, m, v: _adamw_math(w, g, m, v), (g_pack, pack(W), pack(M), pack(V)), [blk] * 4,
                         (jax.ShapeDtypeStruct((ad_total, LANES), f32),) * 3, (blk,) * 3, (1,), "adamw_small")
    res_small = (g_pack,) + tuple(res_small)
    off = 0
    for n, rows in zip(small + ["ssd_conv_w"], sm_rows + [24]):
        size = W[n].size
        out[n] = tuple(r[off:off + rows].reshape(-1)[:size].reshape(W[n].shape) for r in res_small)
        off += rows

    return (loss, grad_x, *[out[n][0] for n in names], *[out[n][1] for n in names],
            *[out[n][2] for n in names], *[out[n][3] for n in names])
```

```python
import functools
import math

import jax
import jax.numpy as jnp
from jax import lax
from jax.experimental import pallas as pl
from jax.experimental.pallas import tpu as pltpu

f32, bf16 = jnp.float32, jnp.bfloat16
MESH = pl.DeviceIdType.MESH

N_DEV = 8
D = 1024
DEPTH = 4
EPS = 1e-6
MEM_LEN, MEM_HEADS, MEM_W, HD = 256, 4, 512, 128
SSD_INNER, SSD_HEADS, SSD_G, SSD_P, SSD_N, SSD_L = 1536, 24, 4, 64, 128, 128
SSD_GW = SSD_INNER // SSD_G
SSD_CONV_DIM = 2560
SSD_IN = 4632
SSD_IN_PAD = 4736
SB_W, SB_HEADS, SB_IN = 1536, 12, 5120
SB_BLK = 256
SB_SCALE = HD ** -0.5
SB_DEAD = 105.0
FFN_H = 2816
LANES = 128
VMEM_LIMIT = 56 * 1024 * 1024

ADAM_LR, ADAM_B1, ADAM_B2, ADAM_EPS, ADAM_WD, ADAM_STEP = 0.001, 0.9, 0.999, 1e-08, 0.01, 10

HIGHEST = lax.Precision.HIGHEST


def _params(n_grid):
    return pltpu.CompilerParams(dimension_semantics=("arbitrary",) * n_grid, vmem_limit_bytes=VMEM_LIMIT)


def _dg(a, b, ca, cb):
    return lax.dot_general(a.astype(bf16), b.astype(bf16), (((ca,), (cb,)), ((), ())), preferred_element_type=f32)


@jax.custom_vjp
def bdot_nn(a, b):
    return _dg(a, b, 1, 0)


def _nn_fwd(a, b):
    return _dg(a, b, 1, 0), (a, b)


def _nn_bwd(res, ct):
    a, b = res
    return _dg(ct, b, 1, 1).astype(a.dtype), _dg(a, ct, 0, 0).astype(b.dtype)


bdot_nn.defvjp(_nn_fwd, _nn_bwd)


@jax.custom_vjp
def bdot_nt(a, b):
    return _dg(a, b, 1, 1)


def _nt_fwd(a, b):
    return _dg(a, b, 1, 1), (a, b)


def _nt_bwd(res, ct):
    a, b = res
    return _dg(ct, b, 1, 0).astype(a.dtype), _dg(ct, a, 0, 0).astype(b.dtype)


bdot_nt.defvjp(_nt_fwd, _nt_bwd)


@jax.custom_vjp
def bdot_tn(a, b):
    return _dg(a, b, 0, 0)


def _tn_fwd(a, b):
    return _dg(a, b, 0, 0), (a, b)


def _tn_bwd(res, ct):
    a, b = res
    return _dg(b, ct, 1, 1).astype(a.dtype), _dg(a, ct, 1, 0).astype(b.dtype)


bdot_tn.defvjp(_tn_fwd, _tn_bwd)


def _rms(x, g):
    return x * lax.rsqrt(jnp.mean(x * x, axis=-1, keepdims=True) + EPS) * g


def _iota(shape, axis):
    return lax.broadcasted_iota(jnp.int32, shape, axis)


def _fn_call(fn, args, in_specs, out_shapes, out_specs, grid, name, acc=None):
    n_in = len(args)
    acc = acc or {}
    n_grid = len(grid)

    def body(*refs):
        ins, outs = refs[:n_in], refs[n_in:]
        res = fn(*[r[...] for r in ins])
        if not isinstance(res, (tuple, list)):
            res = (res,)
        for k, (o, r) in enumerate(zip(outs, res)):
            mode = acc.get(k)
            if mode is None:
                o[...] = r.astype(o.dtype)
                continue
            if mode == "last":
                first = pl.program_id(n_grid - 1) == 0
            else:
                first = functools.reduce(jnp.logical_and, [pl.program_id(d) == 0 for d in range(n_grid)])

            @pl.when(first)
            def _(o=o, r=r):
                o[...] = r.astype(o.dtype)

            @pl.when(jnp.logical_not(first))
            def _(o=o, r=r):
                o[...] += r.astype(o.dtype)

    return pl.pallas_call(
        body, grid=grid, in_specs=in_specs, out_specs=out_specs, out_shape=out_shapes, name=name,
        compiler_params=_params(n_grid))(*args)


def _matmul(a, b, *, ta=False, tb=False, out_dtype=f32, tm, tn, tk, res=None, name):
    M, K = (a.shape[1], a.shape[0]) if ta else a.shape
    N = b.shape[0] if tb else b.shape[1]
    tm, tn, tk = min(tm, M), min(tn, N), min(tk, K)
    assert M % tm == 0 and N % tn == 0 and K % tk == 0, (name, M, N, K, tm, tn, tk)
    nk = K // tk
    a_spec = pl.BlockSpec((tk, tm), lambda i, j, k: (k, i)) if ta else pl.BlockSpec((tm, tk), lambda i, j, k: (i, k))
    b_spec = pl.BlockSpec((tn, tk), lambda i, j, k: (j, k)) if tb else pl.BlockSpec((tk, tn), lambda i, j, k: (k, j))
    o_spec = pl.BlockSpec((tm, tn), lambda i, j, k: (i, j))
    ca, cb = (0 if ta else 1), (1 if tb else 0)

    def body(*refs):
        a_ref, b_ref = refs[:2]
        r_ref = None if res is None else refs[2]
        o_ref = refs[2 if res is None else 3]
        part = _dg(a_ref[...], b_ref[...], ca, cb)

        def finish(out):
            if r_ref is not None:
                out = out + r_ref[...].astype(f32)
            o_ref[...] = out.astype(o_ref.dtype)

        if nk == 1:
            finish(part)
            return
        acc_ref = refs[-1]
        k = pl.program_id(2)

        @pl.when(k == 0)
        def _():
            acc_ref[...] = part

        @pl.when(k > 0)
        def _():
            acc_ref[...] += part

        @pl.when(k == nk - 1)
        def _():
            finish(acc_ref[...])

    args = (a, b) if res is None else (a, b, res)
    in_specs = [a_spec, b_spec] + ([] if res is None else [o_spec])
    return pl.pallas_call(
        body, grid=(M // tm, N // tn, nk), in_specs=in_specs, out_specs=o_spec,
        out_shape=jax.ShapeDtypeStruct((M, N), out_dtype), name=name,
        scratch_shapes=[] if nk == 1 else [pltpu.VMEM((tm, tn), f32)], compiler_params=_params(3))(*args)


def _row_tile(T):
    return min(T, 512)


def _my_pos():
    return lax.axis_index("x"), lax.axis_index("y"), lax.axis_index("c")


def _allgather_hbm(xs, name):
    R, C = xs.shape

    def body(x_ref, out_ref, send_sems, recv_sems, local_sem):
        x, y, c = _my_pos()
        me, sibling = (x, y, c), (x, y, 1 - c)
        chips = [(1 - x, y), (x, 1 - y), (1 - x, 1 - y)]

        def slot(px, py, pc):
            return out_ref.at[4 * px + 2 * py + pc]

        def copy(k, block, to, src=None):
            return pltpu.make_async_remote_copy(
                src_ref=slot(*block) if src is None else src, dst_ref=slot(*block),
                send_sem=send_sems.at[k], recv_sem=recv_sems.at[k], device_id=to, device_id_type=MESH)

        mine = pltpu.make_async_copy(x_ref, slot(*me), local_sem)
        mine.start()
        first = [copy(0, me, sibling, src=x_ref)]
        first += [copy(1 + j, me, (*chip, c), src=x_ref) for j, chip in enumerate(chips)]
        for cp in first:
            cp.start()
        passed = [copy(4 + j, (*chip, c), sibling) for j, chip in enumerate(chips)]
        for j, chip in enumerate(chips):
            copy(1 + j, (*chip, c), me).wait_recv()
            passed[j].start()
        copy(0, sibling, me).wait_recv()
        for j, chip in enumerate(chips):
            copy(4 + j, (*chip, 1 - c), me).wait_recv()
        for cp in first + passed:
            cp.wait_send()
        mine.wait()

    return pl.pallas_call(
        body, out_shape=jax.ShapeDtypeStruct((N_DEV, R, C), xs.dtype),
        in_specs=[pl.BlockSpec(memory_space=pl.ANY)], out_specs=pl.BlockSpec(memory_space=pl.ANY),
        scratch_shapes=[pltpu.SemaphoreType.DMA((7,)), pltpu.SemaphoreType.DMA((7,)), pltpu.SemaphoreType.DMA],
        name=name)(xs)


def _allgather_vmem(xs, name):
    R, C = xs.shape

    def body(x_ref, out_ref, send_sems, recv_sems):
        x, y, c = _my_pos()
        me = 4 * x + 2 * y + c
        out_ref[me] = x_ref[...]
        copies = []
        for k in range(1, N_DEV):
            px = 1 - x if k & 4 else x
            py = 1 - y if k & 2 else y
            pc = 1 - c if k & 1 else c
            cp = pltpu.make_async_remote_copy(
                src_ref=x_ref, dst_ref=out_ref.at[me], send_sem=send_sems.at[k - 1], recv_sem=recv_sems.at[k - 1],
                device_id=(px, py, pc), device_id_type=MESH)
            cp.start()
            copies.append(cp)
        for cp in copies:
            cp.wait()

    return pl.pallas_call(
        body, out_shape=jax.ShapeDtypeStruct((N_DEV, R, C), xs.dtype),
        in_specs=[pl.BlockSpec(memory_space=pltpu.VMEM)], out_specs=pl.BlockSpec(memory_space=pltpu.VMEM),
        scratch_shapes=[pltpu.SemaphoreType.DMA((7,)), pltpu.SemaphoreType.DMA((7,))], name=name)(xs)


N_CHIP = 4


def _pair_exchange(gs, name):
    n = len(gs)

    def body(*refs):
        g_refs, out_refs = refs[:n], refs[n:2 * n]
        send_sems, recv_sems, local_sems = refs[2 * n:]
        x, y, c = _my_pos()
        copies = []
        for a in range(n):
            for k in range(N_CHIP):
                chip = 4 * (k >> 1) + 2 * (k & 1)
                cp = pltpu.make_async_copy(g_refs[a].at[chip + c], out_refs[a].at[0, k], local_sems.at[a, k])
                cp.start()
                copies.append(cp)
                cp = pltpu.make_async_remote_copy(
                    src_ref=g_refs[a].at[chip + 1 - c], dst_ref=out_refs[a].at[1, k],
                    send_sem=send_sems.at[a, k], recv_sem=recv_sems.at[a, k],
                    device_id=(x, y, 1 - c), device_id_type=MESH)
                cp.start()
                copies.append(cp)
        for cp in copies:
            cp.wait()

    hbm = pl.BlockSpec(memory_space=pl.ANY)
    return pl.pallas_call(
        body, out_shape=[jax.ShapeDtypeStruct((2, N_CHIP) + g.shape[1:], g.dtype) for g in gs],
        in_specs=[hbm] * n, out_specs=[hbm] * n,
        scratch_shapes=[pltpu.SemaphoreType.DMA((n, N_CHIP))] * 3, name=name)(*gs)


def _chip_exchange(cs, name):
    n = len(cs)

    def body(*refs):
        c_refs, out_refs = refs[:n], refs[n:2 * n]
        send_sems, recv_sems, local_sems = refs[2 * n:]
        x, y, c = _my_pos()
        my_chip = 2 * x + y
        copies = []
        for a in range(n):
            cp = pltpu.make_async_copy(c_refs[a].at[my_chip], out_refs[a].at[my_chip], local_sems.at[a])
            cp.start()
            copies.append(cp)
            for rel in range(1, N_CHIP):
                px = 1 - x if rel & 2 else x
                py = 1 - y if rel & 1 else y
                cp = pltpu.make_async_remote_copy(
                    src_ref=c_refs[a].at[2 * px + py], dst_ref=out_refs[a].at[my_chip],
                    send_sem=send_sems.at[a, rel - 1], recv_sem=recv_sems.at[a, rel - 1],
                    device_id=(px, py, c), device_id_type=MESH)
                cp.start()
                copies.append(cp)
        for cp in copies:
            cp.wait()

    hbm = pl.BlockSpec(memory_space=pl.ANY)
    return pl.pallas_call(
        body, out_shape=[jax.ShapeDtypeStruct(g.shape, g.dtype) for g in cs],
        in_specs=[hbm] * n, out_specs=[hbm] * n,
        scratch_shapes=[pltpu.SemaphoreType.DMA((n, N_CHIP - 1)), pltpu.SemaphoreType.DMA((n, N_CHIP - 1)),
                        pltpu.SemaphoreType.DMA((n,))], name=name)(*cs)


def _shard_row_tile(rows):
    return next(t for t in (256, 128) if rows % t == 0)


def _pair_sum(buf, name):
    _, _, rows, cols = buf.shape
    tr = _shard_row_tile(rows)
    return _fn_call(lambda b: b[0].astype(f32) + b[1].astype(f32), (buf,),
                    [pl.BlockSpec((2, N_CHIP, tr, cols), lambda i: (0, 0, i, 0))],
                    jax.ShapeDtypeStruct((N_CHIP, rows, cols), buf.dtype),
                    pl.BlockSpec((N_CHIP, tr, cols), lambda i: (0, i, 0)), (rows // tr,), name)


def _adamw_math(w, g, m, v):
    m = ADAM_B1 * m + (1.0 - ADAM_B1) * g
    v = ADAM_B2 * v + (1.0 - ADAM_B2) * jnp.square(g)
    m_hat = m / (1.0 - ADAM_B1 ** ADAM_STEP)
    v_hat = v / (1.0 - ADAM_B2 ** ADAM_STEP)
    delta = -ADAM_LR * (m_hat / (jnp.sqrt(v_hat) + ADAM_EPS) + ADAM_WD * w)
    return delta, m, v


def _sum_slots(parts):
    g = parts[0].astype(f32)
    for i in range(1, parts.shape[0]):
        g = g + parts[i].astype(f32)
    return g


def _adamw_reduce(parts, w, m, v, name):
    rows, cols = w.shape
    tr = _shard_row_tile(rows)

    def fn(p, w, m, v):
        g = _sum_slots(p)
        return (g,) + _adamw_math(w, g, m, v)

    row = pl.BlockSpec((tr, cols), lambda i: (i, 0))
    sds = jax.ShapeDtypeStruct((rows, cols), f32)
    return _fn_call(fn, (parts, w, m, v), [pl.BlockSpec((N_CHIP, tr, cols), lambda i: (0, i, 0)), row, row, row],
                    (sds,) * 4, (row,) * 4, (rows // tr,), name)


def _rmsnorm_fwd(x, g, name):
    T = x.shape[0]
    tm = _row_tile(T)
    row = pl.BlockSpec((tm, D), lambda i: (i, 0))
    par = pl.BlockSpec((1, D), lambda i: (0, 0))
    return _fn_call(lambda x, g: _rms(x, g), (x, g), [row, par], jax.ShapeDtypeStruct((T, D), bf16), row, (T // tm,), name)


def _rmsnorm_bwd(x, g, dh, dres, name):
    T = x.shape[0]
    tm = _row_tile(T)
    row = pl.BlockSpec((tm, D), lambda i: (i, 0))
    par = pl.BlockSpec((1, D), lambda i: (0, 0))

    def fn(x, g, dh, dres):
        _, vjp = jax.vjp(_rms, x, g)
        dx, dg = vjp(dh.astype(f32))
        return dx + dres, dg

    return _fn_call(fn, (x, g, dh, dres), [row, par, row, row],
                    (jax.ShapeDtypeStruct((T, D), f32), jax.ShapeDtypeStruct((1, D), f32)), (row, par),
                    (T // tm,), name, acc={1: "all"})


def _swiglu_act(g, u):
    return jax.nn.silu(g) * u


def _ffn_up_act(h2, w_gu, name):
    T = h2.shape[0]
    tm = min(T, 256)

    def body(h_ref, w_ref, gu_ref, act_ref):
        gu = _dg(h_ref[...], w_ref[...], 1, 0)
        gu_ref[...] = gu
        act_ref[...] = _swiglu_act(gu[:, :FFN_H], gu[:, FFN_H:]).astype(act_ref.dtype)

    return pl.pallas_call(
        body, grid=(T // tm,),
        in_specs=[pl.BlockSpec((tm, D), lambda i: (i, 0)), pl.BlockSpec((D, 2 * FFN_H), lambda i: (0, 0))],
        out_specs=(pl.BlockSpec((tm, 2 * FFN_H), lambda i: (i, 0)), pl.BlockSpec((tm, FFN_H), lambda i: (i, 0))),
        out_shape=(jax.ShapeDtypeStruct((T, 2 * FFN_H), f32), jax.ShapeDtypeStruct((T, FFN_H), bf16)),
        name=name, compiler_params=_params(1))(h2, w_gu)


def _ffn_dgu(dx, w_down, gu, name):
    T = dx.shape[0]
    tm = min(T, 256)

    def body(dx_ref, w_ref, gu_ref, o_ref):
        dact = _dg(dx_ref[...], w_ref[...], 1, 1)
        gu = gu_ref[...]
        _, vjp = jax.vjp(_swiglu_act, gu[:, :FFN_H], gu[:, FFN_H:])
        dg, du = vjp(dact)
        o_ref[:, :FFN_H] = dg.astype(o_ref.dtype)
        o_ref[:, FFN_H:] = du.astype(o_ref.dtype)

    return pl.pallas_call(
        body, grid=(T // tm,),
        in_specs=[pl.BlockSpec((tm, D), lambda i: (i, 0)), pl.BlockSpec((FFN_H, D), lambda i: (0, 0)),
                  pl.BlockSpec((tm, 2 * FFN_H), lambda i: (i, 0))],
        out_specs=pl.BlockSpec((tm, 2 * FFN_H), lambda i: (i, 0)),
        out_shape=jax.ShapeDtypeStruct((T, 2 * FFN_H), bf16), name=name, compiler_params=_params(1))(dx, w_down, gu)


def _loss_head(x, target, name):
    T = x.shape[0]
    tm = _row_tile(T)
    row = pl.BlockSpec((tm, D), lambda i: (i, 0))
    par = pl.BlockSpec((1, LANES), lambda i: (0, 0))

    def fn(x, t):
        e = x - t
        s = jnp.sum(e * e, axis=0, keepdims=True)
        part = s[:, 0:LANES]
        for k in range(1, D // LANES):
            part = part + s[:, k * LANES:(k + 1) * LANES]
        return e * (1.0 / D), part * (0.5 / D)

    return _fn_call(fn, (x, target), [row, row],
                    (jax.ShapeDtypeStruct((T, D), f32), jax.ShapeDtypeStruct((1, LANES), f32)), (row, par),
                    (T // tm,), name, acc={1: "all"})


def _memkv_fn(mem, mg, wkv, kg):
    mn = _rms(mem, mg)
    kv = bdot_nn(mn, wkv)
    ks = [_rms(kv[:, h * HD:(h + 1) * HD], kg) for h in range(MEM_HEADS)]
    return jnp.concatenate(ks, axis=1), kv[:, MEM_W:]


def _memkv_fwd(mem, mg, wkv, kg, name):
    whole = lambda s: pl.BlockSpec(s, lambda i: (0,) * len(s))
    sds = jax.ShapeDtypeStruct((MEM_LEN, MEM_W), f32)
    return _fn_call(lambda m, g, w, k: _memkv_fn(m, g, w.astype(f32), k), (mem, mg, wkv, kg),
                    [whole((MEM_LEN, D)), whole((1, D)), whole((D, 2 * MEM_W)), whole((1, HD))],
                    (sds, sds), (whole((MEM_LEN, MEM_W)),) * 2, (1,), name)


def _memkv_bwd(mem, mg, wkv, kg, dk, dv, name):
    whole = lambda s: pl.BlockSpec(s, lambda i: (0,) * len(s))

    def fn(m, g, w, k, dk, dv):
        _, vjp = jax.vjp(lambda g, w, k: _memkv_fn(m, g, w, k), g, w.astype(f32), k)
        return vjp((dk, dv))

    return _fn_call(fn, (mem, mg, wkv, kg, dk, dv),
                    [whole((MEM_LEN, D)), whole((1, D)), whole((D, 2 * MEM_W)), whole((1, HD)),
                     whole((MEM_LEN, MEM_W)), whole((MEM_LEN, MEM_W))],
                    (jax.ShapeDtypeStruct((1, D), f32), jax.ShapeDtypeStruct((D, 2 * MEM_W), f32),
                     jax.ShapeDtypeStruct((1, HD), f32)),
                    (whole((1, D)), whole((D, 2 * MEM_W)), whole((1, HD))), (1,), name)


def _memattn_fn(q, k, v, qg):
    qn = _rms(q, qg)
    s = bdot_nt(qn, k) * (HD ** -0.5)
    s = s - jnp.max(s, axis=-1, keepdims=True)
    p = jnp.exp(s)
    p = p / jnp.sum(p, axis=-1, keepdims=True)
    return bdot_nn(p, v)


def _memattn_fwd(proj, q_col, k, v, qg, name):
    T = proj.shape[0]
    tm = _row_tile(T)
    return _fn_call(_memattn_fn, (proj, k, v, qg),
                    [pl.BlockSpec((tm, HD), lambda h, i: (i, q_col + h)), pl.BlockSpec((MEM_LEN, HD), lambda h, i: (0, h)),
                     pl.BlockSpec((MEM_LEN, HD), lambda h, i: (0, h)), pl.BlockSpec((1, HD), lambda h, i: (0, 0))],
                    jax.ShapeDtypeStruct((T, MEM_W), bf16), pl.BlockSpec((tm, HD), lambda h, i: (i, h)),
                    (MEM_HEADS, T // tm), name)


def _memattn_bwd(proj, q_col, k, v, qg, dycat, do_col, name):
    T = proj.shape[0]
    tm = _row_tile(T)

    def fn(q, k, v, qg, do):
        _, vjp = jax.vjp(_memattn_fn, q, k, v, qg)
        dq, dk, dv, dg = vjp(do.astype(f32))
        return dq, dk, dv, dg[None]

    kv_spec = pl.BlockSpec((MEM_LEN, HD), lambda h, i: (0, h))
    kv_sds = jax.ShapeDtypeStruct((MEM_LEN, MEM_W), f32)
    return _fn_call(fn, (proj, k, v, qg, dycat),
                    [pl.BlockSpec((tm, HD), lambda h, i: (i, q_col + h)), kv_spec, kv_spec,
                     pl.BlockSpec((1, HD), lambda h, i: (0, 0)), pl.BlockSpec((tm, HD), lambda h, i: (i, do_col + h))],
                    (jax.ShapeDtypeStruct((T, MEM_W), bf16), kv_sds, kv_sds, jax.ShapeDtypeStruct((MEM_HEADS, 1, HD), f32)),
                    (pl.BlockSpec((tm, HD), lambda h, i: (i, h)), kv_spec, kv_spec,
                     pl.BlockSpec((1, 1, HD), lambda h, i: (h, 0, 0))),
                    (MEM_HEADS, T // tm), name, acc={1: "last", 2: "last", 3: "last"})


def _conv_taps(xp, w, first, tm):
    out = w[0:1, :] * xp[first:first + tm, :]
    for k in range(1, 4):
        out = out + w[k:k + 1, :] * xp[first + k:first + k + tm, :]
    return out


def _conv_blocks(T):
    tm, tc = _row_tile(T), 512
    nt = T // tm
    cur = pl.BlockSpec((tm, tc), lambda j, i: (i, 3 + j))
    prev = pl.BlockSpec((8, tc), lambda j, i: (jnp.maximum(i * (tm // 8) - 1, 0), 3 + j))
    par4 = pl.BlockSpec((4, tc), lambda j, i: (0, j))
    par1 = pl.BlockSpec((1, tc), lambda j, i: (0, j))
    out = pl.BlockSpec((tm, tc), lambda j, i: (i, j))
    return tm, tc, nt, cur, prev, par4, par1, out


def _conv_fwd(proj, w, b, name):
    T = proj.shape[0]
    tm, tc, nt, cur, prev, par4, par1, out = _conv_blocks(T)

    def body(prev_ref, cur_ref, w_ref, b_ref, o_ref):
        halo = jnp.where(pl.program_id(1) == 0, 0.0, prev_ref[...])
        xp = jnp.concatenate([halo, cur_ref[...]], axis=0)
        o_ref[...] = jax.nn.silu(_conv_taps(xp, w_ref[...], 5, tm) + b_ref[...])

    return pl.pallas_call(body, grid=(SSD_CONV_DIM // tc, nt), in_specs=[prev, cur, par4, par1], out_specs=out,
                          out_shape=jax.ShapeDtypeStruct((T, SSD_CONV_DIM), f32), name=name,
                          compiler_params=_params(2))(proj, proj, w, b)


def _conv_bwd_pre(proj, w, b, dact, name):
    T = proj.shape[0]
    tm, tc, nt, cur, prev, par4, par1, out = _conv_blocks(T)

    def body(prev_ref, cur_ref, w_ref, b_ref, da_ref, dp_ref, dw_ref, db_ref):
        i = pl.program_id(1)
        halo = jnp.where(i == 0, 0.0, prev_ref[...])
        xp = jnp.concatenate([halo, cur_ref[...]], axis=0)
        pre = _conv_taps(xp, w_ref[...], 5, tm) + b_ref[...]
        sig = jax.nn.sigmoid(pre)
        dpre = da_ref[...] * (sig * (1.0 + pre * (1.0 - sig)))
        dp_ref[...] = dpre
        dw = jnp.concatenate([jnp.sum(dpre * xp[5 + k:5 + k + tm, :], axis=0, keepdims=True) for k in range(4)], axis=0)
        db = jnp.sum(dpre, axis=0, keepdims=True)

        @pl.when(i == 0)
        def _():
            dw_ref[...] = dw
            db_ref[...] = db

        @pl.when(i > 0)
        def _():
            dw_ref[...] += dw
            db_ref[...] += db

    return pl.pallas_call(
        body, grid=(SSD_CONV_DIM // tc, nt), in_specs=[prev, cur, par4, par1, out], out_specs=(out, par4, par1),
        out_shape=(jax.ShapeDtypeStruct((T, SSD_CONV_DIM), f32), jax.ShapeDtypeStruct((4, SSD_CONV_DIM), f32),
                   jax.ShapeDtypeStruct((1, SSD_CONV_DIM), f32)),
        name=name, compiler_params=_params(2))(proj, proj, w, b, dact)


def _conv_bwd_in(dpre, w, name):
    T = dpre.shape[0]
    tm, tc, nt, _, _, par4, _, out = _conv_blocks(T)
    nxt = pl.BlockSpec((8, tc), lambda j, i: (jnp.minimum((i + 1) * (tm // 8), T // 8 - 1), j))

    def body(cur_ref, nxt_ref, w_ref, o_ref):
        halo = jnp.where(pl.program_id(1) == nt - 1, 0.0, nxt_ref[...])
        xp = jnp.concatenate([cur_ref[...], halo], axis=0)
        w = w_ref[...]
        acc = w[3:4, :] * xp[0:tm, :]
        for k in range(3):
            acc = acc + w[k:k + 1, :] * xp[3 - k:3 - k + tm, :]
        o_ref[...] = acc.astype(o_ref.dtype)

    return pl.pallas_call(body, grid=(SSD_CONV_DIM // tc, nt), in_specs=[out, nxt, par4], out_specs=out,
                          out_shape=jax.ShapeDtypeStruct((T, SSD_CONV_DIM), bf16), name=name,
                          compiler_params=_params(2))(dpre, dpre, w)


def _ssd_chunk(hbase, xs, bm, cm, z, dtr, dtb, alog, dsk, ng, ht):
    L = SSD_L
    dt = jax.nn.softplus(dtr + dtb)
    da = dt * (-jnp.exp(alog))
    li, si = _iota((L, L), 0), _iota((L, L), 1)
    causal = li >= si
    cs = jnp.dot(causal.astype(f32), da, precision=HIGHEST, preferred_element_type=f32)
    cs_t = cs.T
    chan_head = _iota((1, SSD_GW), 1) // SSD_P
    heads = range(SSD_GW // SSD_P)
    lane_of = [(_iota((1, LANES), 1) == hbase + r).astype(f32) for r in heads]
    cs_cols = [jnp.sum(cs * lane_of[r], axis=1, keepdims=True) for r in heads]
    dt_cols = [jnp.sum(dt * lane_of[r], axis=1, keepdims=True) for r in heads]
    cs_e = jnp.zeros((L, SSD_GW), f32)
    dt_e = jnp.zeros((L, SSD_GW), f32)
    for r in heads:
        cs_e = jnp.where(chan_head == r, cs_cols[r], cs_e)
        dt_e = jnp.where(chan_head == r, dt_cols[r], dt_e)
    xdt = xs * dt_e
    cb = bdot_nt(cm, bm)
    y = jnp.zeros((L, SSD_GW), f32)
    for r in heads:
        cs_row = jnp.sum(cs_t * (_iota((LANES, 1), 0) == hbase + r).astype(f32), axis=0, keepdims=True)
        decay = jnp.where(causal, jnp.exp(jnp.where(causal, cs_cols[r] - cs_row, 0.0)), 0.0)
        y = y + bdot_nn(cb * decay, xdt * (chan_head == r).astype(f32))
    y = y + jnp.exp(cs_e) * bdot_nn(cm, ht)
    cs_last = jnp.sum(cs_e * (_iota((L, 1), 0) == L - 1).astype(f32), axis=0, keepdims=True)
    ht_new = ht * jnp.exp(cs_last) + bdot_tn(bm, xdt * jnp.exp(cs_last - cs_e))
    y = (y + dsk * xs) * jax.nn.silu(z)
    return _rms(y, ng), ht_new


def _ssd_specs(T, rev):
    nc = T // SSD_L
    cidx = (lambda c: nc - 1 - c) if rev else (lambda c: c)
    return nc, dict(
        xs=pl.BlockSpec((SSD_L, SSD_GW), lambda g, c: (cidx(c), g)),
        bm=pl.BlockSpec((SSD_L, SSD_N), lambda g, c: (cidx(c), 12 + g)),
        cm=pl.BlockSpec((SSD_L, SSD_N), lambda g, c: (cidx(c), 16 + g)),
        z=pl.BlockSpec((SSD_L, SSD_GW), lambda g, c: (cidx(c), g)),
        dt=pl.BlockSpec((SSD_L, LANES), lambda g, c: (cidx(c), 36)),
        p128=pl.BlockSpec((1, LANES), lambda g, c: (0, 0)),
        pgw=pl.BlockSpec((1, SSD_GW), lambda g, c: (0, g)),
        hs=pl.BlockSpec((None, None, SSD_N, SSD_GW), lambda g, c: (g, cidx(c), 0, 0)),
        grp=pl.BlockSpec((SSD_L, SSD_N), lambda g, c: (cidx(c), g)),
    )


def _ssd_fwd(xbc, proj, dtb, alog, dsk, ng, name):
    T = proj.shape[0]
    nc, s = _ssd_specs(T, False)

    def body(xs_ref, bm_ref, cm_ref, z_ref, dt_ref, dtb_ref, alog_ref, dsk_ref, ng_ref, y_ref, hs_ref, h_scr):
        @pl.when(pl.program_id(1) == 0)
        def _():
            h_scr[...] = jnp.zeros_like(h_scr)

        ht = h_scr[...]
        hs_ref[...] = ht
        y, ht_new = _ssd_chunk(pl.program_id(0) * (SSD_GW // SSD_P), xs_ref[...], bm_ref[...], cm_ref[...], z_ref[...],
                               dt_ref[...], dtb_ref[...], alog_ref[...], dsk_ref[...], ng_ref[...], ht)
        y_ref[...] = y.astype(y_ref.dtype)
        h_scr[...] = ht_new

    return pl.pallas_call(
        body, grid=(SSD_G, nc),
        in_specs=[s["xs"], s["bm"], s["cm"], s["z"], s["dt"], s["p128"], s["p128"], s["pgw"], s["pgw"]],
        out_specs=(s["xs"], s["hs"]),
        out_shape=(jax.ShapeDtypeStruct((T, SSD_INNER), bf16), jax.ShapeDtypeStruct((SSD_G, nc, SSD_N, SSD_GW), f32)),
        scratch_shapes=[pltpu.VMEM((SSD_N, SSD_GW), f32)], name=name, compiler_params=_params(2))(
            xbc, xbc, xbc, proj, proj, dtb, alog, dsk, ng)


def _ssd_bwd(xbc, proj, dtb, alog, dsk, ng, hs, dycat, name):
    T = proj.shape[0]
    nc, s = _ssd_specs(T, True)

    def body(xs_ref, bm_ref, cm_ref, z_ref, dt_ref, dtb_ref, alog_ref, dsk_ref, ng_ref, hs_ref, dy_ref,
             dxs_ref, dbm_ref, dcm_ref, dz_ref, ddt_ref, ddtb_ref, dalog_ref, ddsk_ref, dng_ref, dh_scr):
        c = pl.program_id(1)

        @pl.when(c == 0)
        def _():
            dh_scr[...] = jnp.zeros_like(dh_scr)

        hbase = pl.program_id(0) * (SSD_GW // SSD_P)
        _, vjp = jax.vjp(functools.partial(_ssd_chunk, hbase), xs_ref[...], bm_ref[...], cm_ref[...], z_ref[...],
                         dt_ref[...], dtb_ref[...], alog_ref[...], dsk_ref[...], ng_ref[...], hs_ref[...])
        dxs, dbm, dcm, dz, ddt, ddtb, dalog, ddsk, dng, dht = vjp((dy_ref[...].astype(f32), dh_scr[...]))
        dxs_ref[...] = dxs
        dbm_ref[...] = dbm
        dcm_ref[...] = dcm
        dz_ref[...] = dz.astype(dz_ref.dtype)
        ddt_ref[...] = ddt
        dh_scr[...] = dht

        @pl.when(c == 0)
        def _():
            ddtb_ref[...] = ddtb
            dalog_ref[...] = dalog
            ddsk_ref[...] = ddsk
            dng_ref[...] = dng

        @pl.when(c > 0)
        def _():
            ddtb_ref[...] += ddtb
            dalog_ref[...] += dalog
            ddsk_ref[...] += ddsk
            dng_ref[...] += dng

    cidx = lambda c: nc - 1 - c
    g128 = pl.BlockSpec((None, 1, LANES), lambda g, c: (g, 0, 0))
    return pl.pallas_call(
        body, grid=(SSD_G, nc),
        in_specs=[s["xs"], s["bm"], s["cm"], s["z"], s["dt"], s["p128"], s["p128"], s["pgw"], s["pgw"], s["hs"], s["xs"]],
        out_specs=(s["xs"], s["grp"], s["grp"], s["xs"],
                   pl.BlockSpec((None, SSD_L, LANES), lambda g, c: (g, cidx(c), 0)), g128, g128, s["pgw"], s["pgw"]),
        out_shape=(jax.ShapeDtypeStruct((T, SSD_INNER), f32), jax.ShapeDtypeStruct((T, SSD_G * SSD_N), f32),
                   jax.ShapeDtypeStruct((T, SSD_G * SSD_N), f32), jax.ShapeDtypeStruct((T, SSD_INNER), bf16),
                   jax.ShapeDtypeStruct((SSD_G, T, LANES), f32), jax.ShapeDtypeStruct((SSD_G, 1, LANES), f32),
                   jax.ShapeDtypeStruct((SSD_G, 1, LANES), f32), jax.ShapeDtypeStruct((1, SSD_INNER), f32),
                   jax.ShapeDtypeStruct((1, SSD_INNER), f32)),
        scratch_shapes=[pltpu.VMEM((SSD_N, SSD_GW), f32)], name=name, compiler_params=_params(2))(
            xbc, xbc, xbc, proj, proj, dtb, alog, dsk, ng, hs, dycat)


def _qk_norm_fn(q, k, qg, kg):
    return _rms(q, qg), _rms(k, kg)


def _sb_qknorm_fwd(proj, qg, kg, name):
    T = proj.shape[0]
    tm = _row_tile(T)
    par = pl.BlockSpec((1, HD), lambda h, i: (0, 0))
    out = pl.BlockSpec((tm, HD), lambda h, i: (i, h))
    sds = jax.ShapeDtypeStruct((T, SB_W), bf16)
    return _fn_call(_qk_norm_fn, (proj, proj, qg, kg),
                    [out, pl.BlockSpec((tm, HD), lambda h, i: (i, SB_HEADS + h)), par, par],
                    (sds, sds), (out, out), (SB_HEADS, T // tm), name)


def _sb_qknorm_bwd(proj, qg, kg, dqn, dkn, name):
    T = proj.shape[0]
    tm = _row_tile(T)
    par = pl.BlockSpec((1, HD), lambda h, i: (0, 0))
    out = pl.BlockSpec((tm, HD), lambda h, i: (i, h))
    gout = pl.BlockSpec((1, 1, HD), lambda h, i: (h, 0, 0))

    def fn(q, k, qg, kg, dqn, dkn):
        _, vjp = jax.vjp(_qk_norm_fn, q, k, qg, kg)
        dq, dk, dqg, dkg = vjp((dqn, dkn))
        return dq, dk, dqg[None], dkg[None]

    sds = jax.ShapeDtypeStruct((T, SB_W), bf16)
    gsds = jax.ShapeDtypeStruct((SB_HEADS, 1, HD), f32)
    return _fn_call(fn, (proj, proj, qg, kg, dqn, dkn),
                    [out, pl.BlockSpec((tm, HD), lambda h, i: (i, SB_HEADS + h)), par, par, out, out],
                    (sds, sds, gsds, gsds), (out, out, gout, gout), (SB_HEADS, T // tm), name, acc={2: "last", 3: "last"})


def _split_dot(a, tri):
    hi = a.astype(bf16)
    lo = (a - hi.astype(f32)).astype(bf16)
    return jnp.dot(hi, tri, preferred_element_type=f32) + jnp.dot(lo, tri, preferred_element_type=f32)


def _sb_weights(q, kblk, run, later, mask):
    z = _dg(q, kblk, 1, 1) * SB_SCALE
    t = jnp.log(1.0 + jnp.exp(-jnp.abs(z)))
    sp = jnp.maximum(z, 0.0) + t
    log_beta = jnp.minimum(z, 0.0) - t
    if mask is not None:
        sp = jnp.where(mask, sp, 0.0)
    w = jnp.exp(log_beta - _split_dot(sp, later) - run)
    if mask is not None:
        w = jnp.where(mask, w, 0.0)
    return jnp.exp(log_beta), sp, w


def _sb_older_blocks(qb, carry, step, run_of):
    def cond(state):
        i, cr = state
        return jnp.logical_and(i < qb, jnp.min(run_of(cr)) < SB_DEAD)

    def body(state):
        i, cr = state
        return i + 1, step(qb - 1 - i, cr)

    return lax.while_loop(cond, body, (jnp.int32(0), carry))[1]


def _sb_fwd(qn, kn, proj, name):
    T = qn.shape[0]
    B = min(SB_BLK, T)
    nq = T // B

    def body(q_ref, k_ref, v_ref, o_ref, ox_ref):
        qb = pl.program_id(1)
        q = q_ref[...]
        ri, ci = _iota((B, B), 0), _iota((B, B), 1)
        later = (ri > ci).astype(bf16)

        def block(kb, carry, mask):
            acc, acc_lo, run = carry
            off = pl.multiple_of(kb * B, B)
            _, sp, w = _sb_weights(q, k_ref[pl.ds(off, B), :], run, later, mask)
            vblk = v_ref[pl.ds(off, B), :]
            w_hi = w.astype(bf16)
            acc = acc + _dg(w_hi, vblk, 1, 0)
            acc_lo = acc_lo + _dg(w - w_hi.astype(f32), vblk, 1, 0)
            return acc, acc_lo, run + jnp.sum(sp, axis=1, keepdims=True)

        zero = jnp.zeros((B, HD), f32)
        carry = block(qb, (zero, zero, jnp.zeros((B, 1), f32)), ci < ri)
        carry = _sb_older_blocks(qb, carry, lambda kb, cr: block(kb, cr, None), lambda cr: cr[2])
        o_ref[...] = carry[0].astype(o_ref.dtype)
        ox_ref[...] = carry[0] + carry[1]

    blk = pl.BlockSpec((B, HD), lambda h, i: (i, h))
    return pl.pallas_call(
        body, grid=(SB_HEADS, nq),
        in_specs=[blk, pl.BlockSpec((T, HD), lambda h, i: (0, h)), pl.BlockSpec((T, HD), lambda h, i: (0, 2 * SB_HEADS + h))],
        out_specs=(blk, blk), out_shape=(jax.ShapeDtypeStruct((T, SB_W), bf16), jax.ShapeDtypeStruct((T, SB_W), f32)),
        name=name, compiler_params=_params(2))(qn, kn, proj)


def _sb_bwd(qn, kn, proj, o, dycat, name):
    T = qn.shape[0]
    B = min(SB_BLK, T)
    nq = T // B

    def body(q_ref, k_ref, v_ref, o_ref, do_ref, dq_ref, dk_ref, dv_ref):
        qb = pl.program_id(1)

        @pl.when(qb == 0)
        def _():
            dk_ref[...] = jnp.zeros_like(dk_ref)
            dv_ref[...] = jnp.zeros_like(dv_ref)

        q = q_ref[...]
        do = do_ref[...].astype(f32)
        do_b = do.astype(bf16)
        gtot = jnp.sum(do_b.astype(f32) * o_ref[...], axis=1, keepdims=True)
        ri, ci = _iota((B, B), 0), _iota((B, B), 1)
        later = (ri > ci).astype(bf16)
        from_here = (ri >= ci).astype(bf16)

        def block(kb, carry, mask):
            dq, run, rung = carry
            off = pl.multiple_of(kb * B, B)
            kblk = k_ref[pl.ds(off, B), :]
            sig, sp, w = _sb_weights(q, kblk, run, later, mask)
            g = w * _dg(do_b, v_ref[pl.ds(off, B), :], 1, 1)
            before = gtot - rung - _split_dot(g, from_here)
            dz = (g * (1.0 - sig) - sig * before) * SB_SCALE
            if mask is not None:
                dz = jnp.where(mask, dz, 0.0)
            dz_b = dz.astype(bf16)
            dv_ref[pl.ds(off, B), :] += _dg(w, do_b, 0, 0)
            dk_ref[pl.ds(off, B), :] += _dg(dz_b, q, 0, 0)
            dq = dq + _dg(dz_b, kblk, 1, 0)
            return dq, run + jnp.sum(sp, axis=1, keepdims=True), rung + jnp.sum(g, axis=1, keepdims=True)

        zero = jnp.zeros((B, 1), f32)
        carry = block(qb, (jnp.zeros((B, HD), f32), zero, zero), ci < ri)
        carry = _sb_older_blocks(qb, carry, lambda kb, cr: block(kb, cr, None), lambda cr: cr[1])
        dq_ref[...] = carry[0]

    blk = pl.BlockSpec((B, HD), lambda h, i: (i, h))
    full = pl.BlockSpec((T, HD), lambda h, i: (0, h))
    sds = jax.ShapeDtypeStruct((T, SB_W), f32)
    return pl.pallas_call(
        body, grid=(SB_HEADS, nq),
        in_specs=[blk, full, pl.BlockSpec((T, HD), lambda h, i: (0, 2 * SB_HEADS + h)), blk, blk],
        out_specs=(blk, full, full), out_shape=(sds, sds, sds), name=name, compiler_params=_params(2))(
            qn, kn, proj, o, dycat)


_BIG = (("mem_w_kv", (4, 128, 1024), 1), ("ssd_w_in", (2, 1024, 579), 2), ("ssd_w_out", (2, 256, 1024), 1),
        ("sb_w_in", (2, 1024, 640), 2), ("sb_w_out", (2, 256, 1024), 1), ("ffn_w_gate_up", (4, 1024, 704), 2),
        ("ffn_w_down", (4, 352, 1024), 1))
_BIG_ROWS = tuple(math.prod(s) // LANES for _, s, _ in _BIG)
_BIG_TOTAL = sum(_BIG_ROWS)


def _pack_rows(parts, total):
    rows = sum(p.shape[-2] for p in parts)
    if rows == total:
        return jnp.concatenate(list(parts), axis=-2)
    pad = jnp.zeros(parts[0].shape[:-2] + (total - rows, LANES), parts[0].dtype)
    return jnp.concatenate(list(parts) + [pad], axis=-2)


def _full_from_slots(slots, shard_shape, axis):
    n = shard_shape[0]
    s = slots.reshape((N_DEV,) + shard_shape)
    if axis == 1:
        return s.transpose(1, 0, 2, 3).reshape(n, N_DEV * shard_shape[1], shard_shape[2])
    return s.transpose(1, 2, 0, 3).reshape(n, shard_shape[1], N_DEV * shard_shape[2])


def _slots_from_full(full, shard_shape, axis):
    n = shard_shape[0]
    if axis == 1:
        s = full.reshape(n, N_DEV, shard_shape[1], shard_shape[2]).transpose(1, 0, 2, 3)
    else:
        s = full.reshape(n, shard_shape[1], N_DEV, shard_shape[2]).transpose(2, 0, 1, 3)
    return s.reshape(N_DEV, n * shard_shape[1], shard_shape[2])


def _ssd_in_cols(w):
    pad = jnp.zeros(w.shape[:-1] + (SSD_IN_PAD - SSD_IN,), w.dtype)
    return jnp.concatenate([w[..., :4096], w[..., 4120:4632], w[..., 4096:4120], pad], axis=-1)


def _ssd_in_cols_back(w):
    return jnp.concatenate([w[..., :4096], w[..., 4608:4632], w[..., 4096:4608]], axis=-1)


def _lane_rows(a):
    flat = a.reshape(-1)
    n = -(-flat.shape[0] // (8 * LANES)) * (8 * LANES)
    return jnp.pad(flat, (0, n - flat.shape[0])).reshape(-1, LANES)


def _pad128(a):
    return jnp.pad(a, ((0, 0), (0, LANES - a.shape[1])))


def kernel(x, mem, mix_norm_g, ffn_norm_g, mem_norm_g, mem_w_kv, mem_q_norm_g, mem_k_norm_g, ssd_w_in, ssd_conv_w, ssd_conv_b, ssd_dt_bias, ssd_a_log, ssd_d, ssd_norm_g, ssd_w_out, sb_w_in, sb_q_norm_g, sb_k_norm_g, sb_w_out, ffn_w_gate_up, ffn_w_down, loss_target, m_mix_norm_g, m_ffn_norm_g, m_mem_norm_g, m_mem_w_kv, m_mem_q_norm_g, m_mem_k_norm_g, m_ssd_w_in, m_ssd_conv_w, m_ssd_conv_b, m_ssd_dt_bias, m_ssd_a_log, m_ssd_d, m_ssd_norm_g, m_ssd_w_out, m_sb_w_in, m_sb_q_norm_g, m_sb_k_norm_g, m_sb_w_out, m_ffn_w_gate_up, m_ffn_w_down, v_mix_norm_g, v_ffn_norm_g, v_mem_norm_g, v_mem_w_kv, v_mem_q_norm_g, v_mem_k_norm_g, v_ssd_w_in, v_ssd_conv_w, v_ssd_conv_b, v_ssd_dt_bias, v_ssd_a_log, v_ssd_d, v_ssd_norm_g, v_ssd_w_out, v_sb_w_in, v_sb_q_norm_g, v_sb_k_norm_g, v_sb_w_out, v_ffn_w_gate_up, v_ffn_w_down):
    W = dict(mix_norm_g=mix_norm_g, ffn_norm_g=ffn_norm_g, mem_norm_g=mem_norm_g, mem_w_kv=mem_w_kv, mem_q_norm_g=mem_q_norm_g, mem_k_norm_g=mem_k_norm_g, ssd_w_in=ssd_w_in, ssd_conv_w=ssd_conv_w, ssd_conv_b=ssd_conv_b, ssd_dt_bias=ssd_dt_bias, ssd_a_log=ssd_a_log, ssd_d=ssd_d, ssd_norm_g=ssd_norm_g, ssd_w_out=ssd_w_out, sb_w_in=sb_w_in, sb_q_norm_g=sb_q_norm_g, sb_k_norm_g=sb_k_norm_g, sb_w_out=sb_w_out, ffn_w_gate_up=ffn_w_gate_up, ffn_w_down=ffn_w_down)
    M = dict(mix_norm_g=m_mix_norm_g, ffn_norm_g=m_ffn_norm_g, mem_norm_g=m_mem_norm_g, mem_w_kv=m_mem_w_kv, mem_q_norm_g=m_mem_q_norm_g, mem_k_norm_g=m_mem_k_norm_g, ssd_w_in=m_ssd_w_in, ssd_conv_w=m_ssd_conv_w, ssd_conv_b=m_ssd_conv_b, ssd_dt_bias=m_ssd_dt_bias, ssd_a_log=m_ssd_a_log, ssd_d=m_ssd_d, ssd_norm_g=m_ssd_norm_g, ssd_w_out=m_ssd_w_out, sb_w_in=m_sb_w_in, sb_q_norm_g=m_sb_q_norm_g, sb_k_norm_g=m_sb_k_norm_g, sb_w_out=m_sb_w_out, ffn_w_gate_up=m_ffn_w_gate_up, ffn_w_down=m_ffn_w_down)
    V = dict(mix_norm_g=v_mix_norm_g, ffn_norm_g=v_ffn_norm_g, mem_norm_g=v_mem_norm_g, mem_w_kv=v_mem_w_kv, mem_q_norm_g=v_mem_q_norm_g, mem_k_norm_g=v_mem_k_norm_g, ssd_w_in=v_ssd_w_in, ssd_conv_w=v_ssd_conv_w, ssd_conv_b=v_ssd_conv_b, ssd_dt_bias=v_ssd_dt_bias, ssd_a_log=v_ssd_a_log, ssd_d=v_ssd_d, ssd_norm_g=v_ssd_norm_g, ssd_w_out=v_ssd_w_out, sb_w_in=v_sb_w_in, sb_q_norm_g=v_sb_q_norm_g, sb_k_norm_g=v_sb_k_norm_g, sb_w_out=v_sb_w_out, ffn_w_gate_up=v_ffn_w_gate_up, ffn_w_down=v_ffn_w_down)
    names = list(W)
    T = x.shape[1]
    x0 = x.reshape(T, D)
    mem2 = mem.reshape(MEM_LEN, D)
    target = loss_target.reshape(T, D)
    my_dev = 4 * lax.axis_index("x") + 2 * lax.axis_index("y") + lax.axis_index("c")

    w_flat = _pack_rows([W[n].astype(bf16).reshape(-1, LANES) for n, _, _ in _BIG], _BIG_TOTAL)
    slots = _allgather_hbm(w_flat, "allgather_weights")
    full, off = {}, 0
    for (n, shp, ax), rows in zip(_BIG, _BIG_ROWS):
        full[n] = _full_from_slots(slots[:, off:off + rows], shp, ax)
        off += rows
    full["ssd_w_in"] = _ssd_in_cols(full["ssd_w_in"])
    conv_slots = _allgather_vmem(_lane_rows(ssd_conv_w), "allgather_conv_w")
    conv_w = _full_from_slots(conv_slots[:, :20], (2, 4, 320), 2)

    mem_g = mem_norm_g.reshape(1, D)

    saved = []
    xc = x0
    for i in range(DEPTH):
        j = i // 2
        ssd = i % 2 == 0
        L = f"l{i}_"
        mix_g = mix_norm_g[i:i + 1]
        h = _rmsnorm_fwd(xc, mix_g, L + "mix_norm")
        w_in = full["ssd_w_in"][j] if ssd else full["sb_w_in"][j]
        proj = _matmul(h, w_in, tm=256 if ssd else 1024, tn=w_in.shape[1] if ssd else 1024, tk=D, name=L + "in_proj")
        k_mem, v_mem = _memkv_fwd(mem2, mem_g, full["mem_w_kv"][i], mem_k_norm_g[i:i + 1], L + "mem_kv")
        q_col = 32 if ssd else 36
        o_mem = _memattn_fwd(proj, q_col, k_mem, v_mem, mem_q_norm_g[i:i + 1], L + "mem_attn")
        st = dict(x_in=xc, h=h, proj=proj, k_mem=k_mem, v_mem=v_mem)
        if ssd:
            xbc = _conv_fwd(proj, conv_w[j], ssd_conv_b[j:j + 1], L + "conv")
            dtb, alog = _pad128(ssd_dt_bias[j:j + 1]), _pad128(ssd_a_log[j:j + 1])
            dsk = jnp.repeat(ssd_d[j], SSD_P).reshape(1, SSD_INNER)
            y, hs = _ssd_fwd(xbc, proj, dtb, alog, dsk, ssd_norm_g[j:j + 1], L + "ssd_scan")
            st.update(xbc=xbc, hs=hs, dtb=dtb, alog=alog, dsk=dsk)
            w_out = full["ssd_w_out"][j]
        else:
            qn, kn = _sb_qknorm_fwd(proj, sb_q_norm_g[j:j + 1], sb_k_norm_g[j:j + 1], L + "qk_norm")
            y, o_exact = _sb_fwd(qn, kn, proj, L + "sb_attn")
            st.update(qn=qn, kn=kn, o=o_exact)
            w_out = full["sb_w_out"][j]
        ycat = jnp.concatenate([y, o_mem], axis=1)
        x_mid = _matmul(ycat, w_out, tm=512, tn=D, tk=2048, res=xc, name=L + "out_proj")
        h2 = _rmsnorm_fwd(x_mid, ffn_norm_g[i:i + 1], L + "ffn_norm")
        gu, act = _ffn_up_act(h2, full["ffn_w_gate_up"][i], L + "ffn_up")
        xc = _matmul(act, full["ffn_w_down"][i], tm=512, tn=D, tk=FFN_H, res=x_mid, name=L + "ffn_down")
        st.update(ycat=ycat, x_mid=x_mid, h2=h2, gu=gu, act=act, w_in=w_in, w_out=w_out)
        saved.append(st)

    dx, loss_part = _loss_head(xc, target, "loss_head")
    loss = lax.psum(jnp.sum(loss_part), ("x", "y", "c"))

    G = {n: [None] * W[n].shape[0] for n in names if W[n].ndim > 1}
    d_mem_g = jnp.zeros((1, D), f32)
    for i in reversed(range(DEPTH)):
        j = i // 2
        ssd = i % 2 == 0
        L = f"l{i}_b_"
        st = saved[i]
        proj = st["proj"]
        G["ffn_w_down"][i] = _matmul(st["act"], dx, ta=True, tm=FFN_H // 2, tn=D, tk=512, name=L + "dw_down")
        dgu = _ffn_dgu(dx, full["ffn_w_down"][i], st["gu"], L + "d_gu")
        dh2 = _matmul(dgu, full["ffn_w_gate_up"][i], tb=True, tm=512, tn=D, tk=2 * FFN_H, name=L + "d_h2")
        G["ffn_w_gate_up"][i] = _matmul(st["h2"], dgu, ta=True, tm=D, tn=FFN_H // 2, tk=512, name=L + "dw_up")
        dx, G["ffn_norm_g"][i] = _rmsnorm_bwd(st["x_mid"], ffn_norm_g[i:i + 1], dh2, dx, L + "d_ffn_norm")
        dycat = _matmul(dx, st["w_out"], tb=True, tm=512, tn=2048, tk=D, name=L + "d_ycat")
        g_out = _matmul(st["ycat"], dx, ta=True, tm=D, tn=D, tk=512, name=L + "dw_out")
        q_col = 32 if ssd else 36
        dq_mem, dk_mem, dv_mem, dqg = _memattn_bwd(proj, q_col, st["k_mem"], st["v_mem"], mem_q_norm_g[i:i + 1], dycat, 12, L + "d_mem_attn")
        G["mem_q_norm_g"][i] = jnp.sum(dqg, axis=0)
        dmg, G["mem_w_kv"][i], G["mem_k_norm_g"][i] = _memkv_bwd(mem2, mem_g, full["mem_w_kv"][i], mem_k_norm_g[i:i + 1], dk_mem, dv_mem, L + "d_mem_kv")
        d_mem_g = d_mem_g + dmg
        if ssd:
            G["ssd_w_out"][j] = g_out
            dxs, dbm, dcm, dz, ddt, ddtb, dalog, ddsk, dng = _ssd_bwd(
                st["xbc"], proj, st["dtb"], st["alog"], st["dsk"], ssd_norm_g[j:j + 1], st["hs"], dycat, L + "d_ssd_scan")
            G["ssd_dt_bias"][j] = jnp.sum(ddtb, axis=0)[:, :SSD_HEADS]
            G["ssd_a_log"][j] = jnp.sum(dalog, axis=0)[:, :SSD_HEADS]
            G["ssd_d"][j] = jnp.sum(ddsk.reshape(SSD_HEADS, SSD_P), axis=1).reshape(1, SSD_HEADS)
            G["ssd_norm_g"][j] = dng
            dxbc_act = jnp.concatenate([dxs, dbm, dcm], axis=1)
            dpre, G["ssd_conv_w"][j], G["ssd_conv_b"][j] = _conv_bwd_pre(proj, conv_w[j], ssd_conv_b[j:j + 1], dxbc_act, L + "d_conv_pre")
            dxbc = _conv_bwd_in(dpre, conv_w[j], L + "d_conv_in")
            ddt_all = jnp.sum(ddt, axis=0).astype(bf16)
            dproj = jnp.concatenate([dz, dxbc, dq_mem, ddt_all], axis=1)
        else:
            G["sb_w_out"][j] = g_out
            dqn, dkn, dv = _sb_bwd(st["qn"], st["kn"], proj, st["o"], dycat, L + "d_sb_attn")
            dq, dk, dqg2, dkg2 = _sb_qknorm_bwd(proj, sb_q_norm_g[j:j + 1], sb_k_norm_g[j:j + 1], dqn, dkn, L + "d_qk_norm")
            G["sb_q_norm_g"][j] = jnp.sum(dqg2, axis=0)
            G["sb_k_norm_g"][j] = jnp.sum(dkg2, axis=0)
            dproj = jnp.concatenate([dq, dk, dv.astype(bf16), dq_mem], axis=1)
        n_in = dproj.shape[1]
        dh = _matmul(dproj, st["w_in"], tb=True, tm=512, tn=D, tk=n_in, name=L + "d_h")
        g_in = _matmul(st["h"], dproj, ta=True, tm=256 if ssd else D, tn=n_in if ssd else 1024, tk=512, name=L + "dw_in")
        if ssd:
            G["ssd_w_in"][j] = _ssd_in_cols_back(g_in)
        else:
            G["sb_w_in"][j] = g_in
        dx, G["mix_norm_g"][i] = _rmsnorm_bwd(st["x_in"], mix_norm_g[i:i + 1], dh, dx, L + "d_mix_norm")

    grad_x = dx.reshape(x.shape)

    g_slots = [_slots_from_full(jnp.stack(G[n]), shp, ax).astype(bf16) for n, shp, ax in _BIG]
    pairs = _pair_exchange(g_slots, "exchange_grads_pair")
    chip_sums = [_pair_sum(b, "pair_sum_" + n) for b, (n, _, _) in zip(pairs, _BIG)]
    parts = _chip_exchange(chip_sums, "exchange_grads_chip")
    out = {}
    for p, (n, shp, _) in zip(parts, _BIG):
        view = (shp[0] * shp[1], shp[2])
        res = _adamw_reduce(p, W[n].reshape(view), M[n].reshape(view), V[n].reshape(view), "adamw_" + n)
        out[n] = tuple(r.reshape(shp) for r in res)

    small = [n for n in names if n not in out and n != "ssd_conv_w"]
    G["mem_norm_g"] = d_mem_g.reshape(D)
    small_grads = [_lane_rows(G[n] if n == "mem_norm_g" else jnp.concatenate(G[n], axis=0)) for n in small]
    conv_grad = _lane_rows(jnp.stack(G["ssd_conv_w"]))
    sm_rows = [g.shape[0] for g in small_grads]
    n_small = sum(sm_rows)
    sm_total = -(-(n_small + conv_grad.shape[0]) // 8) * 8
    gathered = _allgather_vmem(_pack_rows(small_grads + [conv_grad], sm_total), "allgather_small_grads")
    whole = lambda s: pl.BlockSpec(s, lambda i: (0,) * len(s))
    g_sum = _fn_call(_sum_slots, (gathered,), [whole((N_DEV, sm_total, LANES))],
                     jax.ShapeDtypeStruct((sm_total, LANES), f32), whole((sm_total, LANES)), (1,), "sum_small_grads")
    conv_full = g_sum[n_small:n_small + 160].reshape(2, 4, SSD_CONV_DIM)
    conv_mine = lax.dynamic_slice_in_dim(conv_full, my_dev * 320, 320, axis=2)
    ad_total = n_small + 24
    pack = lambda d: _pack_rows([_lane_rows(d[n]) for n in small] + [_lane_rows(d["ssd_conv_w"])], ad_total)
    g_pack = _pack_rows([g_sum[:n_small], _lane_rows(conv_mine)], ad_total)
    blk = whole((ad_total, LANES))
    res_small = _fn_call(lambda g, w, m, v: _adamw_math(w, g, m, v), (g_pack, pack(W), pack(M), pack(V)), [blk] * 4,
                         (jax.ShapeDtypeStruct((ad_total, LANES), f32),) * 3, (blk,) * 3, (1,), "adamw_small")
    res_small = (g_pack,) + tuple(res_small)
    off = 0
    for n, rows in zip(small + ["ssd_conv_w"], sm_rows + [24]):
        size = W[n].size
        out[n] = tuple(r[off:off + rows].reshape(-1)[:size].reshape(W[n].shape) for r in res_small)
        off += rows

    return (loss, grad_x, *[out[n][0] for n in names], *[out[n][1] for n in names],
            *[out[n][2] for n in names], *[out[n][3] for n in names])
```

```python
import functools
import math

import jax
import jax.numpy as jnp
from jax import lax
from jax.experimental import pallas as pl
from jax.experimental.pallas import tpu as pltpu

f32, bf16 = jnp.float32, jnp.bfloat16
MESH = pl.DeviceIdType.MESH

N_DEV = 8
D = 1024
DEPTH = 4
EPS = 1e-6
MEM_LEN, MEM_HEADS, MEM_W, HD = 256, 4, 512, 128
SSD_INNER, SSD_HEADS, SSD_G, SSD_P, SSD_N, SSD_L = 1536, 24, 4, 64, 128, 128
SSD_GW = SSD_INNER // SSD_G
SSD_CONV_DIM = 2560
SSD_IN = 4632
SSD_IN_PAD = 4736
SB_W, SB_HEADS, SB_IN = 1536, 12, 5120
SB_BLK = 256
SB_SCALE = HD ** -0.5
SB_DEAD = 105.0
FFN_H = 2816
LANES = 128
VMEM_LIMIT = 56 * 1024 * 1024

ADAM_LR, ADAM_B1, ADAM_B2, ADAM_EPS, ADAM_WD, ADAM_STEP = 0.001, 0.9, 0.999, 1e-08, 0.01, 10

HIGHEST = lax.Precision.HIGHEST


def _params(n_grid):
    return pltpu.CompilerParams(dimension_semantics=("arbitrary",) * n_grid, vmem_limit_bytes=VMEM_LIMIT)


def _dg(a, b, ca, cb):
    return lax.dot_general(a.astype(bf16), b.astype(bf16), (((ca,), (cb,)), ((), ())), preferred_element_type=f32)


@jax.custom_vjp
def bdot_nn(a, b):
    return _dg(a, b, 1, 0)


def _nn_fwd(a, b):
    return _dg(a, b, 1, 0), (a, b)


def _nn_bwd(res, ct):
    a, b = res
    return _dg(ct, b, 1, 1).astype(a.dtype), _dg(a, ct, 0, 0).astype(b.dtype)


bdot_nn.defvjp(_nn_fwd, _nn_bwd)


@jax.custom_vjp
def bdot_nt(a, b):
    return _dg(a, b, 1, 1)


def _nt_fwd(a, b):
    return _dg(a, b, 1, 1), (a, b)


def _nt_bwd(res, ct):
    a, b = res
    return _dg(ct, b, 1, 0).astype(a.dtype), _dg(ct, a, 0, 0).astype(b.dtype)


bdot_nt.defvjp(_nt_fwd, _nt_bwd)


@jax.custom_vjp
def bdot_tn(a, b):
    return _dg(a, b, 0, 0)


def _tn_fwd(a, b):
    return _dg(a, b, 0, 0), (a, b)


def _tn_bwd(res, ct):
    a, b = res
    return _dg(b, ct, 1, 1).astype(a.dtype), _dg(a, ct, 1, 0).astype(b.dtype)


bdot_tn.defvjp(_tn_fwd, _tn_bwd)


def _rms(x, g):
    return x * lax.rsqrt(jnp.mean(x * x, axis=-1, keepdims=True) + EPS) * g


def _iota(shape, axis):
    return lax.broadcasted_iota(jnp.int32, shape, axis)


def _fn_call(fn, args, in_specs, out_shapes, out_specs, grid, name, acc=None):
    n_in = len(args)
    acc = acc or {}
    n_grid = len(grid)

    def body(*refs):
        ins, outs = refs[:n_in], refs[n_in:]
        res = fn(*[r[...] for r in ins])
        if not isinstance(res, (tuple, list)):
            res = (res,)
        for k, (o, r) in enumerate(zip(outs, res)):
            mode = acc.get(k)
            if mode is None:
                o[...] = r.astype(o.dtype)
                continue
            if mode == "last":
                first = pl.program_id(n_grid - 1) == 0
            else:
                first = functools.reduce(jnp.logical_and, [pl.program_id(d) == 0 for d in range(n_grid)])

            @pl.when(first)
            def _(o=o, r=r):
                o[...] = r.astype(o.dtype)

            @pl.when(jnp.logical_not(first))
            def _(o=o, r=r):
                o[...] += r.astype(o.dtype)

    return pl.pallas_call(
        body, grid=grid, in_specs=in_specs, out_specs=out_specs, out_shape=out_shapes, name=name,
        compiler_params=_params(n_grid))(*args)


def _matmul(a, b, *, ta=False, tb=False, out_dtype=f32, tm, tn, tk, res=None, name):
    M, K = (a.shape[1], a.shape[0]) if ta else a.shape
    N = b.shape[0] if tb else b.shape[1]
    tm, tn, tk = min(tm, M), min(tn, N), min(tk, K)
    assert M % tm == 0 and N % tn == 0 and K % tk == 0, (name, M, N, K, tm, tn, tk)
    nk = K // tk
    a_spec = pl.BlockSpec((tk, tm), lambda i, j, k: (k, i)) if ta else pl.BlockSpec((tm, tk), lambda i, j, k: (i, k))
    b_spec = pl.BlockSpec((tn, tk), lambda i, j, k: (j, k)) if tb else pl.BlockSpec((tk, tn), lambda i, j, k: (k, j))
    o_spec = pl.BlockSpec((tm, tn), lambda i, j, k: (i, j))
    ca, cb = (0 if ta else 1), (1 if tb else 0)

    def body(*refs):
        a_ref, b_ref = refs[:2]
        r_ref = None if res is None else refs[2]
        o_ref = refs[2 if res is None else 3]
        part = _dg(a_ref[...], b_ref[...], ca, cb)

        def finish(out):
            if r_ref is not None:
                out = out + r_ref[...].astype(f32)
            o_ref[...] = out.astype(o_ref.dtype)

        if nk == 1:
            finish(part)
            return
        acc_ref = refs[-1]
        k = pl.program_id(2)

        @pl.when(k == 0)
        def _():
            acc_ref[...] = part

        @pl.when(k > 0)
        def _():
            acc_ref[...] += part

        @pl.when(k == nk - 1)
        def _():
            finish(acc_ref[...])

    args = (a, b) if res is None else (a, b, res)
    in_specs = [a_spec, b_spec] + ([] if res is None else [o_spec])
    return pl.pallas_call(
        body, grid=(M // tm, N // tn, nk), in_specs=in_specs, out_specs=o_spec,
        out_shape=jax.ShapeDtypeStruct((M, N), out_dtype), name=name,
        scratch_shapes=[] if nk == 1 else [pltpu.VMEM((tm, tn), f32)], compiler_params=_params(3))(*args)


def _row_tile(T):
    return min(T, 512)


def _my_pos():
    return lax.axis_index("x"), lax.axis_index("y"), lax.axis_index("c")


def _allgather_hbm(xs, name):
    R, C = xs.shape

    def body(x_ref, out_ref, send_sems, recv_sems, local_sem):
        x, y, c = _my_pos()
        me, sibling = (x, y, c), (x, y, 1 - c)
        chips = [(1 - x, y), (x, 1 - y), (1 - x, 1 - y)]

        def slot(px, py, pc):
            return out_ref.at[4 * px + 2 * py + pc]

        def copy(k, block, to, src=None):
            return pltpu.make_async_remote_copy(
                src_ref=slot(*block) if src is None else src, dst_ref=slot(*block),
                send_sem=send_sems.at[k], recv_sem=recv_sems.at[k], device_id=to, device_id_type=MESH)

        mine = pltpu.make_async_copy(x_ref, slot(*me), local_sem)
        mine.start()
        first = [copy(0, me, sibling, src=x_ref)]
        first += [copy(1 + j, me, (*chip, c), src=x_ref) for j, chip in enumerate(chips)]
        for cp in first:
            cp.start()
        passed = [copy(4 + j, (*chip, c), sibling) for j, chip in enumerate(chips)]
        for j, chip in enumerate(chips):
            copy(1 + j, (*chip, c), me).wait_recv()
            passed[j].start()
        copy(0, sibling, me).wait_recv()
        for j, chip in enumerate(chips):
            copy(4 + j, (*chip, 1 - c), me).wait_recv()
        for cp in first + passed:
            cp.wait_send()
        mine.wait()

    return pl.pallas_call(
        body, out_shape=jax.ShapeDtypeStruct((N_DEV, R, C), xs.dtype),
        in_specs=[pl.BlockSpec(memory_space=pl.ANY)], out_specs=pl.BlockSpec(memory_space=pl.ANY),
        scratch_shapes=[pltpu.SemaphoreType.DMA((7,)), pltpu.SemaphoreType.DMA((7,)), pltpu.SemaphoreType.DMA],
        name=name)(xs)


def _allgather_vmem(xs, name):
    R, C = xs.shape

    def body(x_ref, out_ref, send_sems, recv_sems):
        x, y, c = _my_pos()
        me = 4 * x + 2 * y + c
        out_ref[me] = x_ref[...]
        copies = []
        for k in range(1, N_DEV):
            px = 1 - x if k & 4 else x
            py = 1 - y if k & 2 else y
            pc = 1 - c if k & 1 else c
            cp = pltpu.make_async_remote_copy(
                src_ref=x_ref, dst_ref=out_ref.at[me], send_sem=send_sems.at[k - 1], recv_sem=recv_sems.at[k - 1],
                device_id=(px, py, pc), device_id_type=MESH)
            cp.start()
            copies.append(cp)
        for cp in copies:
            cp.wait()

    return pl.pallas_call(
        body, out_shape=jax.ShapeDtypeStruct((N_DEV, R, C), xs.dtype),
        in_specs=[pl.BlockSpec(memory_space=pltpu.VMEM)], out_specs=pl.BlockSpec(memory_space=pltpu.VMEM),
        scratch_shapes=[pltpu.SemaphoreType.DMA((7,)), pltpu.SemaphoreType.DMA((7,))], name=name)(xs)


N_CHIP = 4


def _pair_exchange(gs, name):
    n = len(gs)

    def body(*refs):
        g_refs, out_refs = refs[:n], refs[n:2 * n]
        send_sems, recv_sems = refs[2 * n:]
        x, y, c = _my_pos()
        copies = []
        for a in range(n):
            for k in range(N_CHIP):
                cp = pltpu.make_async_remote_copy(
                    src_ref=g_refs[a].at[4 * (k >> 1) + 2 * (k & 1) + 1 - c], dst_ref=out_refs[a].at[k],
                    send_sem=send_sems.at[a, k], recv_sem=recv_sems.at[a, k],
                    device_id=(x, y, 1 - c), device_id_type=MESH)
                cp.start()
                copies.append(cp)
        for cp in copies:
            cp.wait()

    hbm = pl.BlockSpec(memory_space=pl.ANY)
    return pl.pallas_call(
        body, out_shape=[jax.ShapeDtypeStruct((N_CHIP,) + g.shape[1:], g.dtype) for g in gs],
        in_specs=[hbm] * n, out_specs=[hbm] * n,
        scratch_shapes=[pltpu.SemaphoreType.DMA((n, N_CHIP))] * 2, name=name)(*gs)


def _chip_exchange(cs, name):
    n = len(cs)

    def body(*refs):
        c_refs, out_refs = refs[:n], refs[n:2 * n]
        send_sems, recv_sems, local_sems = refs[2 * n:]
        x, y, c = _my_pos()
        my_chip = 2 * x + y
        copies = []
        for a in range(n):
            cp = pltpu.make_async_copy(c_refs[a].at[my_chip], out_refs[a].at[my_chip], local_sems.at[a])
            cp.start()
            copies.append(cp)
            for rel in range(1, N_CHIP):
                px = 1 - x if rel & 2 else x
                py = 1 - y if rel & 1 else y
                cp = pltpu.make_async_remote_copy(
                    src_ref=c_refs[a].at[2 * px + py], dst_ref=out_refs[a].at[my_chip],
                    send_sem=send_sems.at[a, rel - 1], recv_sem=recv_sems.at[a, rel - 1],
                    device_id=(px, py, c), device_id_type=MESH)
                cp.start()
                copies.append(cp)
        for cp in copies:
            cp.wait()

    hbm = pl.BlockSpec(memory_space=pl.ANY)
    return pl.pallas_call(
        body, out_shape=[jax.ShapeDtypeStruct(g.shape, g.dtype) for g in cs],
        in_specs=[hbm] * n, out_specs=[hbm] * n,
        scratch_shapes=[pltpu.SemaphoreType.DMA((n, N_CHIP - 1)), pltpu.SemaphoreType.DMA((n, N_CHIP - 1)),
                        pltpu.SemaphoreType.DMA((n,))], name=name)(*cs)


def _shard_row_tile(rows):
    return next(t for t in (256, 128) if rows % t == 0)


def _pair_sum(g, recv, core, name):
    _, rows, cols = g.shape
    tr = _shard_row_tile(rows)

    def body(core_ref, g0, g1, g2, g3, recv_ref, o_ref):
        for k, g_ref in enumerate((g0, g1, g2, g3)):
            o_ref[k] = (g_ref[...].astype(f32) + recv_ref[k].astype(f32)).astype(o_ref.dtype)

    mine = [pl.BlockSpec((None, tr, cols), lambda i, core, k=k: (4 * (k >> 1) + 2 * (k & 1) + core[0], i, 0))
            for k in range(N_CHIP)]
    four = pl.BlockSpec((N_CHIP, tr, cols), lambda i, core: (0, i, 0))
    return pl.pallas_call(
        body, grid_spec=pltpu.PrefetchScalarGridSpec(
            num_scalar_prefetch=1, grid=(rows // tr,), in_specs=mine + [four], out_specs=four),
        out_shape=jax.ShapeDtypeStruct((N_CHIP, rows, cols), g.dtype), name=name,
        compiler_params=_params(1))(core, g, g, g, g, recv)


def _adamw_math(w, g, m, v):
    m = ADAM_B1 * m + (1.0 - ADAM_B1) * g
    v = ADAM_B2 * v + (1.0 - ADAM_B2) * jnp.square(g)
    m_hat = m / (1.0 - ADAM_B1 ** ADAM_STEP)
    v_hat = v / (1.0 - ADAM_B2 ** ADAM_STEP)
    delta = -ADAM_LR * (m_hat / (jnp.sqrt(v_hat) + ADAM_EPS) + ADAM_WD * w)
    return delta, m, v


def _sum_slots(parts):
    g = parts[0].astype(f32)
    for i in range(1, parts.shape[0]):
        g = g + parts[i].astype(f32)
    return g


def _adamw_reduce(parts, w, m, v, name):
    rows, cols = w.shape
    tr = _shard_row_tile(rows)

    def fn(p, w, m, v):
        g = _sum_slots(p)
        return (g,) + _adamw_math(w, g, m, v)

    row = pl.BlockSpec((tr, cols), lambda i: (i, 0))
    sds = jax.ShapeDtypeStruct((rows, cols), f32)
    return _fn_call(fn, (parts, w, m, v), [pl.BlockSpec((N_CHIP, tr, cols), lambda i: (0, i, 0)), row, row, row],
                    (sds,) * 4, (row,) * 4, (rows // tr,), name)


def _rmsnorm_fwd(x, g, name):
    T = x.shape[0]
    tm = _row_tile(T)
    row = pl.BlockSpec((tm, D), lambda i: (i, 0))
    par = pl.BlockSpec((1, D), lambda i: (0, 0))
    return _fn_call(lambda x, g: _rms(x, g), (x, g), [row, par], jax.ShapeDtypeStruct((T, D), bf16), row, (T // tm,), name)


def _rmsnorm_bwd(x, g, dh, dres, name):
    T = x.shape[0]
    tm = _row_tile(T)
    row = pl.BlockSpec((tm, D), lambda i: (i, 0))
    par = pl.BlockSpec((1, D), lambda i: (0, 0))

    def fn(x, g, dh, dres):
        _, vjp = jax.vjp(_rms, x, g)
        dx, dg = vjp(dh.astype(f32))
        return dx + dres, dg

    return _fn_call(fn, (x, g, dh, dres), [row, par, row, row],
                    (jax.ShapeDtypeStruct((T, D), f32), jax.ShapeDtypeStruct((1, D), f32)), (row, par),
                    (T // tm,), name, acc={1: "all"})


def _swiglu_act(g, u):
    return jax.nn.silu(g) * u


def _ffn_up_act(h2, w_gu, name):
    T = h2.shape[0]
    tm = min(T, 256)

    def body(h_ref, w_ref, gu_ref, act_ref):
        gu = _dg(h_ref[...], w_ref[...], 1, 0)
        gu_ref[...] = gu
        act_ref[...] = _swiglu_act(gu[:, :FFN_H], gu[:, FFN_H:]).astype(act_ref.dtype)

    return pl.pallas_call(
        body, grid=(T // tm,),
        in_specs=[pl.BlockSpec((tm, D), lambda i: (i, 0)), pl.BlockSpec((D, 2 * FFN_H), lambda i: (0, 0))],
        out_specs=(pl.BlockSpec((tm, 2 * FFN_H), lambda i: (i, 0)), pl.BlockSpec((tm, FFN_H), lambda i: (i, 0))),
        out_shape=(jax.ShapeDtypeStruct((T, 2 * FFN_H), f32), jax.ShapeDtypeStruct((T, FFN_H), bf16)),
        name=name, compiler_params=_params(1))(h2, w_gu)


def _ffn_dgu(dx, w_down, gu, name):
    T = dx.shape[0]
    tm = min(T, 256)

    def body(dx_ref, w_ref, gu_ref, o_ref):
        dact = _dg(dx_ref[...], w_ref[...], 1, 1)
        gu = gu_ref[...]
        _, vjp = jax.vjp(_swiglu_act, gu[:, :FFN_H], gu[:, FFN_H:])
        dg, du = vjp(dact)
        o_ref[:, :FFN_H] = dg.astype(o_ref.dtype)
        o_ref[:, FFN_H:] = du.astype(o_ref.dtype)

    return pl.pallas_call(
        body, grid=(T // tm,),
        in_specs=[pl.BlockSpec((tm, D), lambda i: (i, 0)), pl.BlockSpec((FFN_H, D), lambda i: (0, 0)),
                  pl.BlockSpec((tm, 2 * FFN_H), lambda i: (i, 0))],
        out_specs=pl.BlockSpec((tm, 2 * FFN_H), lambda i: (i, 0)),
        out_shape=jax.ShapeDtypeStruct((T, 2 * FFN_H), bf16), name=name, compiler_params=_params(1))(dx, w_down, gu)


def _loss_head(x, target, name):
    T = x.shape[0]
    tm = _row_tile(T)
    row = pl.BlockSpec((tm, D), lambda i: (i, 0))
    par = pl.BlockSpec((1, LANES), lambda i: (0, 0))

    def fn(x, t):
        e = x - t
        s = jnp.sum(e * e, axis=0, keepdims=True)
        part = s[:, 0:LANES]
        for k in range(1, D // LANES):
            part = part + s[:, k * LANES:(k + 1) * LANES]
        return e * (1.0 / D), part * (0.5 / D)

    return _fn_call(fn, (x, target), [row, row],
                    (jax.ShapeDtypeStruct((T, D), f32), jax.ShapeDtypeStruct((1, LANES), f32)), (row, par),
                    (T // tm,), name, acc={1: "all"})


def _memkv_fn(mem, mg, wkv, kg):
    mn = _rms(mem, mg)
    kv = bdot_nn(mn, wkv)
    ks = [_rms(kv[:, h * HD:(h + 1) * HD], kg) for h in range(MEM_HEADS)]
    return jnp.concatenate(ks, axis=1), kv[:, MEM_W:]


def _memkv_fwd(mem, mg, wkv, kg, name):
    whole = lambda s: pl.BlockSpec(s, lambda i: (0,) * len(s))
    sds = jax.ShapeDtypeStruct((MEM_LEN, MEM_W), f32)
    return _fn_call(lambda m, g, w, k: _memkv_fn(m, g, w.astype(f32), k), (mem, mg, wkv, kg),
                    [whole((MEM_LEN, D)), whole((1, D)), whole((D, 2 * MEM_W)), whole((1, HD))],
                    (sds, sds), (whole((MEM_LEN, MEM_W)),) * 2, (1,), name)


def _memkv_bwd(mem, mg, wkv, kg, dk, dv, name):
    whole = lambda s: pl.BlockSpec(s, lambda i: (0,) * len(s))

    def fn(m, g, w, k, dk, dv):
        _, vjp = jax.vjp(lambda g, w, k: _memkv_fn(m, g, w, k), g, w.astype(f32), k)
        return vjp((dk, dv))

    return _fn_call(fn, (mem, mg, wkv, kg, dk, dv),
                    [whole((MEM_LEN, D)), whole((1, D)), whole((D, 2 * MEM_W)), whole((1, HD)),
                     whole((MEM_LEN, MEM_W)), whole((MEM_LEN, MEM_W))],
                    (jax.ShapeDtypeStruct((1, D), f32), jax.ShapeDtypeStruct((D, 2 * MEM_W), f32),
                     jax.ShapeDtypeStruct((1, HD), f32)),
                    (whole((1, D)), whole((D, 2 * MEM_W)), whole((1, HD))), (1,), name)


def _memattn_fn(q, k, v, qg):
    qn = _rms(q, qg)
    s = bdot_nt(qn, k) * (HD ** -0.5)
    s = s - jnp.max(s, axis=-1, keepdims=True)
    p = jnp.exp(s)
    p = p / jnp.sum(p, axis=-1, keepdims=True)
    return bdot_nn(p, v)


def _memattn_fwd(proj, q_col, k, v, qg, name):
    T = proj.shape[0]
    tm = _row_tile(T)
    return _fn_call(_memattn_fn, (proj, k, v, qg),
                    [pl.BlockSpec((tm, HD), lambda h, i: (i, q_col + h)), pl.BlockSpec((MEM_LEN, HD), lambda h, i: (0, h)),
                     pl.BlockSpec((MEM_LEN, HD), lambda h, i: (0, h)), pl.BlockSpec((1, HD), lambda h, i: (0, 0))],
                    jax.ShapeDtypeStruct((T, MEM_W), bf16), pl.BlockSpec((tm, HD), lambda h, i: (i, h)),
                    (MEM_HEADS, T // tm), name)


def _memattn_bwd(proj, q_col, k, v, qg, dycat, do_col, name):
    T = proj.shape[0]
    tm = _row_tile(T)

    def fn(q, k, v, qg, do):
        _, vjp = jax.vjp(_memattn_fn, q, k, v, qg)
        dq, dk, dv, dg = vjp(do.astype(f32))
        return dq, dk, dv, dg[None]

    kv_spec = pl.BlockSpec((MEM_LEN, HD), lambda h, i: (0, h))
    kv_sds = jax.ShapeDtypeStruct((MEM_LEN, MEM_W), f32)
    return _fn_call(fn, (proj, k, v, qg, dycat),
                    [pl.BlockSpec((tm, HD), lambda h, i: (i, q_col + h)), kv_spec, kv_spec,
                     pl.BlockSpec((1, HD), lambda h, i: (0, 0)), pl.BlockSpec((tm, HD), lambda h, i: (i, do_col + h))],
                    (jax.ShapeDtypeStruct((T, MEM_W), bf16), kv_sds, kv_sds, jax.ShapeDtypeStruct((MEM_HEADS, 1, HD), f32)),
                    (pl.BlockSpec((tm, HD), lambda h, i: (i, h)), kv_spec, kv_spec,
                     pl.BlockSpec((1, 1, HD), lambda h, i: (h, 0, 0))),
                    (MEM_HEADS, T // tm), name, acc={1: "last", 2: "last", 3: "last"})


def _conv_taps(xp, w, first, tm):
    out = w[0:1, :] * xp[first:first + tm, :]
    for k in range(1, 4):
        out = out + w[k:k + 1, :] * xp[first + k:first + k + tm, :]
    return out


def _conv_blocks(T):
    tm, tc = _row_tile(T), 512
    nt = T // tm
    cur = pl.BlockSpec((tm, tc), lambda j, i: (i, 3 + j))
    prev = pl.BlockSpec((8, tc), lambda j, i: (jnp.maximum(i * (tm // 8) - 1, 0), 3 + j))
    par4 = pl.BlockSpec((4, tc), lambda j, i: (0, j))
    par1 = pl.BlockSpec((1, tc), lambda j, i: (0, j))
    out = pl.BlockSpec((tm, tc), lambda j, i: (i, j))
    return tm, tc, nt, cur, prev, par4, par1, out


def _conv_fwd(proj, w, b, name):
    T = proj.shape[0]
    tm, tc, nt, cur, prev, par4, par1, out = _conv_blocks(T)

    def body(prev_ref, cur_ref, w_ref, b_ref, o_ref):
        halo = jnp.where(pl.program_id(1) == 0, 0.0, prev_ref[...])
        xp = jnp.concatenate([halo, cur_ref[...]], axis=0)
        o_ref[...] = jax.nn.silu(_conv_taps(xp, w_ref[...], 5, tm) + b_ref[...])

    return pl.pallas_call(body, grid=(SSD_CONV_DIM // tc, nt), in_specs=[prev, cur, par4, par1], out_specs=out,
                          out_shape=jax.ShapeDtypeStruct((T, SSD_CONV_DIM), f32), name=name,
                          compiler_params=_params(2))(proj, proj, w, b)


def _conv_bwd_pre(proj, w, b, dact, name):
    T = proj.shape[0]
    tm, tc, nt, cur, prev, par4, par1, out = _conv_blocks(T)

    def body(prev_ref, cur_ref, w_ref, b_ref, da_ref, dp_ref, dw_ref, db_ref):
        i = pl.program_id(1)
        halo = jnp.where(i == 0, 0.0, prev_ref[...])
        xp = jnp.concatenate([halo, cur_ref[...]], axis=0)
        pre = _conv_taps(xp, w_ref[...], 5, tm) + b_ref[...]
        sig = jax.nn.sigmoid(pre)
        dpre = da_ref[...] * (sig * (1.0 + pre * (1.0 - sig)))
        dp_ref[...] = dpre
        dw = jnp.concatenate([jnp.sum(dpre * xp[5 + k:5 + k + tm, :], axis=0, keepdims=True) for k in range(4)], axis=0)
        db = jnp.sum(dpre, axis=0, keepdims=True)

        @pl.when(i == 0)
        def _():
            dw_ref[...] = dw
            db_ref[...] = db

        @pl.when(i > 0)
        def _():
            dw_ref[...] += dw
            db_ref[...] += db

    return pl.pallas_call(
        body, grid=(SSD_CONV_DIM // tc, nt), in_specs=[prev, cur, par4, par1, out], out_specs=(out, par4, par1),
        out_shape=(jax.ShapeDtypeStruct((T, SSD_CONV_DIM), f32), jax.ShapeDtypeStruct((4, SSD_CONV_DIM), f32),
                   jax.ShapeDtypeStruct((1, SSD_CONV_DIM), f32)),
        name=name, compiler_params=_params(2))(proj, proj, w, b, dact)


def _conv_bwd_in(dpre, w, name):
    T = dpre.shape[0]
    tm, tc, nt, _, _, par4, _, out = _conv_blocks(T)
    nxt = pl.BlockSpec((8, tc), lambda j, i: (jnp.minimum((i + 1) * (tm // 8), T // 8 - 1), j))

    def body(cur_ref, nxt_ref, w_ref, o_ref):
        halo = jnp.where(pl.program_id(1) == nt - 1, 0.0, nxt_ref[...])
        xp = jnp.concatenate([cur_ref[...], halo], axis=0)
        w = w_ref[...]
        acc = w[3:4, :] * xp[0:tm, :]
        for k in range(3):
            acc = acc + w[k:k + 1, :] * xp[3 - k:3 - k + tm, :]
        o_ref[...] = acc.astype(o_ref.dtype)

    return pl.pallas_call(body, grid=(SSD_CONV_DIM // tc, nt), in_specs=[out, nxt, par4], out_specs=out,
                          out_shape=jax.ShapeDtypeStruct((T, SSD_CONV_DIM), bf16), name=name,
                          compiler_params=_params(2))(dpre, dpre, w)


def _ssd_chunk(hbase, xs, bm, cm, z, dtr, dtb, alog, dsk, ng, ht):
    L = SSD_L
    dt = jax.nn.softplus(dtr + dtb)
    da = dt * (-jnp.exp(alog))
    li, si = _iota((L, L), 0), _iota((L, L), 1)
    causal = li >= si
    cs = jnp.dot(causal.astype(f32), da, precision=HIGHEST, preferred_element_type=f32)
    cs_t = cs.T
    chan_head = _iota((1, SSD_GW), 1) // SSD_P
    heads = range(SSD_GW // SSD_P)
    lane_of = [(_iota((1, LANES), 1) == hbase + r).astype(f32) for r in heads]
    cs_cols = [jnp.sum(cs * lane_of[r], axis=1, keepdims=True) for r in heads]
    dt_cols = [jnp.sum(dt * lane_of[r], axis=1, keepdims=True) for r in heads]
    cs_e = jnp.zeros((L, SSD_GW), f32)
    dt_e = jnp.zeros((L, SSD_GW), f32)
    for r in heads:
        cs_e = jnp.where(chan_head == r, cs_cols[r], cs_e)
        dt_e = jnp.where(chan_head == r, dt_cols[r], dt_e)
    xdt = xs * dt_e
    cb = bdot_nt(cm, bm)
    y = jnp.zeros((L, SSD_GW), f32)
    for r in heads:
        cs_row = jnp.sum(cs_t * (_iota((LANES, 1), 0) == hbase + r).astype(f32), axis=0, keepdims=True)
        decay = jnp.where(causal, jnp.exp(jnp.where(causal, cs_cols[r] - cs_row, 0.0)), 0.0)
        y = y + bdot_nn(cb * decay, xdt * (chan_head == r).astype(f32))
    y = y + jnp.exp(cs_e) * bdot_nn(cm, ht)
    cs_last = jnp.sum(cs_e * (_iota((L, 1), 0) == L - 1).astype(f32), axis=0, keepdims=True)
    ht_new = ht * jnp.exp(cs_last) + bdot_tn(bm, xdt * jnp.exp(cs_last - cs_e))
    y = (y + dsk * xs) * jax.nn.silu(z)
    return _rms(y, ng), ht_new


def _ssd_specs(T, rev):
    nc = T // SSD_L
    cidx = (lambda c: nc - 1 - c) if rev else (lambda c: c)
    return nc, dict(
        xs=pl.BlockSpec((SSD_L, SSD_GW), lambda g, c: (cidx(c), g)),
        bm=pl.BlockSpec((SSD_L, SSD_N), lambda g, c: (cidx(c), 12 + g)),
        cm=pl.BlockSpec((SSD_L, SSD_N), lambda g, c: (cidx(c), 16 + g)),
        z=pl.BlockSpec((SSD_L, SSD_GW), lambda g, c: (cidx(c), g)),
        dt=pl.BlockSpec((SSD_L, LANES), lambda g, c: (cidx(c), 36)),
        p128=pl.BlockSpec((1, LANES), lambda g, c: (0, 0)),
        pgw=pl.BlockSpec((1, SSD_GW), lambda g, c: (0, g)),
        hs=pl.BlockSpec((None, None, SSD_N, SSD_GW), lambda g, c: (g, cidx(c), 0, 0)),
        grp=pl.BlockSpec((SSD_L, SSD_N), lambda g, c: (cidx(c), g)),
    )


def _ssd_fwd(xbc, proj, dtb, alog, dsk, ng, name):
    T = proj.shape[0]
    nc, s = _ssd_specs(T, False)

    def body(xs_ref, bm_ref, cm_ref, z_ref, dt_ref, dtb_ref, alog_ref, dsk_ref, ng_ref, y_ref, hs_ref, h_scr):
        @pl.when(pl.program_id(1) == 0)
        def _():
            h_scr[...] = jnp.zeros_like(h_scr)

        ht = h_scr[...]
        hs_ref[...] = ht
        y, ht_new = _ssd_chunk(pl.program_id(0) * (SSD_GW // SSD_P), xs_ref[...], bm_ref[...], cm_ref[...], z_ref[...],
                               dt_ref[...], dtb_ref[...], alog_ref[...], dsk_ref[...], ng_ref[...], ht)
        y_ref[...] = y.astype(y_ref.dtype)
        h_scr[...] = ht_new

    return pl.pallas_call(
        body, grid=(SSD_G, nc),
        in_specs=[s["xs"], s["bm"], s["cm"], s["z"], s["dt"], s["p128"], s["p128"], s["pgw"], s["pgw"]],
        out_specs=(s["xs"], s["hs"]),
        out_shape=(jax.ShapeDtypeStruct((T, SSD_INNER), bf16), jax.ShapeDtypeStruct((SSD_G, nc, SSD_N, SSD_GW), f32)),
        scratch_shapes=[pltpu.VMEM((SSD_N, SSD_GW), f32)], name=name, compiler_params=_params(2))(
            xbc, xbc, xbc, proj, proj, dtb, alog, dsk, ng)


def _ssd_bwd(xbc, proj, dtb, alog, dsk, ng, hs, dycat, name):
    T = proj.shape[0]
    nc, s = _ssd_specs(T, True)

    def body(xs_ref, bm_ref, cm_ref, z_ref, dt_ref, dtb_ref, alog_ref, dsk_ref, ng_ref, hs_ref, dy_ref,
             dxs_ref, dbm_ref, dcm_ref, dz_ref, ddt_ref, ddtb_ref, dalog_ref, ddsk_ref, dng_ref, dh_scr):
        c = pl.program_id(1)

        @pl.when(c == 0)
        def _():
            dh_scr[...] = jnp.zeros_like(dh_scr)

        hbase = pl.program_id(0) * (SSD_GW // SSD_P)
        _, vjp = jax.vjp(functools.partial(_ssd_chunk, hbase), xs_ref[...], bm_ref[...], cm_ref[...], z_ref[...],
                         dt_ref[...], dtb_ref[...], alog_ref[...], dsk_ref[...], ng_ref[...], hs_ref[...])
        dxs, dbm, dcm, dz, ddt, ddtb, dalog, ddsk, dng, dht = vjp((dy_ref[...].astype(f32), dh_scr[...]))
        dxs_ref[...] = dxs
        dbm_ref[...] = dbm
        dcm_ref[...] = dcm
        dz_ref[...] = dz.astype(dz_ref.dtype)
        ddt_ref[...] = ddt
        dh_scr[...] = dht

        @pl.when(c == 0)
        def _():
            ddtb_ref[...] = ddtb
            dalog_ref[...] = dalog
            ddsk_ref[...] = ddsk
            dng_ref[...] = dng

        @pl.when(c > 0)
        def _():
            ddtb_ref[...] += ddtb
            dalog_ref[...] += dalog
            ddsk_ref[...] += ddsk
            dng_ref[...] += dng

    cidx = lambda c: nc - 1 - c
    g128 = pl.BlockSpec((None, 1, LANES), lambda g, c: (g, 0, 0))
    return pl.pallas_call(
        body, grid=(SSD_G, nc),
        in_specs=[s["xs"], s["bm"], s["cm"], s["z"], s["dt"], s["p128"], s["p128"], s["pgw"], s["pgw"], s["hs"], s["xs"]],
        out_specs=(s["xs"], s["grp"], s["grp"], s["xs"],
                   pl.BlockSpec((None, SSD_L, LANES), lambda g, c: (g, cidx(c), 0)), g128, g128, s["pgw"], s["pgw"]),
        out_shape=(jax.ShapeDtypeStruct((T, SSD_INNER), f32), jax.ShapeDtypeStruct((T, SSD_G * SSD_N), f32),
                   jax.ShapeDtypeStruct((T, SSD_G * SSD_N), f32), jax.ShapeDtypeStruct((T, SSD_INNER), bf16),
                   jax.ShapeDtypeStruct((SSD_G, T, LANES), f32), jax.ShapeDtypeStruct((SSD_G, 1, LANES), f32),
                   jax.ShapeDtypeStruct((SSD_G, 1, LANES), f32), jax.ShapeDtypeStruct((1, SSD_INNER), f32),
                   jax.ShapeDtypeStruct((1, SSD_INNER), f32)),
        scratch_shapes=[pltpu.VMEM((SSD_N, SSD_GW), f32)], name=name, compiler_params=_params(2))(
            xbc, xbc, xbc, proj, proj, dtb, alog, dsk, ng, hs, dycat)


def _qk_norm_fn(q, k, qg, kg):
    return _rms(q, qg), _rms(k, kg)


def _sb_qknorm_fwd(proj, qg, kg, name):
    T = proj.shape[0]
    tm = _row_tile(T)
    par = pl.BlockSpec((1, HD), lambda h, i: (0, 0))
    out = pl.BlockSpec((tm, HD), lambda h, i: (i, h))
    sds = jax.ShapeDtypeStruct((T, SB_W), bf16)
    return _fn_call(_qk_norm_fn, (proj, proj, qg, kg),
                    [out, pl.BlockSpec((tm, HD), lambda h, i: (i, SB_HEADS + h)), par, par],
                    (sds, sds), (out, out), (SB_HEADS, T // tm), name)


def _sb_qknorm_bwd(proj, qg, kg, dqn, dkn, name):
    T = proj.shape[0]
    tm = _row_tile(T)
    par = pl.BlockSpec((1, HD), lambda h, i: (0, 0))
    out = pl.BlockSpec((tm, HD), lambda h, i: (i, h))
    gout = pl.BlockSpec((1, 1, HD), lambda h, i: (h, 0, 0))

    def fn(q, k, qg, kg, dqn, dkn):
        _, vjp = jax.vjp(_qk_norm_fn, q, k, qg, kg)
        dq, dk, dqg, dkg = vjp((dqn, dkn))
        return dq, dk, dqg[None], dkg[None]

    sds = jax.ShapeDtypeStruct((T, SB_W), bf16)
    gsds = jax.ShapeDtypeStruct((SB_HEADS, 1, HD), f32)
    return _fn_call(fn, (proj, proj, qg, kg, dqn, dkn),
                    [out, pl.BlockSpec((tm, HD), lambda h, i: (i, SB_HEADS + h)), par, par, out, out],
                    (sds, sds, gsds, gsds), (out, out, gout, gout), (SB_HEADS, T // tm), name, acc={2: "last", 3: "last"})


def _split_dot(a, tri):
    hi = a.astype(bf16)
    lo = (a - hi.astype(f32)).astype(bf16)
    return jnp.dot(hi, tri, preferred_element_type=f32) + jnp.dot(lo, tri, preferred_element_type=f32)


def _sb_weights(q, kblk, run, later, mask):
    z = _dg(q, kblk, 1, 1) * SB_SCALE
    t = jnp.log(1.0 + jnp.exp(-jnp.abs(z)))
    sp = jnp.maximum(z, 0.0) + t
    log_beta = jnp.minimum(z, 0.0) - t
    if mask is not None:
        sp = jnp.where(mask, sp, 0.0)
    w = jnp.exp(log_beta - _split_dot(sp, later) - run)
    if mask is not None:
        w = jnp.where(mask, w, 0.0)
    return jnp.exp(log_beta), sp, w


def _sb_older_blocks(qb, carry, step, run_of):
    def cond(state):
        i, cr = state
        return jnp.logical_and(i < qb, jnp.min(run_of(cr)) < SB_DEAD)

    def body(state):
        i, cr = state
        return i + 1, step(qb - 1 - i, cr)

    return lax.while_loop(cond, body, (jnp.int32(0), carry))[1]


def _sb_fwd(qn, kn, proj, name):
    T = qn.shape[0]
    B = min(SB_BLK, T)
    nq = T // B

    def body(q_ref, k_ref, v_ref, o_ref, ox_ref):
        qb = pl.program_id(1)
        q = q_ref[...]
        ri, ci = _iota((B, B), 0), _iota((B, B), 1)
        later = (ri > ci).astype(bf16)

        def block(kb, carry, mask):
            acc, acc_lo, run = carry
            off = pl.multiple_of(kb * B, B)
            _, sp, w = _sb_weights(q, k_ref[pl.ds(off, B), :], run, later, mask)
            vblk = v_ref[pl.ds(off, B), :]
            w_hi = w.astype(bf16)
            acc = acc + _dg(w_hi, vblk, 1, 0)
            acc_lo = acc_lo + _dg(w - w_hi.astype(f32), vblk, 1, 0)
            return acc, acc_lo, run + jnp.sum(sp, axis=1, keepdims=True)

        zero = jnp.zeros((B, HD), f32)
        carry = block(qb, (zero, zero, jnp.zeros((B, 1), f32)), ci < ri)
        carry = _sb_older_blocks(qb, carry, lambda kb, cr: block(kb, cr, None), lambda cr: cr[2])
        o_ref[...] = carry[0].astype(o_ref.dtype)
        ox_ref[...] = carry[0] + carry[1]

    blk = pl.BlockSpec((B, HD), lambda h, i: (i, h))
    return pl.pallas_call(
        body, grid=(SB_HEADS, nq),
        in_specs=[blk, pl.BlockSpec((T, HD), lambda h, i: (0, h)), pl.BlockSpec((T, HD), lambda h, i: (0, 2 * SB_HEADS + h))],
        out_specs=(blk, blk), out_shape=(jax.ShapeDtypeStruct((T, SB_W), bf16), jax.ShapeDtypeStruct((T, SB_W), f32)),
        name=name, compiler_params=_params(2))(qn, kn, proj)


def _sb_bwd(qn, kn, proj, o, dycat, name):
    T = qn.shape[0]
    B = min(SB_BLK, T)
    nq = T // B

    def body(q_ref, k_ref, v_ref, o_ref, do_ref, dq_ref, dk_ref, dv_ref):
        qb = pl.program_id(1)

        @pl.when(qb == 0)
        def _():
            dk_ref[...] = jnp.zeros_like(dk_ref)
            dv_ref[...] = jnp.zeros_like(dv_ref)

        q = q_ref[...]
        do = do_ref[...].astype(f32)
        do_b = do.astype(bf16)
        gtot = jnp.sum(do_b.astype(f32) * o_ref[...], axis=1, keepdims=True)
        ri, ci = _iota((B, B), 0), _iota((B, B), 1)
        later = (ri > ci).astype(bf16)
        from_here = (ri >= ci).astype(bf16)

        def block(kb, carry, mask):
            dq, run, rung = carry
            off = pl.multiple_of(kb * B, B)
            kblk = k_ref[pl.ds(off, B), :]
            sig, sp, w = _sb_weights(q, kblk, run, later, mask)
            g = w * _dg(do_b, v_ref[pl.ds(off, B), :], 1, 1)
            before = gtot - rung - _split_dot(g, from_here)
            dz = (g * (1.0 - sig) - sig * before) * SB_SCALE
            if mask is not None:
                dz = jnp.where(mask, dz, 0.0)
            dz_b = dz.astype(bf16)
            dv_ref[pl.ds(off, B), :] += _dg(w, do_b, 0, 0)
            dk_ref[pl.ds(off, B), :] += _dg(dz_b, q, 0, 0)
            dq = dq + _dg(dz_b, kblk, 1, 0)
            return dq, run + jnp.sum(sp, axis=1, keepdims=True), rung + jnp.sum(g, axis=1, keepdims=True)

        zero = jnp.zeros((B, 1), f32)
        carry = block(qb, (jnp.zeros((B, HD), f32), zero, zero), ci < ri)
        carry = _sb_older_blocks(qb, carry, lambda kb, cr: block(kb, cr, None), lambda cr: cr[1])
        dq_ref[...] = carry[0]

    blk = pl.BlockSpec((B, HD), lambda h, i: (i, h))
    full = pl.BlockSpec((T, HD), lambda h, i: (0, h))
    sds = jax.ShapeDtypeStruct((T, SB_W), f32)
    return pl.pallas_call(
        body, grid=(SB_HEADS, nq),
        in_specs=[blk, full, pl.BlockSpec((T, HD), lambda h, i: (0, 2 * SB_HEADS + h)), blk, blk],
        out_specs=(blk, full, full), out_shape=(sds, sds, sds), name=name, compiler_params=_params(2))(
            qn, kn, proj, o, dycat)


_BIG = (("mem_w_kv", (4, 128, 1024), 1), ("ssd_w_in", (2, 1024, 579), 2), ("ssd_w_out", (2, 256, 1024), 1),
        ("sb_w_in", (2, 1024, 640), 2), ("sb_w_out", (2, 256, 1024), 1), ("ffn_w_gate_up", (4, 1024, 704), 2),
        ("ffn_w_down", (4, 352, 1024), 1))
_BIG_ROWS = tuple(math.prod(s) // LANES for _, s, _ in _BIG)
_BIG_TOTAL = sum(_BIG_ROWS)


def _pack_rows(parts, total):
    rows = sum(p.shape[-2] for p in parts)
    if rows == total:
        return jnp.concatenate(list(parts), axis=-2)
    pad = jnp.zeros(parts[0].shape[:-2] + (total - rows, LANES), parts[0].dtype)
    return jnp.concatenate(list(parts) + [pad], axis=-2)


def _full_from_slots(slots, shard_shape, axis):
    n = shard_shape[0]
    s = slots.reshape((N_DEV,) + shard_shape)
    if axis == 1:
        return s.transpose(1, 0, 2, 3).reshape(n, N_DEV * shard_shape[1], shard_shape[2])
    return s.transpose(1, 2, 0, 3).reshape(n, shard_shape[1], N_DEV * shard_shape[2])


def _slots_from_full(full, shard_shape, axis):
    n = shard_shape[0]
    if axis == 1:
        s = full.reshape(n, N_DEV, shard_shape[1], shard_shape[2]).transpose(1, 0, 2, 3)
    else:
        s = full.reshape(n, shard_shape[1], N_DEV, shard_shape[2]).transpose(2, 0, 1, 3)
    return s.reshape(N_DEV, n * shard_shape[1], shard_shape[2])


def _ssd_in_cols(w):
    pad = jnp.zeros(w.shape[:-1] + (SSD_IN_PAD - SSD_IN,), w.dtype)
    return jnp.concatenate([w[..., :4096], w[..., 4120:4632], w[..., 4096:4120], pad], axis=-1)


def _ssd_in_cols_back(w):
    return jnp.concatenate([w[..., :4096], w[..., 4608:4632], w[..., 4096:4608]], axis=-1)


def _lane_rows(a):
    flat = a.reshape(-1)
    n = -(-flat.shape[0] // (8 * LANES)) * (8 * LANES)
    return jnp.pad(flat, (0, n - flat.shape[0])).reshape(-1, LANES)


def _pad128(a):
    return jnp.pad(a, ((0, 0), (0, LANES - a.shape[1])))


def kernel(x, mem, mix_norm_g, ffn_norm_g, mem_norm_g, mem_w_kv, mem_q_norm_g, mem_k_norm_g, ssd_w_in, ssd_conv_w, ssd_conv_b, ssd_dt_bias, ssd_a_log, ssd_d, ssd_norm_g, ssd_w_out, sb_w_in, sb_q_norm_g, sb_k_norm_g, sb_w_out, ffn_w_gate_up, ffn_w_down, loss_target, m_mix_norm_g, m_ffn_norm_g, m_mem_norm_g, m_mem_w_kv, m_mem_q_norm_g, m_mem_k_norm_g, m_ssd_w_in, m_ssd_conv_w, m_ssd_conv_b, m_ssd_dt_bias, m_ssd_a_log, m_ssd_d, m_ssd_norm_g, m_ssd_w_out, m_sb_w_in, m_sb_q_norm_g, m_sb_k_norm_g, m_sb_w_out, m_ffn_w_gate_up, m_ffn_w_down, v_mix_norm_g, v_ffn_norm_g, v_mem_norm_g, v_mem_w_kv, v_mem_q_norm_g, v_mem_k_norm_g, v_ssd_w_in, v_ssd_conv_w, v_ssd_conv_b, v_ssd_dt_bias, v_ssd_a_log, v_ssd_d, v_ssd_norm_g, v_ssd_w_out, v_sb_w_in, v_sb_q_norm_g, v_sb_k_norm_g, v_sb_w_out, v_ffn_w_gate_up, v_ffn_w_down):
    W = dict(mix_norm_g=mix_norm_g, ffn_norm_g=ffn_norm_g, mem_norm_g=mem_norm_g, mem_w_kv=mem_w_kv, mem_q_norm_g=mem_q_norm_g, mem_k_norm_g=mem_k_norm_g, ssd_w_in=ssd_w_in, ssd_conv_w=ssd_conv_w, ssd_conv_b=ssd_conv_b, ssd_dt_bias=ssd_dt_bias, ssd_a_log=ssd_a_log, ssd_d=ssd_d, ssd_norm_g=ssd_norm_g, ssd_w_out=ssd_w_out, sb_w_in=sb_w_in, sb_q_norm_g=sb_q_norm_g, sb_k_norm_g=sb_k_norm_g, sb_w_out=sb_w_out, ffn_w_gate_up=ffn_w_gate_up, ffn_w_down=ffn_w_down)
    M = dict(mix_norm_g=m_mix_norm_g, ffn_norm_g=m_ffn_norm_g, mem_norm_g=m_mem_norm_g, mem_w_kv=m_mem_w_kv, mem_q_norm_g=m_mem_q_norm_g, mem_k_norm_g=m_mem_k_norm_g, ssd_w_in=m_ssd_w_in, ssd_conv_w=m_ssd_conv_w, ssd_conv_b=m_ssd_conv_b, ssd_dt_bias=m_ssd_dt_bias, ssd_a_log=m_ssd_a_log, ssd_d=m_ssd_d, ssd_norm_g=m_ssd_norm_g, ssd_w_out=m_ssd_w_out, sb_w_in=m_sb_w_in, sb_q_norm_g=m_sb_q_norm_g, sb_k_norm_g=m_sb_k_norm_g, sb_w_out=m_sb_w_out, ffn_w_gate_up=m_ffn_w_gate_up, ffn_w_down=m_ffn_w_down)
    V = dict(mix_norm_g=v_mix_norm_g, ffn_norm_g=v_ffn_norm_g, mem_norm_g=v_mem_norm_g, mem_w_kv=v_mem_w_kv, mem_q_norm_g=v_mem_q_norm_g, mem_k_norm_g=v_mem_k_norm_g, ssd_w_in=v_ssd_w_in, ssd_conv_w=v_ssd_conv_w, ssd_conv_b=v_ssd_conv_b, ssd_dt_bias=v_ssd_dt_bias, ssd_a_log=v_ssd_a_log, ssd_d=v_ssd_d, ssd_norm_g=v_ssd_norm_g, ssd_w_out=v_ssd_w_out, sb_w_in=v_sb_w_in, sb_q_norm_g=v_sb_q_norm_g, sb_k_norm_g=v_sb_k_norm_g, sb_w_out=v_sb_w_out, ffn_w_gate_up=v_ffn_w_gate_up, ffn_w_down=v_ffn_w_down)
    names = list(W)
    T = x.shape[1]
    x0 = x.reshape(T, D)
    mem2 = mem.reshape(MEM_LEN, D)
    target = loss_target.reshape(T, D)
    my_dev = 4 * lax.axis_index("x") + 2 * lax.axis_index("y") + lax.axis_index("c")

    w_flat = _pack_rows([W[n].astype(bf16).reshape(-1, LANES) for n, _, _ in _BIG], _BIG_TOTAL)
    slots = _allgather_hbm(w_flat, "allgather_weights")
    full, off = {}, 0
    for (n, shp, ax), rows in zip(_BIG, _BIG_ROWS):
        full[n] = _full_from_slots(slots[:, off:off + rows], shp, ax)
        off += rows
    full["ssd_w_in"] = _ssd_in_cols(full["ssd_w_in"])
    conv_slots = _allgather_vmem(_lane_rows(ssd_conv_w), "allgather_conv_w")
    conv_w = _full_from_slots(conv_slots[:, :20], (2, 4, 320), 2)

    mem_g = mem_norm_g.reshape(1, D)

    saved = []
    xc = x0
    for i in range(DEPTH):
        j = i // 2
        ssd = i % 2 == 0
        L = f"l{i}_"
        mix_g = mix_norm_g[i:i + 1]
        h = _rmsnorm_fwd(xc, mix_g, L + "mix_norm")
        w_in = full["ssd_w_in"][j] if ssd else full["sb_w_in"][j]
        proj = _matmul(h, w_in, tm=256 if ssd else 1024, tn=w_in.shape[1] if ssd else 1024, tk=D, name=L + "in_proj")
        k_mem, v_mem = _memkv_fwd(mem2, mem_g, full["mem_w_kv"][i], mem_k_norm_g[i:i + 1], L + "mem_kv")
        q_col = 32 if ssd else 36
        o_mem = _memattn_fwd(proj, q_col, k_mem, v_mem, mem_q_norm_g[i:i + 1], L + "mem_attn")
        st = dict(x_in=xc, h=h, proj=proj, k_mem=k_mem, v_mem=v_mem)
        if ssd:
            xbc = _conv_fwd(proj, conv_w[j], ssd_conv_b[j:j + 1], L + "conv")
            dtb, alog = _pad128(ssd_dt_bias[j:j + 1]), _pad128(ssd_a_log[j:j + 1])
            dsk = jnp.repeat(ssd_d[j], SSD_P).reshape(1, SSD_INNER)
            y, hs = _ssd_fwd(xbc, proj, dtb, alog, dsk, ssd_norm_g[j:j + 1], L + "ssd_scan")
            st.update(xbc=xbc, hs=hs, dtb=dtb, alog=alog, dsk=dsk)
            w_out = full["ssd_w_out"][j]
        else:
            qn, kn = _sb_qknorm_fwd(proj, sb_q_norm_g[j:j + 1], sb_k_norm_g[j:j + 1], L + "qk_norm")
            y, o_exact = _sb_fwd(qn, kn, proj, L + "sb_attn")
            st.update(qn=qn, kn=kn, o=o_exact)
            w_out = full["sb_w_out"][j]
        ycat = jnp.concatenate([y, o_mem], axis=1)
        x_mid = _matmul(ycat, w_out, tm=512, tn=D, tk=2048, res=xc, name=L + "out_proj")
        h2 = _rmsnorm_fwd(x_mid, ffn_norm_g[i:i + 1], L + "ffn_norm")
        gu, act = _ffn_up_act(h2, full["ffn_w_gate_up"][i], L + "ffn_up")
        xc = _matmul(act, full["ffn_w_down"][i], tm=512, tn=D, tk=FFN_H, res=x_mid, name=L + "ffn_down")
        st.update(ycat=ycat, x_mid=x_mid, h2=h2, gu=gu, act=act, w_in=w_in, w_out=w_out)
        saved.append(st)

    dx, loss_part = _loss_head(xc, target, "loss_head")
    loss = lax.psum(jnp.sum(loss_part), ("x", "y", "c"))

    G = {n: [None] * W[n].shape[0] for n in names if W[n].ndim > 1}
    d_mem_g = jnp.zeros((1, D), f32)
    for i in reversed(range(DEPTH)):
        j = i // 2
        ssd = i % 2 == 0
        L = f"l{i}_b_"
        st = saved[i]
        proj = st["proj"]
        G["ffn_w_down"][i] = _matmul(st["act"], dx, ta=True, tm=FFN_H // 2, tn=D, tk=512, name=L + "dw_down")
        dgu = _ffn_dgu(dx, full["ffn_w_down"][i], st["gu"], L + "d_gu")
        dh2 = _matmul(dgu, full["ffn_w_gate_up"][i], tb=True, tm=512, tn=D, tk=2 * FFN_H, name=L + "d_h2")
        G["ffn_w_gate_up"][i] = _matmul(st["h2"], dgu, ta=True, tm=D, tn=FFN_H // 2, tk=512, name=L + "dw_up")
        dx, G["ffn_norm_g"][i] = _rmsnorm_bwd(st["x_mid"], ffn_norm_g[i:i + 1], dh2, dx, L + "d_ffn_norm")
        dycat = _matmul(dx, st["w_out"], tb=True, tm=512, tn=2048, tk=D, name=L + "d_ycat")
        g_out = _matmul(st["ycat"], dx, ta=True, tm=D, tn=D, tk=512, name=L + "dw_out")
        q_col = 32 if ssd else 36
        dq_mem, dk_mem, dv_mem, dqg = _memattn_bwd(proj, q_col, st["k_mem"], st["v_mem"], mem_q_norm_g[i:i + 1], dycat, 12, L + "d_mem_attn")
        G["mem_q_norm_g"][i] = jnp.sum(dqg, axis=0)
        dmg, G["mem_w_kv"][i], G["mem_k_norm_g"][i] = _memkv_bwd(mem2, mem_g, full["mem_w_kv"][i], mem_k_norm_g[i:i + 1], dk_mem, dv_mem, L + "d_mem_kv")
        d_mem_g = d_mem_g + dmg
        if ssd:
            G["ssd_w_out"][j] = g_out
            dxs, dbm, dcm, dz, ddt, ddtb, dalog, ddsk, dng = _ssd_bwd(
                st["xbc"], proj, st["dtb"], st["alog"], st["dsk"], ssd_norm_g[j:j + 1], st["hs"], dycat, L + "d_ssd_scan")
            G["ssd_dt_bias"][j] = jnp.sum(ddtb, axis=0)[:, :SSD_HEADS]
            G["ssd_a_log"][j] = jnp.sum(dalog, axis=0)[:, :SSD_HEADS]
            G["ssd_d"][j] = jnp.sum(ddsk.reshape(SSD_HEADS, SSD_P), axis=1).reshape(1, SSD_HEADS)
            G["ssd_norm_g"][j] = dng
            dxbc_act = jnp.concatenate([dxs, dbm, dcm], axis=1)
            dpre, G["ssd_conv_w"][j], G["ssd_conv_b"][j] = _conv_bwd_pre(proj, conv_w[j], ssd_conv_b[j:j + 1], dxbc_act, L + "d_conv_pre")
            dxbc = _conv_bwd_in(dpre, conv_w[j], L + "d_conv_in")
            ddt_all = jnp.sum(ddt, axis=0).astype(bf16)
            dproj = jnp.concatenate([dz, dxbc, dq_mem, ddt_all], axis=1)
        else:
            G["sb_w_out"][j] = g_out
            dqn, dkn, dv = _sb_bwd(st["qn"], st["kn"], proj, st["o"], dycat, L + "d_sb_attn")
            dq, dk, dqg2, dkg2 = _sb_qknorm_bwd(proj, sb_q_norm_g[j:j + 1], sb_k_norm_g[j:j + 1], dqn, dkn, L + "d_qk_norm")
            G["sb_q_norm_g"][j] = jnp.sum(dqg2, axis=0)
            G["sb_k_norm_g"][j] = jnp.sum(dkg2, axis=0)
            dproj = jnp.concatenate([dq, dk, dv.astype(bf16), dq_mem], axis=1)
        n_in = dproj.shape[1]
        dh = _matmul(dproj, st["w_in"], tb=True, tm=512, tn=D, tk=n_in, name=L + "d_h")
        g_in = _matmul(st["h"], dproj, ta=True, tm=256 if ssd else D, tn=n_in if ssd else 1024, tk=512, name=L + "dw_in")
        if ssd:
            G["ssd_w_in"][j] = _ssd_in_cols_back(g_in)
        else:
            G["sb_w_in"][j] = g_in
        dx, G["mix_norm_g"][i] = _rmsnorm_bwd(st["x_in"], mix_norm_g[i:i + 1], dh, dx, L + "d_mix_norm")

    grad_x = dx.reshape(x.shape)

    g_slots = [_slots_from_full(jnp.stack(G[n]), shp, ax).astype(bf16) for n, shp, ax in _BIG]
    pairs = _pair_exchange(g_slots, "exchange_grads_pair")
    core = lax.axis_index("c").astype(jnp.int32).reshape(1)
    chip_sums = [_pair_sum(g, r, core, "pair_sum_" + n) for g, r, (n, _, _) in zip(g_slots, pairs, _BIG)]
    parts = _chip_exchange(chip_sums, "exchange_grads_chip")
    out = {}
    for p, (n, shp, _) in zip(parts, _BIG):
        view = (shp[0] * shp[1], shp[2])
        res = _adamw_reduce(p, W[n].reshape(view), M[n].reshape(view), V[n].reshape(view), "adamw_" + n)
        out[n] = tuple(r.reshape(shp) for r in res)

    small = [n for n in names if n not in out and n != "ssd_conv_w"]
    G["mem_norm_g"] = d_mem_g.reshape(D)
    small_grads = [_lane_rows(G[n] if n == "mem_norm_g" else jnp.concatenate(G[n], axis=0)) for n in small]
    conv_grad = _lane_rows(jnp.stack(G["ssd_conv_w"]))
    sm_rows = [g.shape[0] for g in small_grads]
    n_small = sum(sm_rows)
    sm_total = -(-(n_small + conv_grad.shape[0]) // 8) * 8
    gathered = _allgather_vmem(_pack_rows(small_grads + [conv_grad], sm_total), "allgather_small_grads")
    whole = lambda s: pl.BlockSpec(s, lambda i: (0,) * len(s))
    g_sum = _fn_call(_sum_slots, (gathered,), [whole((N_DEV, sm_total, LANES))],
                     jax.ShapeDtypeStruct((sm_total, LANES), f32), whole((sm_total, LANES)), (1,), "sum_small_grads")
    conv_full = g_sum[n_small:n_small + 160].reshape(2, 4, SSD_CONV_DIM)
    conv_mine = lax.dynamic_slice_in_dim(conv_full, my_dev * 320, 320, axis=2)
    ad_total = n_small + 24
    pack = lambda d: _pack_rows([_lane_rows(d[n]) for n in small] + [_lane_rows(d["ssd_conv_w"])], ad_total)
    g_pack = _pack_rows([g_sum[:n_small], _lane_rows(conv_mine)], ad_total)
    blk = whole((ad_total, LANES))
    res_small = _fn_call(lambda g, w, m, v: _adamw_math(w, g, m, v), (g_pack, pack(W), pack(M), pack(V)), [blk] * 4,
                         (jax.ShapeDtypeStruct((ad_total, LANES), f32),) * 3, (blk,) * 3, (1,), "adamw_small")
    res_small = (g_pack,) + tuple(res_small)
    off = 0
    for n, rows in zip(small + ["ssd_conv_w"], sm_rows + [24]):
        size = W[n].size
        out[n] = tuple(r[off:off + rows].reshape(-1)[:size].reshape(W[n].shape) for r in res_small)
        off += rows

    return (loss, grad_x, *[out[n][0] for n in names], *[out[n][1] for n in names],
            *[out[n][2] for n in names], *[out[n][3] for n in names])
```

```python
import functools
import math

import jax
import jax.numpy as jnp
from jax import lax
from jax.experimental import pallas as pl
from jax.experimental.pallas import tpu as pltpu

f32, bf16 = jnp.float32, jnp.bfloat16
MESH = pl.DeviceIdType.MESH

N_DEV = 8
D = 1024
DEPTH = 4
EPS = 1e-6
MEM_LEN, MEM_HEADS, MEM_W, HD = 256, 4, 512, 128
SSD_INNER, SSD_HEADS, SSD_G, SSD_P, SSD_N, SSD_L = 1536, 24, 4, 64, 128, 128
SSD_GW = SSD_INNER // SSD_G
SSD_CONV_DIM = 2560
SSD_IN = 4632
SSD_IN_PAD = 4736
SB_W, SB_HEADS, SB_IN = 1536, 12, 5120
SB_BLK = 256
SB_SCALE = HD ** -0.5
SB_DEAD = 105.0
FFN_H = 2816
LANES = 128
VMEM_LIMIT = 56 * 1024 * 1024

ADAM_LR, ADAM_B1, ADAM_B2, ADAM_EPS, ADAM_WD, ADAM_STEP = 0.001, 0.9, 0.999, 1e-08, 0.01, 10

HIGHEST = lax.Precision.HIGHEST


def _params(n_grid):
    return pltpu.CompilerParams(dimension_semantics=("arbitrary",) * n_grid, vmem_limit_bytes=VMEM_LIMIT)


def _dg(a, b, ca, cb):
    return lax.dot_general(a.astype(bf16), b.astype(bf16), (((ca,), (cb,)), ((), ())), preferred_element_type=f32)


@jax.custom_vjp
def bdot_nn(a, b):
    return _dg(a, b, 1, 0)


def _nn_fwd(a, b):
    return _dg(a, b, 1, 0), (a, b)


def _nn_bwd(res, ct):
    a, b = res
    return _dg(ct, b, 1, 1).astype(a.dtype), _dg(a, ct, 0, 0).astype(b.dtype)


bdot_nn.defvjp(_nn_fwd, _nn_bwd)


@jax.custom_vjp
def bdot_nt(a, b):
    return _dg(a, b, 1, 1)


def _nt_fwd(a, b):
    return _dg(a, b, 1, 1), (a, b)


def _nt_bwd(res, ct):
    a, b = res
    return _dg(ct, b, 1, 0).astype(a.dtype), _dg(ct, a, 0, 0).astype(b.dtype)


bdot_nt.defvjp(_nt_fwd, _nt_bwd)


@jax.custom_vjp
def bdot_tn(a, b):
    return _dg(a, b, 0, 0)


def _tn_fwd(a, b):
    return _dg(a, b, 0, 0), (a, b)


def _tn_bwd(res, ct):
    a, b = res
    return _dg(b, ct, 1, 1).astype(a.dtype), _dg(a, ct, 1, 0).astype(b.dtype)


bdot_tn.defvjp(_tn_fwd, _tn_bwd)


def _rms(x, g):
    return x * lax.rsqrt(jnp.mean(x * x, axis=-1, keepdims=True) + EPS) * g


def _iota(shape, axis):
    return lax.broadcasted_iota(jnp.int32, shape, axis)


def _fn_call(fn, args, in_specs, out_shapes, out_specs, grid, name, acc=None):
    n_in = len(args)
    acc = acc or {}
    n_grid = len(grid)

    def body(*refs):
        ins, outs = refs[:n_in], refs[n_in:]
        res = fn(*[r[...] for r in ins])
        if not isinstance(res, (tuple, list)):
            res = (res,)
        for k, (o, r) in enumerate(zip(outs, res)):
            mode = acc.get(k)
            if mode is None:
                o[...] = r.astype(o.dtype)
                continue
            if mode == "last":
                first = pl.program_id(n_grid - 1) == 0
            else:
                first = functools.reduce(jnp.logical_and, [pl.program_id(d) == 0 for d in range(n_grid)])

            @pl.when(first)
            def _(o=o, r=r):
                o[...] = r.astype(o.dtype)

            @pl.when(jnp.logical_not(first))
            def _(o=o, r=r):
                o[...] += r.astype(o.dtype)

    return pl.pallas_call(
        body, grid=grid, in_specs=in_specs, out_specs=out_specs, out_shape=out_shapes, name=name,
        compiler_params=_params(n_grid))(*args)


def _matmul(a, b, *, ta=False, tb=False, out_dtype=f32, tm, tn, tk, res=None, name):
    M, K = (a.shape[1], a.shape[0]) if ta else a.shape
    N = b.shape[0] if tb else b.shape[1]
    tm, tn, tk = min(tm, M), min(tn, N), min(tk, K)
    assert M % tm == 0 and N % tn == 0 and K % tk == 0, (name, M, N, K, tm, tn, tk)
    nk = K // tk
    a_spec = pl.BlockSpec((tk, tm), lambda i, j, k: (k, i)) if ta else pl.BlockSpec((tm, tk), lambda i, j, k: (i, k))
    b_spec = pl.BlockSpec((tn, tk), lambda i, j, k: (j, k)) if tb else pl.BlockSpec((tk, tn), lambda i, j, k: (k, j))
    o_spec = pl.BlockSpec((tm, tn), lambda i, j, k: (i, j))
    ca, cb = (0 if ta else 1), (1 if tb else 0)

    def body(*refs):
        a_ref, b_ref = refs[:2]
        r_ref = None if res is None else refs[2]
        o_ref = refs[2 if res is None else 3]
        part = _dg(a_ref[...], b_ref[...], ca, cb)

        def finish(out):
            if r_ref is not None:
                out = out + r_ref[...].astype(f32)
            o_ref[...] = out.astype(o_ref.dtype)

        if nk == 1:
            finish(part)
            return
        acc_ref = refs[-1]
        k = pl.program_id(2)

        @pl.when(k == 0)
        def _():
            acc_ref[...] = part

        @pl.when(k > 0)
        def _():
            acc_ref[...] += part

        @pl.when(k == nk - 1)
        def _():
            finish(acc_ref[...])

    args = (a, b) if res is None else (a, b, res)
    in_specs = [a_spec, b_spec] + ([] if res is None else [o_spec])
    return pl.pallas_call(
        body, grid=(M // tm, N // tn, nk), in_specs=in_specs, out_specs=o_spec,
        out_shape=jax.ShapeDtypeStruct((M, N), out_dtype), name=name,
        scratch_shapes=[] if nk == 1 else [pltpu.VMEM((tm, tn), f32)], compiler_params=_params(3))(*args)


def _row_tile(T):
    return min(T, 512)


def _my_pos():
    return lax.axis_index("x"), lax.axis_index("y"), lax.axis_index("c")


def _allgather_hbm(xs, name):
    R, C = xs.shape

    def body(x_ref, out_ref, send_sems, recv_sems, local_sem):
        x, y, c = _my_pos()
        me, sibling = (x, y, c), (x, y, 1 - c)
        chips = [(1 - x, y), (x, 1 - y), (1 - x, 1 - y)]

        def slot(px, py, pc):
            return out_ref.at[4 * px + 2 * py + pc]

        def copy(k, block, to, src=None):
            return pltpu.make_async_remote_copy(
                src_ref=slot(*block) if src is None else src, dst_ref=slot(*block),
                send_sem=send_sems.at[k], recv_sem=recv_sems.at[k], device_id=to, device_id_type=MESH)

        mine = pltpu.make_async_copy(x_ref, slot(*me), local_sem)
        mine.start()
        first = [copy(0, me, sibling, src=x_ref)]
        first += [copy(1 + j, me, (*chip, c), src=x_ref) for j, chip in enumerate(chips)]
        for cp in first:
            cp.start()
        passed = [copy(4 + j, (*chip, c), sibling) for j, chip in enumerate(chips)]
        for j, chip in enumerate(chips):
            copy(1 + j, (*chip, c), me).wait_recv()
            passed[j].start()
        copy(0, sibling, me).wait_recv()
        for j, chip in enumerate(chips):
            copy(4 + j, (*chip, 1 - c), me).wait_recv()
        for cp in first + passed:
            cp.wait_send()
        mine.wait()

    return pl.pallas_call(
        body, out_shape=jax.ShapeDtypeStruct((N_DEV, R, C), xs.dtype),
        in_specs=[pl.BlockSpec(memory_space=pl.ANY)], out_specs=pl.BlockSpec(memory_space=pl.ANY),
        scratch_shapes=[pltpu.SemaphoreType.DMA((7,)), pltpu.SemaphoreType.DMA((7,)), pltpu.SemaphoreType.DMA],
        name=name)(xs)


def _allgather_vmem(xs, name):
    R, C = xs.shape

    def body(x_ref, out_ref, send_sems, recv_sems):
        x, y, c = _my_pos()
        me = 4 * x + 2 * y + c
        out_ref[me] = x_ref[...]
        copies = []
        for k in range(1, N_DEV):
            px = 1 - x if k & 4 else x
            py = 1 - y if k & 2 else y
            pc = 1 - c if k & 1 else c
            cp = pltpu.make_async_remote_copy(
                src_ref=x_ref, dst_ref=out_ref.at[me], send_sem=send_sems.at[k - 1], recv_sem=recv_sems.at[k - 1],
                device_id=(px, py, pc), device_id_type=MESH)
            cp.start()
            copies.append(cp)
        for cp in copies:
            cp.wait()

    return pl.pallas_call(
        body, out_shape=jax.ShapeDtypeStruct((N_DEV, R, C), xs.dtype),
        in_specs=[pl.BlockSpec(memory_space=pltpu.VMEM)], out_specs=pl.BlockSpec(memory_space=pltpu.VMEM),
        scratch_shapes=[pltpu.SemaphoreType.DMA((7,)), pltpu.SemaphoreType.DMA((7,))], name=name)(xs)


N_CHIP = 4


def _pair_exchange(gs, name):
    n = len(gs)

    def body(*refs):
        g_refs, out_refs = refs[:n], refs[n:2 * n]
        send_sems, recv_sems = refs[2 * n:]
        x, y, c = _my_pos()
        copies = []
        for a in range(n):
            for k in range(N_CHIP):
                cp = pltpu.make_async_remote_copy(
                    src_ref=g_refs[a].at[4 * (k >> 1) + 2 * (k & 1) + 1 - c], dst_ref=out_refs[a].at[k],
                    send_sem=send_sems.at[a, k], recv_sem=recv_sems.at[a, k],
                    device_id=(x, y, 1 - c), device_id_type=MESH)
                cp.start()
                copies.append(cp)
        for cp in copies:
            cp.wait()

    hbm = pl.BlockSpec(memory_space=pl.ANY)
    return pl.pallas_call(
        body, out_shape=[jax.ShapeDtypeStruct((N_CHIP,) + g.shape[1:], g.dtype) for g in gs],
        in_specs=[hbm] * n, out_specs=[hbm] * n,
        scratch_shapes=[pltpu.SemaphoreType.DMA((n, N_CHIP))] * 2, name=name)(*gs)


def _chip_exchange(cs, name):
    n = len(cs)

    def body(*refs):
        c_refs, out_refs = refs[:n], refs[n:2 * n]
        send_sems, recv_sems, local_sems = refs[2 * n:]
        x, y, c = _my_pos()
        my_chip = 2 * x + y
        copies = []
        for a in range(n):
            cp = pltpu.make_async_copy(c_refs[a].at[my_chip], out_refs[a].at[my_chip], local_sems.at[a])
            cp.start()
            copies.append(cp)
            for rel in range(1, N_CHIP):
                px = 1 - x if rel & 2 else x
                py = 1 - y if rel & 1 else y
                cp = pltpu.make_async_remote_copy(
                    src_ref=c_refs[a].at[2 * px + py], dst_ref=out_refs[a].at[my_chip],
                    send_sem=send_sems.at[a, rel - 1], recv_sem=recv_sems.at[a, rel - 1],
                    device_id=(px, py, c), device_id_type=MESH)
                cp.start()
                copies.append(cp)
        for cp in copies:
            cp.wait()

    hbm = pl.BlockSpec(memory_space=pl.ANY)
    return pl.pallas_call(
        body, out_shape=[jax.ShapeDtypeStruct(g.shape, g.dtype) for g in cs],
        in_specs=[hbm] * n, out_specs=[hbm] * n,
        scratch_shapes=[pltpu.SemaphoreType.DMA((n, N_CHIP - 1)), pltpu.SemaphoreType.DMA((n, N_CHIP - 1)),
                        pltpu.SemaphoreType.DMA((n,))], name=name)(*cs)


def _shard_row_tile(rows):
    return next(t for t in (256, 128) if rows % t == 0)


def _pair_sum(g, recv, core, name):
    _, rows, cols = g.shape
    tr = _shard_row_tile(rows)

    def body(core_ref, g0, g1, g2, g3, recv_ref, o_ref):
        for k, g_ref in enumerate((g0, g1, g2, g3)):
            o_ref[k] = (g_ref[...].astype(f32) + recv_ref[k].astype(f32)).astype(o_ref.dtype)

    mine = [pl.BlockSpec((None, tr, cols), lambda i, core, k=k: (4 * (k >> 1) + 2 * (k & 1) + core[0], i, 0))
            for k in range(N_CHIP)]
    four = pl.BlockSpec((N_CHIP, tr, cols), lambda i, core: (0, i, 0))
    return pl.pallas_call(
        body, grid_spec=pltpu.PrefetchScalarGridSpec(
            num_scalar_prefetch=1, grid=(rows // tr,), in_specs=mine + [four], out_specs=four),
        out_shape=jax.ShapeDtypeStruct((N_CHIP, rows, cols), g.dtype), name=name,
        compiler_params=_params(1))(core, g, g, g, g, recv)


def _adamw_math(w, g, m, v):
    m = ADAM_B1 * m + (1.0 - ADAM_B1) * g
    v = ADAM_B2 * v + (1.0 - ADAM_B2) * jnp.square(g)
    m_hat = m / (1.0 - ADAM_B1 ** ADAM_STEP)
    v_hat = v / (1.0 - ADAM_B2 ** ADAM_STEP)
    delta = -ADAM_LR * (m_hat / (jnp.sqrt(v_hat) + ADAM_EPS) + ADAM_WD * w)
    return delta, m, v


def _sum_slots(parts):
    g = parts[0].astype(f32)
    for i in range(1, parts.shape[0]):
        g = g + parts[i].astype(f32)
    return g


def _adamw_reduce(parts, w, m, v, name):
    rows, cols = w.shape
    tr = _shard_row_tile(rows)

    def fn(p, w, m, v):
        g = _sum_slots(p)
        return (g,) + _adamw_math(w, g, m, v)

    row = pl.BlockSpec((tr, cols), lambda i: (i, 0))
    sds = jax.ShapeDtypeStruct((rows, cols), f32)
    return _fn_call(fn, (parts, w, m, v), [pl.BlockSpec((N_CHIP, tr, cols), lambda i: (0, i, 0)), row, row, row],
                    (sds,) * 4, (row,) * 4, (rows // tr,), name)


def _rmsnorm_fwd(x, g, name):
    T = x.shape[0]
    tm = _row_tile(T)
    row = pl.BlockSpec((tm, D), lambda i: (i, 0))
    par = pl.BlockSpec((1, D), lambda i: (0, 0))
    return _fn_call(lambda x, g: _rms(x, g), (x, g), [row, par], jax.ShapeDtypeStruct((T, D), bf16), row, (T // tm,), name)


def _rmsnorm_bwd(x, g, dh, dres, name):
    T = x.shape[0]
    tm = _row_tile(T)
    row = pl.BlockSpec((tm, D), lambda i: (i, 0))
    par = pl.BlockSpec((1, D), lambda i: (0, 0))

    def fn(x, g, dh, dres):
        _, vjp = jax.vjp(_rms, x, g)
        dx, dg = vjp(dh.astype(f32))
        return dx + dres, dg

    return _fn_call(fn, (x, g, dh, dres), [row, par, row, row],
                    (jax.ShapeDtypeStruct((T, D), f32), jax.ShapeDtypeStruct((1, D), f32)), (row, par),
                    (T // tm,), name, acc={1: "all"})


def _swiglu_act(g, u):
    return jax.nn.silu(g) * u


def _ffn_up_act(h2, w_gu, name):
    T = h2.shape[0]
    tm = min(T, 256)

    def body(h_ref, w_ref, gu_ref, act_ref):
        gu = _dg(h_ref[...], w_ref[...], 1, 0)
        gu_ref[...] = gu
        act_ref[...] = _swiglu_act(gu[:, :FFN_H], gu[:, FFN_H:]).astype(act_ref.dtype)

    return pl.pallas_call(
        body, grid=(T // tm,),
        in_specs=[pl.BlockSpec((tm, D), lambda i: (i, 0)), pl.BlockSpec((D, 2 * FFN_H), lambda i: (0, 0))],
        out_specs=(pl.BlockSpec((tm, 2 * FFN_H), lambda i: (i, 0)), pl.BlockSpec((tm, FFN_H), lambda i: (i, 0))),
        out_shape=(jax.ShapeDtypeStruct((T, 2 * FFN_H), f32), jax.ShapeDtypeStruct((T, FFN_H), bf16)),
        name=name, compiler_params=_params(1))(h2, w_gu)


def _ffn_dgu(dx, w_down, gu, name):
    T = dx.shape[0]
    tm = min(T, 256)

    def body(dx_ref, w_ref, gu_ref, o_ref):
        dact = _dg(dx_ref[...], w_ref[...], 1, 1)
        gu = gu_ref[...]
        _, vjp = jax.vjp(_swiglu_act, gu[:, :FFN_H], gu[:, FFN_H:])
        dg, du = vjp(dact)
        o_ref[:, :FFN_H] = dg.astype(o_ref.dtype)
        o_ref[:, FFN_H:] = du.astype(o_ref.dtype)

    return pl.pallas_call(
        body, grid=(T // tm,),
        in_specs=[pl.BlockSpec((tm, D), lambda i: (i, 0)), pl.BlockSpec((FFN_H, D), lambda i: (0, 0)),
                  pl.BlockSpec((tm, 2 * FFN_H), lambda i: (i, 0))],
        out_specs=pl.BlockSpec((tm, 2 * FFN_H), lambda i: (i, 0)),
        out_shape=jax.ShapeDtypeStruct((T, 2 * FFN_H), bf16), name=name, compiler_params=_params(1))(dx, w_down, gu)


def _loss_head(x, target, name):
    T = x.shape[0]
    tm = _row_tile(T)
    row = pl.BlockSpec((tm, D), lambda i: (i, 0))
    par = pl.BlockSpec((1, LANES), lambda i: (0, 0))

    def fn(x, t):
        e = x - t
        s = jnp.sum(e * e, axis=0, keepdims=True)
        part = s[:, 0:LANES]
        for k in range(1, D // LANES):
            part = part + s[:, k * LANES:(k + 1) * LANES]
        return e * (1.0 / D), part * (0.5 / D)

    return _fn_call(fn, (x, target), [row, row],
                    (jax.ShapeDtypeStruct((T, D), f32), jax.ShapeDtypeStruct((1, LANES), f32)), (row, par),
                    (T // tm,), name, acc={1: "all"})


def _memkv_fn(mem, mg, wkv, kg):
    mn = _rms(mem, mg)
    kv = bdot_nn(mn, wkv)
    ks = [_rms(kv[:, h * HD:(h + 1) * HD], kg) for h in range(MEM_HEADS)]
    return jnp.concatenate(ks, axis=1), kv[:, MEM_W:]


def _memkv_fwd(mem, mg, wkv, kg, name):
    whole = lambda s: pl.BlockSpec(s, lambda i: (0,) * len(s))
    sds = jax.ShapeDtypeStruct((MEM_LEN, MEM_W), f32)
    return _fn_call(lambda m, g, w, k: _memkv_fn(m, g, w.astype(f32), k), (mem, mg, wkv, kg),
                    [whole((MEM_LEN, D)), whole((1, D)), whole((D, 2 * MEM_W)), whole((1, HD))],
                    (sds, sds), (whole((MEM_LEN, MEM_W)),) * 2, (1,), name)


def _memkv_bwd(mem, mg, wkv, kg, dk, dv, name):
    whole = lambda s: pl.BlockSpec(s, lambda i: (0,) * len(s))

    def fn(m, g, w, k, dk, dv):
        _, vjp = jax.vjp(lambda g, w, k: _memkv_fn(m, g, w, k), g, w.astype(f32), k)
        return vjp((dk, dv))

    return _fn_call(fn, (mem, mg, wkv, kg, dk, dv),
                    [whole((MEM_LEN, D)), whole((1, D)), whole((D, 2 * MEM_W)), whole((1, HD)),
                     whole((MEM_LEN, MEM_W)), whole((MEM_LEN, MEM_W))],
                    (jax.ShapeDtypeStruct((1, D), f32), jax.ShapeDtypeStruct((D, 2 * MEM_W), f32),
                     jax.ShapeDtypeStruct((1, HD), f32)),
                    (whole((1, D)), whole((D, 2 * MEM_W)), whole((1, HD))), (1,), name)


def _memattn_fn(q, k, v, qg):
    qn = _rms(q, qg)
    s = bdot_nt(qn, k) * (HD ** -0.5)
    s = s - jnp.max(s, axis=-1, keepdims=True)
    p = jnp.exp(s)
    p = p / jnp.sum(p, axis=-1, keepdims=True)
    return bdot_nn(p, v)


def _per_head(fn, n_heads, head_args, *shared):
    outs = [fn(*[a[:, h * HD:(h + 1) * HD] for a in head_args], *shared) for h in range(n_heads)]
    if isinstance(outs[0], tuple):
        return tuple(jnp.concatenate(o, axis=1) for o in zip(*outs))
    return jnp.concatenate(outs, axis=1)


def _memattn_all(q, k, v, qg):
    return _per_head(_memattn_fn, MEM_HEADS, (q, k, v), qg)


def _memattn_fwd(proj, q_col, k, v, qg, name):
    T = proj.shape[0]
    tm = _row_tile(T)
    kv_spec = pl.BlockSpec((MEM_LEN, MEM_W), lambda i: (0, 0))
    return _fn_call(_memattn_all, (proj, k, v, qg),
                    [pl.BlockSpec((tm, MEM_W), lambda i: (i, q_col)), kv_spec, kv_spec, pl.BlockSpec((1, HD), lambda i: (0, 0))],
                    jax.ShapeDtypeStruct((T, MEM_W), bf16), pl.BlockSpec((tm, MEM_W), lambda i: (i, 0)), (T // tm,), name)


def _memattn_bwd(proj, q_col, k, v, qg, dycat, do_col, name):
    T = proj.shape[0]
    tm = _row_tile(T)

    def fn(q, k, v, qg, do):
        _, vjp = jax.vjp(_memattn_all, q, k, v, qg)
        return vjp(do.astype(f32))

    kv_spec = pl.BlockSpec((MEM_LEN, MEM_W), lambda i: (0, 0))
    kv_sds = jax.ShapeDtypeStruct((MEM_LEN, MEM_W), f32)
    par = pl.BlockSpec((1, HD), lambda i: (0, 0))
    return _fn_call(fn, (proj, k, v, qg, dycat),
                    [pl.BlockSpec((tm, MEM_W), lambda i: (i, q_col)), kv_spec, kv_spec, par,
                     pl.BlockSpec((tm, MEM_W), lambda i: (i, do_col))],
                    (jax.ShapeDtypeStruct((T, MEM_W), bf16), kv_sds, kv_sds, jax.ShapeDtypeStruct((1, HD), f32)),
                    (pl.BlockSpec((tm, MEM_W), lambda i: (i, 0)), kv_spec, kv_spec, par),
                    (T // tm,), name, acc={1: "all", 2: "all", 3: "all"})


def _conv_taps(xp, w, first, tm):
    out = w[0:1, :] * xp[first:first + tm, :]
    for k in range(1, 4):
        out = out + w[k:k + 1, :] * xp[first + k:first + k + tm, :]
    return out


def _conv_blocks(T):
    tm, tc = _row_tile(T), 512
    nt = T // tm
    cur = pl.BlockSpec((tm, tc), lambda j, i: (i, 3 + j))
    prev = pl.BlockSpec((8, tc), lambda j, i: (jnp.maximum(i * (tm // 8) - 1, 0), 3 + j))
    par4 = pl.BlockSpec((4, tc), lambda j, i: (0, j))
    par1 = pl.BlockSpec((1, tc), lambda j, i: (0, j))
    out = pl.BlockSpec((tm, tc), lambda j, i: (i, j))
    return tm, tc, nt, cur, prev, par4, par1, out


def _conv_fwd(proj, w, b, name):
    T = proj.shape[0]
    tm, tc, nt, cur, prev, par4, par1, out = _conv_blocks(T)

    def body(prev_ref, cur_ref, w_ref, b_ref, o_ref):
        halo = jnp.where(pl.program_id(1) == 0, 0.0, prev_ref[...])
        xp = jnp.concatenate([halo, cur_ref[...]], axis=0)
        o_ref[...] = jax.nn.silu(_conv_taps(xp, w_ref[...], 5, tm) + b_ref[...])

    return pl.pallas_call(body, grid=(SSD_CONV_DIM // tc, nt), in_specs=[prev, cur, par4, par1], out_specs=out,
                          out_shape=jax.ShapeDtypeStruct((T, SSD_CONV_DIM), f32), name=name,
                          compiler_params=_params(2))(proj, proj, w, b)


def _conv_bwd_pre(proj, w, b, dact, name):
    T = proj.shape[0]
    tm, tc, nt, cur, prev, par4, par1, out = _conv_blocks(T)

    def body(prev_ref, cur_ref, w_ref, b_ref, da_ref, dp_ref, dw_ref, db_ref):
        i = pl.program_id(1)
        halo = jnp.where(i == 0, 0.0, prev_ref[...])
        xp = jnp.concatenate([halo, cur_ref[...]], axis=0)
        pre = _conv_taps(xp, w_ref[...], 5, tm) + b_ref[...]
        sig = jax.nn.sigmoid(pre)
        dpre = da_ref[...] * (sig * (1.0 + pre * (1.0 - sig)))
        dp_ref[...] = dpre
        dw = jnp.concatenate([jnp.sum(dpre * xp[5 + k:5 + k + tm, :], axis=0, keepdims=True) for k in range(4)], axis=0)
        db = jnp.sum(dpre, axis=0, keepdims=True)

        @pl.when(i == 0)
        def _():
            dw_ref[...] = dw
            db_ref[...] = db

        @pl.when(i > 0)
        def _():
            dw_ref[...] += dw
            db_ref[...] += db

    return pl.pallas_call(
        body, grid=(SSD_CONV_DIM // tc, nt), in_specs=[prev, cur, par4, par1, out], out_specs=(out, par4, par1),
        out_shape=(jax.ShapeDtypeStruct((T, SSD_CONV_DIM), f32), jax.ShapeDtypeStruct((4, SSD_CONV_DIM), f32),
                   jax.ShapeDtypeStruct((1, SSD_CONV_DIM), f32)),
        name=name, compiler_params=_params(2))(proj, proj, w, b, dact)


def _conv_bwd_in(dpre, w, name):
    T = dpre.shape[0]
    tm, tc, nt, _, _, par4, _, out = _conv_blocks(T)
    nxt = pl.BlockSpec((8, tc), lambda j, i: (jnp.minimum((i + 1) * (tm // 8), T // 8 - 1), j))

    def body(cur_ref, nxt_ref, w_ref, o_ref):
        halo = jnp.where(pl.program_id(1) == nt - 1, 0.0, nxt_ref[...])
        xp = jnp.concatenate([cur_ref[...], halo], axis=0)
        w = w_ref[...]
        acc = w[3:4, :] * xp[0:tm, :]
        for k in range(3):
            acc = acc + w[k:k + 1, :] * xp[3 - k:3 - k + tm, :]
        o_ref[...] = acc.astype(o_ref.dtype)

    return pl.pallas_call(body, grid=(SSD_CONV_DIM // tc, nt), in_specs=[out, nxt, par4], out_specs=out,
                          out_shape=jax.ShapeDtypeStruct((T, SSD_CONV_DIM), bf16), name=name,
                          compiler_params=_params(2))(dpre, dpre, w)


def _ssd_chunk(hbase, xs, bm, cm, z, dtr, dtb, alog, dsk, ng, ht):
    L = SSD_L
    dt = jax.nn.softplus(dtr + dtb)
    da = dt * (-jnp.exp(alog))
    li, si = _iota((L, L), 0), _iota((L, L), 1)
    causal = li >= si
    cs = jnp.dot(causal.astype(f32), da, precision=HIGHEST, preferred_element_type=f32)
    cs_t = cs.T
    chan_head = _iota((1, SSD_GW), 1) // SSD_P
    heads = range(SSD_GW // SSD_P)
    lane_of = [(_iota((1, LANES), 1) == hbase + r).astype(f32) for r in heads]
    cs_cols = [jnp.sum(cs * lane_of[r], axis=1, keepdims=True) for r in heads]
    dt_cols = [jnp.sum(dt * lane_of[r], axis=1, keepdims=True) for r in heads]
    cs_e = jnp.zeros((L, SSD_GW), f32)
    dt_e = jnp.zeros((L, SSD_GW), f32)
    for r in heads:
        cs_e = jnp.where(chan_head == r, cs_cols[r], cs_e)
        dt_e = jnp.where(chan_head == r, dt_cols[r], dt_e)
    xdt = xs * dt_e
    cb = bdot_nt(cm, bm)
    y = jnp.zeros((L, SSD_GW), f32)
    for r in heads:
        cs_row = jnp.sum(cs_t * (_iota((LANES, 1), 0) == hbase + r).astype(f32), axis=0, keepdims=True)
        decay = jnp.where(causal, jnp.exp(jnp.where(causal, cs_cols[r] - cs_row, 0.0)), 0.0)
        y = y + bdot_nn(cb * decay, xdt * (chan_head == r).astype(f32))
    y = y + jnp.exp(cs_e) * bdot_nn(cm, ht)
    cs_last = jnp.sum(cs_e * (_iota((L, 1), 0) == L - 1).astype(f32), axis=0, keepdims=True)
    ht_new = ht * jnp.exp(cs_last) + bdot_tn(bm, xdt * jnp.exp(cs_last - cs_e))
    y = (y + dsk * xs) * jax.nn.silu(z)
    return _rms(y, ng), ht_new


def _ssd_specs(T, rev):
    nc = T // SSD_L
    cidx = (lambda c: nc - 1 - c) if rev else (lambda c: c)
    return nc, dict(
        xs=pl.BlockSpec((SSD_L, SSD_GW), lambda g, c: (cidx(c), g)),
        bm=pl.BlockSpec((SSD_L, SSD_N), lambda g, c: (cidx(c), 12 + g)),
        cm=pl.BlockSpec((SSD_L, SSD_N), lambda g, c: (cidx(c), 16 + g)),
        z=pl.BlockSpec((SSD_L, SSD_GW), lambda g, c: (cidx(c), g)),
        dt=pl.BlockSpec((SSD_L, LANES), lambda g, c: (cidx(c), 36)),
        p128=pl.BlockSpec((1, LANES), lambda g, c: (0, 0)),
        pgw=pl.BlockSpec((1, SSD_GW), lambda g, c: (0, g)),
        hs=pl.BlockSpec((None, None, SSD_N, SSD_GW), lambda g, c: (g, cidx(c), 0, 0)),
        grp=pl.BlockSpec((SSD_L, SSD_N), lambda g, c: (cidx(c), g)),
    )


def _ssd_fwd(xbc, proj, dtb, alog, dsk, ng, name):
    T = proj.shape[0]
    nc, s = _ssd_specs(T, False)

    def body(xs_ref, bm_ref, cm_ref, z_ref, dt_ref, dtb_ref, alog_ref, dsk_ref, ng_ref, y_ref, hs_ref, h_scr):
        @pl.when(pl.program_id(1) == 0)
        def _():
            h_scr[...] = jnp.zeros_like(h_scr)

        ht = h_scr[...]
        hs_ref[...] = ht
        y, ht_new = _ssd_chunk(pl.program_id(0) * (SSD_GW // SSD_P), xs_ref[...], bm_ref[...], cm_ref[...], z_ref[...],
                               dt_ref[...], dtb_ref[...], alog_ref[...], dsk_ref[...], ng_ref[...], ht)
        y_ref[...] = y.astype(y_ref.dtype)
        h_scr[...] = ht_new

    return pl.pallas_call(
        body, grid=(SSD_G, nc),
        in_specs=[s["xs"], s["bm"], s["cm"], s["z"], s["dt"], s["p128"], s["p128"], s["pgw"], s["pgw"]],
        out_specs=(s["xs"], s["hs"]),
        out_shape=(jax.ShapeDtypeStruct((T, SSD_INNER), bf16), jax.ShapeDtypeStruct((SSD_G, nc, SSD_N, SSD_GW), f32)),
        scratch_shapes=[pltpu.VMEM((SSD_N, SSD_GW), f32)], name=name, compiler_params=_params(2))(
            xbc, xbc, xbc, proj, proj, dtb, alog, dsk, ng)


def _ssd_bwd(xbc, proj, dtb, alog, dsk, ng, hs, dycat, name):
    T = proj.shape[0]
    nc, s = _ssd_specs(T, True)

    def body(xs_ref, bm_ref, cm_ref, z_ref, dt_ref, dtb_ref, alog_ref, dsk_ref, ng_ref, hs_ref, dy_ref,
             dxs_ref, dbm_ref, dcm_ref, dz_ref, ddt_ref, ddtb_ref, dalog_ref, ddsk_ref, dng_ref, dh_scr):
        c = pl.program_id(1)

        @pl.when(c == 0)
        def _():
            dh_scr[...] = jnp.zeros_like(dh_scr)

        hbase = pl.program_id(0) * (SSD_GW // SSD_P)
        _, vjp = jax.vjp(functools.partial(_ssd_chunk, hbase), xs_ref[...], bm_ref[...], cm_ref[...], z_ref[...],
                         dt_ref[...], dtb_ref[...], alog_ref[...], dsk_ref[...], ng_ref[...], hs_ref[...])
        dxs, dbm, dcm, dz, ddt, ddtb, dalog, ddsk, dng, dht = vjp((dy_ref[...].astype(f32), dh_scr[...]))
        dxs_ref[...] = dxs
        dbm_ref[...] = dbm
        dcm_ref[...] = dcm
        dz_ref[...] = dz.astype(dz_ref.dtype)
        ddt_ref[...] = ddt
        dh_scr[...] = dht

        @pl.when(c == 0)
        def _():
            ddtb_ref[...] = ddtb
            dalog_ref[...] = dalog
            ddsk_ref[...] = ddsk
            dng_ref[...] = dng

        @pl.when(c > 0)
        def _():
            ddtb_ref[...] += ddtb
            dalog_ref[...] += dalog
            ddsk_ref[...] += ddsk
            dng_ref[...] += dng

    cidx = lambda c: nc - 1 - c
    g128 = pl.BlockSpec((None, 1, LANES), lambda g, c: (g, 0, 0))
    return pl.pallas_call(
        body, grid=(SSD_G, nc),
        in_specs=[s["xs"], s["bm"], s["cm"], s["z"], s["dt"], s["p128"], s["p128"], s["pgw"], s["pgw"], s["hs"], s["xs"]],
        out_specs=(s["xs"], s["grp"], s["grp"], s["xs"],
                   pl.BlockSpec((None, SSD_L, LANES), lambda g, c: (g, cidx(c), 0)), g128, g128, s["pgw"], s["pgw"]),
        out_shape=(jax.ShapeDtypeStruct((T, SSD_INNER), f32), jax.ShapeDtypeStruct((T, SSD_G * SSD_N), f32),
                   jax.ShapeDtypeStruct((T, SSD_G * SSD_N), f32), jax.ShapeDtypeStruct((T, SSD_INNER), bf16),
                   jax.ShapeDtypeStruct((SSD_G, T, LANES), f32), jax.ShapeDtypeStruct((SSD_G, 1, LANES), f32),
                   jax.ShapeDtypeStruct((SSD_G, 1, LANES), f32), jax.ShapeDtypeStruct((1, SSD_INNER), f32),
                   jax.ShapeDtypeStruct((1, SSD_INNER), f32)),
        scratch_shapes=[pltpu.VMEM((SSD_N, SSD_GW), f32)], name=name, compiler_params=_params(2))(
            xbc, xbc, xbc, proj, proj, dtb, alog, dsk, ng, hs, dycat)


def _qk_norm_fn(q, k, qg, kg):
    return _per_head(_rms, SB_HEADS, (q,), qg), _per_head(_rms, SB_HEADS, (k,), kg)


def _sb_qknorm_fwd(proj, qg, kg, name):
    T = proj.shape[0]
    tm = min(T, 256)
    par = pl.BlockSpec((1, HD), lambda i: (0, 0))
    out = pl.BlockSpec((tm, SB_W), lambda i: (i, 0))
    sds = jax.ShapeDtypeStruct((T, SB_W), bf16)
    return _fn_call(_qk_norm_fn, (proj, proj, qg, kg), [out, pl.BlockSpec((tm, SB_W), lambda i: (i, 1)), par, par],
                    (sds, sds), (out, out), (T // tm,), name)


def _sb_qknorm_bwd(proj, qg, kg, dqn, dkn, name):
    T = proj.shape[0]
    tm = min(T, 256)
    par = pl.BlockSpec((1, HD), lambda i: (0, 0))
    out = pl.BlockSpec((tm, SB_W), lambda i: (i, 0))

    def fn(q, k, qg, kg, dqn, dkn):
        _, vjp = jax.vjp(_qk_norm_fn, q, k, qg, kg)
        return vjp((dqn, dkn))

    sds = jax.ShapeDtypeStruct((T, SB_W), bf16)
    gsds = jax.ShapeDtypeStruct((1, HD), f32)
    return _fn_call(fn, (proj, proj, qg, kg, dqn, dkn), [out, pl.BlockSpec((tm, SB_W), lambda i: (i, 1)), par, par, out, out],
                    (sds, sds, gsds, gsds), (out, out, par, par), (T // tm,), name, acc={2: "all", 3: "all"})


def _split_dot(a, tri):
    hi = a.astype(bf16)
    lo = (a - hi.astype(f32)).astype(bf16)
    return jnp.dot(hi, tri, preferred_element_type=f32) + jnp.dot(lo, tri, preferred_element_type=f32)


def _sb_weights(q, kblk, run, later, mask):
    z = _dg(q, kblk, 1, 1) * SB_SCALE
    t = jnp.log(1.0 + jnp.exp(-jnp.abs(z)))
    sp = jnp.maximum(z, 0.0) + t
    log_beta = jnp.minimum(z, 0.0) - t
    if mask is not None:
        sp = jnp.where(mask, sp, 0.0)
    w = jnp.exp(log_beta - _split_dot(sp, later) - run)
    if mask is not None:
        w = jnp.where(mask, w, 0.0)
    return jnp.exp(log_beta), sp, w


def _sb_older_blocks(qb, carry, step, run_of):
    def cond(state):
        i, cr = state
        return jnp.logical_and(i < qb, jnp.min(run_of(cr)) < SB_DEAD)

    def body(state):
        i, cr = state
        return i + 1, step(qb - 1 - i, cr)

    return lax.while_loop(cond, body, (jnp.int32(0), carry))[1]


def _sb_fwd(qn, kn, proj, name):
    T = qn.shape[0]
    B = min(SB_BLK, T)
    nq = T // B

    def body(q_ref, k_ref, v_ref, o_ref, ox_ref):
        qb = pl.program_id(1)
        q = q_ref[...]
        ri, ci = _iota((B, B), 0), _iota((B, B), 1)
        later = (ri > ci).astype(bf16)

        def block(kb, carry, mask):
            acc, acc_lo, run = carry
            off = pl.multiple_of(kb * B, B)
            _, sp, w = _sb_weights(q, k_ref[pl.ds(off, B), :], run, later, mask)
            vblk = v_ref[pl.ds(off, B), :]
            w_hi = w.astype(bf16)
            acc = acc + _dg(w_hi, vblk, 1, 0)
            acc_lo = acc_lo + _dg(w - w_hi.astype(f32), vblk, 1, 0)
            return acc, acc_lo, run + jnp.sum(sp, axis=1, keepdims=True)

        zero = jnp.zeros((B, HD), f32)
        carry = block(qb, (zero, zero, jnp.zeros((B, 1), f32)), ci < ri)
        carry = _sb_older_blocks(qb, carry, lambda kb, cr: block(kb, cr, None), lambda cr: cr[2])
        o_ref[...] = carry[0].astype(o_ref.dtype)
        ox_ref[...] = carry[0] + carry[1]

    blk = pl.BlockSpec((B, HD), lambda h, i: (i, h))
    return pl.pallas_call(
        body, grid=(SB_HEADS, nq),
        in_specs=[blk, pl.BlockSpec((T, HD), lambda h, i: (0, h)), pl.BlockSpec((T, HD), lambda h, i: (0, 2 * SB_HEADS + h))],
        out_specs=(blk, blk), out_shape=(jax.ShapeDtypeStruct((T, SB_W), bf16), jax.ShapeDtypeStruct((T, SB_W), f32)),
        name=name, compiler_params=_params(2))(qn, kn, proj)


def _sb_bwd(qn, kn, proj, o, dycat, name):
    T = qn.shape[0]
    B = min(SB_BLK, T)
    nq = T // B

    def body(q_ref, k_ref, v_ref, o_ref, do_ref, dq_ref, dk_ref, dv_ref):
        qb = pl.program_id(1)

        @pl.when(qb == 0)
        def _():
            dk_ref[...] = jnp.zeros_like(dk_ref)
            dv_ref[...] = jnp.zeros_like(dv_ref)

        q = q_ref[...]
        do = do_ref[...].astype(f32)
        do_b = do.astype(bf16)
        gtot = jnp.sum(do_b.astype(f32) * o_ref[...], axis=1, keepdims=True)
        ri, ci = _iota((B, B), 0), _iota((B, B), 1)
        later = (ri > ci).astype(bf16)
        from_here = (ri >= ci).astype(bf16)

        def block(kb, carry, mask):
            dq, run, rung = carry
            off = pl.multiple_of(kb * B, B)
            kblk = k_ref[pl.ds(off, B), :]
            sig, sp, w = _sb_weights(q, kblk, run, later, mask)
            g = w * _dg(do_b, v_ref[pl.ds(off, B), :], 1, 1)
            before = gtot - rung - _split_dot(g, from_here)
            dz = (g * (1.0 - sig) - sig * before) * SB_SCALE
            if mask is not None:
                dz = jnp.where(mask, dz, 0.0)
            dz_b = dz.astype(bf16)
            dv_ref[pl.ds(off, B), :] += _dg(w, do_b, 0, 0)
            dk_ref[pl.ds(off, B), :] += _dg(dz_b, q, 0, 0)
            dq = dq + _dg(dz_b, kblk, 1, 0)
            return dq, run + jnp.sum(sp, axis=1, keepdims=True), rung + jnp.sum(g, axis=1, keepdims=True)

        zero = jnp.zeros((B, 1), f32)
        carry = block(qb, (jnp.zeros((B, HD), f32), zero, zero), ci < ri)
        carry = _sb_older_blocks(qb, carry, lambda kb, cr: block(kb, cr, None), lambda cr: cr[1])
        dq_ref[...] = carry[0]

    blk = pl.BlockSpec((B, HD), lambda h, i: (i, h))
    full = pl.BlockSpec((T, HD), lambda h, i: (0, h))
    sds = jax.ShapeDtypeStruct((T, SB_W), f32)
    return pl.pallas_call(
        body, grid=(SB_HEADS, nq),
        in_specs=[blk, full, pl.BlockSpec((T, HD), lambda h, i: (0, 2 * SB_HEADS + h)), blk, blk],
        out_specs=(blk, full, full), out_shape=(sds, sds, sds), name=name, compiler_params=_params(2))(
            qn, kn, proj, o, dycat)


_BIG = (("mem_w_kv", (4, 128, 1024), 1), ("ssd_w_in", (2, 1024, 579), 2), ("ssd_w_out", (2, 256, 1024), 1),
        ("sb_w_in", (2, 1024, 640), 2), ("sb_w_out", (2, 256, 1024), 1), ("ffn_w_gate_up", (4, 1024, 704), 2),
        ("ffn_w_down", (4, 352, 1024), 1))
_BIG_ROWS = tuple(math.prod(s) // LANES for _, s, _ in _BIG)
_BIG_TOTAL = sum(_BIG_ROWS)


def _pack_rows(parts, total):
    rows = sum(p.shape[-2] for p in parts)
    if rows == total:
        return jnp.concatenate(list(parts), axis=-2)
    pad = jnp.zeros(parts[0].shape[:-2] + (total - rows, LANES), parts[0].dtype)
    return jnp.concatenate(list(parts) + [pad], axis=-2)


def _full_from_slots(slots, shard_shape, axis):
    n = shard_shape[0]
    s = slots.reshape((N_DEV,) + shard_shape)
    if axis == 1:
        return s.transpose(1, 0, 2, 3).reshape(n, N_DEV * shard_shape[1], shard_shape[2])
    return s.transpose(1, 2, 0, 3).reshape(n, shard_shape[1], N_DEV * shard_shape[2])


def _slots_from_full(full, shard_shape, axis):
    n = shard_shape[0]
    if axis == 1:
        s = full.reshape(n, N_DEV, shard_shape[1], shard_shape[2]).transpose(1, 0, 2, 3)
    else:
        s = full.reshape(n, shard_shape[1], N_DEV, shard_shape[2]).transpose(2, 0, 1, 3)
    return s.reshape(N_DEV, n * shard_shape[1], shard_shape[2])


def _ssd_in_cols(w):
    pad = jnp.zeros(w.shape[:-1] + (SSD_IN_PAD - SSD_IN,), w.dtype)
    return jnp.concatenate([w[..., :4096], w[..., 4120:4632], w[..., 4096:4120], pad], axis=-1)


def _ssd_in_cols_back(w):
    return jnp.concatenate([w[..., :4096], w[..., 4608:4632], w[..., 4096:4608]], axis=-1)


def _lane_rows(a):
    flat = a.reshape(-1)
    n = -(-flat.shape[0] // (8 * LANES)) * (8 * LANES)
    return jnp.pad(flat, (0, n - flat.shape[0])).reshape(-1, LANES)


def _pad128(a):
    return jnp.pad(a, ((0, 0), (0, LANES - a.shape[1])))


def kernel(x, mem, mix_norm_g, ffn_norm_g, mem_norm_g, mem_w_kv, mem_q_norm_g, mem_k_norm_g, ssd_w_in, ssd_conv_w, ssd_conv_b, ssd_dt_bias, ssd_a_log, ssd_d, ssd_norm_g, ssd_w_out, sb_w_in, sb_q_norm_g, sb_k_norm_g, sb_w_out, ffn_w_gate_up, ffn_w_down, loss_target, m_mix_norm_g, m_ffn_norm_g, m_mem_norm_g, m_mem_w_kv, m_mem_q_norm_g, m_mem_k_norm_g, m_ssd_w_in, m_ssd_conv_w, m_ssd_conv_b, m_ssd_dt_bias, m_ssd_a_log, m_ssd_d, m_ssd_norm_g, m_ssd_w_out, m_sb_w_in, m_sb_q_norm_g, m_sb_k_norm_g, m_sb_w_out, m_ffn_w_gate_up, m_ffn_w_down, v_mix_norm_g, v_ffn_norm_g, v_mem_norm_g, v_mem_w_kv, v_mem_q_norm_g, v_mem_k_norm_g, v_ssd_w_in, v_ssd_conv_w, v_ssd_conv_b, v_ssd_dt_bias, v_ssd_a_log, v_ssd_d, v_ssd_norm_g, v_ssd_w_out, v_sb_w_in, v_sb_q_norm_g, v_sb_k_norm_g, v_sb_w_out, v_ffn_w_gate_up, v_ffn_w_down):
    W = dict(mix_norm_g=mix_norm_g, ffn_norm_g=ffn_norm_g, mem_norm_g=mem_norm_g, mem_w_kv=mem_w_kv, mem_q_norm_g=mem_q_norm_g, mem_k_norm_g=mem_k_norm_g, ssd_w_in=ssd_w_in, ssd_conv_w=ssd_conv_w, ssd_conv_b=ssd_conv_b, ssd_dt_bias=ssd_dt_bias, ssd_a_log=ssd_a_log, ssd_d=ssd_d, ssd_norm_g=ssd_norm_g, ssd_w_out=ssd_w_out, sb_w_in=sb_w_in, sb_q_norm_g=sb_q_norm_g, sb_k_norm_g=sb_k_norm_g, sb_w_out=sb_w_out, ffn_w_gate_up=ffn_w_gate_up, ffn_w_down=ffn_w_down)
    M = dict(mix_norm_g=m_mix_norm_g, ffn_norm_g=m_ffn_norm_g, mem_norm_g=m_mem_norm_g, mem_w_kv=m_mem_w_kv, mem_q_norm_g=m_mem_q_norm_g, mem_k_norm_g=m_mem_k_norm_g, ssd_w_in=m_ssd_w_in, ssd_conv_w=m_ssd_conv_w, ssd_conv_b=m_ssd_conv_b, ssd_dt_bias=m_ssd_dt_bias, ssd_a_log=m_ssd_a_log, ssd_d=m_ssd_d, ssd_norm_g=m_ssd_norm_g, ssd_w_out=m_ssd_w_out, sb_w_in=m_sb_w_in, sb_q_norm_g=m_sb_q_norm_g, sb_k_norm_g=m_sb_k_norm_g, sb_w_out=m_sb_w_out, ffn_w_gate_up=m_ffn_w_gate_up, ffn_w_down=m_ffn_w_down)
    V = dict(mix_norm_g=v_mix_norm_g, ffn_norm_g=v_ffn_norm_g, mem_norm_g=v_mem_norm_g, mem_w_kv=v_mem_w_kv, mem_q_norm_g=v_mem_q_norm_g, mem_k_norm_g=v_mem_k_norm_g, ssd_w_in=v_ssd_w_in, ssd_conv_w=v_ssd_conv_w, ssd_conv_b=v_ssd_conv_b, ssd_dt_bias=v_ssd_dt_bias, ssd_a_log=v_ssd_a_log, ssd_d=v_ssd_d, ssd_norm_g=v_ssd_norm_g, ssd_w_out=v_ssd_w_out, sb_w_in=v_sb_w_in, sb_q_norm_g=v_sb_q_norm_g, sb_k_norm_g=v_sb_k_norm_g, sb_w_out=v_sb_w_out, ffn_w_gate_up=v_ffn_w_gate_up, ffn_w_down=v_ffn_w_down)
    names = list(W)
    T = x.shape[1]
    x0 = x.reshape(T, D)
    mem2 = mem.reshape(MEM_LEN, D)
    target = loss_target.reshape(T, D)
    my_dev = 4 * lax.axis_index("x") + 2 * lax.axis_index("y") + lax.axis_index("c")

    w_flat = _pack_rows([W[n].astype(bf16).reshape(-1, LANES) for n, _, _ in _BIG], _BIG_TOTAL)
    slots = _allgather_hbm(w_flat, "allgather_weights")
    full, off = {}, 0
    for (n, shp, ax), rows in zip(_BIG, _BIG_ROWS):
        full[n] = _full_from_slots(slots[:, off:off + rows], shp, ax)
        off += rows
    full["ssd_w_in"] = _ssd_in_cols(full["ssd_w_in"])
    conv_slots = _allgather_vmem(_lane_rows(ssd_conv_w), "allgather_conv_w")
    conv_w = _full_from_slots(conv_slots[:, :20], (2, 4, 320), 2)

    mem_g = mem_norm_g.reshape(1, D)

    saved = []
    xc = x0
    for i in range(DEPTH):
        j = i // 2
        ssd = i % 2 == 0
        L = f"l{i}_"
        mix_g = mix_norm_g[i:i + 1]
        h = _rmsnorm_fwd(xc, mix_g, L + "mix_norm")
        w_in = full["ssd_w_in"][j] if ssd else full["sb_w_in"][j]
        proj = _matmul(h, w_in, tm=256 if ssd else 1024, tn=w_in.shape[1] if ssd else 1024, tk=D, name=L + "in_proj")
        k_mem, v_mem = _memkv_fwd(mem2, mem_g, full["mem_w_kv"][i], mem_k_norm_g[i:i + 1], L + "mem_kv")
        q_col = 8 if ssd else 9
        o_mem = _memattn_fwd(proj, q_col, k_mem, v_mem, mem_q_norm_g[i:i + 1], L + "mem_attn")
        st = dict(x_in=xc, h=h, proj=proj, k_mem=k_mem, v_mem=v_mem)
        if ssd:
            xbc = _conv_fwd(proj, conv_w[j], ssd_conv_b[j:j + 1], L + "conv")
            dtb, alog = _pad128(ssd_dt_bias[j:j + 1]), _pad128(ssd_a_log[j:j + 1])
            dsk = jnp.repeat(ssd_d[j], SSD_P).reshape(1, SSD_INNER)
            y, hs = _ssd_fwd(xbc, proj, dtb, alog, dsk, ssd_norm_g[j:j + 1], L + "ssd_scan")
            st.update(xbc=xbc, hs=hs, dtb=dtb, alog=alog, dsk=dsk)
            w_out = full["ssd_w_out"][j]
        else:
            qn, kn = _sb_qknorm_fwd(proj, sb_q_norm_g[j:j + 1], sb_k_norm_g[j:j + 1], L + "qk_norm")
            y, o_exact = _sb_fwd(qn, kn, proj, L + "sb_attn")
            st.update(qn=qn, kn=kn, o=o_exact)
            w_out = full["sb_w_out"][j]
        ycat = jnp.concatenate([y, o_mem], axis=1)
        x_mid = _matmul(ycat, w_out, tm=512, tn=D, tk=2048, res=xc, name=L + "out_proj")
        h2 = _rmsnorm_fwd(x_mid, ffn_norm_g[i:i + 1], L + "ffn_norm")
        gu, act = _ffn_up_act(h2, full["ffn_w_gate_up"][i], L + "ffn_up")
        xc = _matmul(act, full["ffn_w_down"][i], tm=512, tn=D, tk=FFN_H, res=x_mid, name=L + "ffn_down")
        st.update(ycat=ycat, x_mid=x_mid, h2=h2, gu=gu, act=act, w_in=w_in, w_out=w_out)
        saved.append(st)

    dx, loss_part = _loss_head(xc, target, "loss_head")
    loss = lax.psum(jnp.sum(loss_part), ("x", "y", "c"))

    G = {n: [None] * W[n].shape[0] for n in names if W[n].ndim > 1}
    d_mem_g = jnp.zeros((1, D), f32)
    for i in reversed(range(DEPTH)):
        j = i // 2
        ssd = i % 2 == 0
        L = f"l{i}_b_"
        st = saved[i]
        proj = st["proj"]
        G["ffn_w_down"][i] = _matmul(st["act"], dx, ta=True, tm=FFN_H // 2, tn=D, tk=512, name=L + "dw_down")
        dgu = _ffn_dgu(dx, full["ffn_w_down"][i], st["gu"], L + "d_gu")
        dh2 = _matmul(dgu, full["ffn_w_gate_up"][i], tb=True, tm=512, tn=D, tk=2 * FFN_H, name=L + "d_h2")
        G["ffn_w_gate_up"][i] = _matmul(st["h2"], dgu, ta=True, tm=D, tn=FFN_H // 2, tk=512, name=L + "dw_up")
        dx, G["ffn_norm_g"][i] = _rmsnorm_bwd(st["x_mid"], ffn_norm_g[i:i + 1], dh2, dx, L + "d_ffn_norm")
        dycat = _matmul(dx, st["w_out"], tb=True, tm=512, tn=2048, tk=D, name=L + "d_ycat")
        g_out = _matmul(st["ycat"], dx, ta=True, tm=D, tn=D, tk=512, name=L + "dw_out")
        q_col = 8 if ssd else 9
        dq_mem, dk_mem, dv_mem, G["mem_q_norm_g"][i] = _memattn_bwd(
            proj, q_col, st["k_mem"], st["v_mem"], mem_q_norm_g[i:i + 1], dycat, 3, L + "d_mem_attn")
        dmg, G["mem_w_kv"][i], G["mem_k_norm_g"][i] = _memkv_bwd(mem2, mem_g, full["mem_w_kv"][i], mem_k_norm_g[i:i + 1], dk_mem, dv_mem, L + "d_mem_kv")
        d_mem_g = d_mem_g + dmg
        if ssd:
            G["ssd_w_out"][j] = g_out
            dxs, dbm, dcm, dz, ddt, ddtb, dalog, ddsk, dng = _ssd_bwd(
                st["xbc"], proj, st["dtb"], st["alog"], st["dsk"], ssd_norm_g[j:j + 1], st["hs"], dycat, L + "d_ssd_scan")
            G["ssd_dt_bias"][j] = jnp.sum(ddtb, axis=0)[:, :SSD_HEADS]
            G["ssd_a_log"][j] = jnp.sum(dalog, axis=0)[:, :SSD_HEADS]
            G["ssd_d"][j] = jnp.sum(ddsk.reshape(SSD_HEADS, SSD_P), axis=1).reshape(1, SSD_HEADS)
            G["ssd_norm_g"][j] = dng
            dxbc_act = jnp.concatenate([dxs, dbm, dcm], axis=1)
            dpre, G["ssd_conv_w"][j], G["ssd_conv_b"][j] = _conv_bwd_pre(proj, conv_w[j], ssd_conv_b[j:j + 1], dxbc_act, L + "d_conv_pre")
            dxbc = _conv_bwd_in(dpre, conv_w[j], L + "d_conv_in")
            ddt_all = jnp.sum(ddt, axis=0).astype(bf16)
            dproj = jnp.concatenate([dz, dxbc, dq_mem, ddt_all], axis=1)
        else:
            G["sb_w_out"][j] = g_out
            dqn, dkn, dv = _sb_bwd(st["qn"], st["kn"], proj, st["o"], dycat, L + "d_sb_attn")
            dq, dk, G["sb_q_norm_g"][j], G["sb_k_norm_g"][j] = _sb_qknorm_bwd(
                proj, sb_q_norm_g[j:j + 1], sb_k_norm_g[j:j + 1], dqn, dkn, L + "d_qk_norm")
            dproj = jnp.concatenate([dq, dk, dv.astype(bf16), dq_mem], axis=1)
        n_in = dproj.shape[1]
        dh = _matmul(dproj, st["w_in"], tb=True, tm=512, tn=D, tk=n_in, name=L + "d_h")
        g_in = _matmul(st["h"], dproj, ta=True, tm=256 if ssd else D, tn=n_in if ssd else 1024, tk=512, name=L + "dw_in")
        if ssd:
            G["ssd_w_in"][j] = _ssd_in_cols_back(g_in)
        else:
            G["sb_w_in"][j] = g_in
        dx, G["mix_norm_g"][i] = _rmsnorm_bwd(st["x_in"], mix_norm_g[i:i + 1], dh, dx, L + "d_mix_norm")

    grad_x = dx.reshape(x.shape)

    g_slots = [_slots_from_full(jnp.stack(G[n]), shp, ax).astype(bf16) for n, shp, ax in _BIG]
    pairs = _pair_exchange(g_slots, "exchange_grads_pair")
    core = lax.axis_index("c").astype(jnp.int32).reshape(1)
    chip_sums = [_pair_sum(g, r, core, "pair_sum_" + n) for g, r, (n, _, _) in zip(g_slots, pairs, _BIG)]
    parts = _chip_exchange(chip_sums, "exchange_grads_chip")
    out = {}
    for p, (n, shp, _) in zip(parts, _BIG):
        view = (shp[0] * shp[1], shp[2])
        res = _adamw_reduce(p, W[n].reshape(view), M[n].reshape(view), V[n].reshape(view), "adamw_" + n)
        out[n] = tuple(r.reshape(shp) for r in res)

    small = [n for n in names if n not in out and n != "ssd_conv_w"]
    G["mem_norm_g"] = d_mem_g.reshape(D)
    small_grads = [_lane_rows(G[n] if n == "mem_norm_g" else jnp.concatenate(G[n], axis=0)) for n in small]
    conv_grad = _lane_rows(jnp.stack(G["ssd_conv_w"]))
    sm_rows = [g.shape[0] for g in small_grads]
    n_small = sum(sm_rows)
    sm_total = -(-(n_small + conv_grad.shape[0]) // 8) * 8
    gathered = _allgather_vmem(_pack_rows(small_grads + [conv_grad], sm_total), "allgather_small_grads")
    whole = lambda s: pl.BlockSpec(s, lambda i: (0,) * len(s))
    g_sum = _fn_call(_sum_slots, (gathered,), [whole((N_DEV, sm_total, LANES))],
                     jax.ShapeDtypeStruct((sm_total, LANES), f32), whole((sm_total, LANES)), (1,), "sum_small_grads")
    conv_full = g_sum[n_small:n_small + 160].reshape(2, 4, SSD_CONV_DIM)
    conv_mine = lax.dynamic_slice_in_dim(conv_full, my_dev * 320, 320, axis=2)
    ad_total = n_small + 24
    pack = lambda d: _pack_rows([_lane_rows(d[n]) for n in small] + [_lane_rows(d["ssd_conv_w"])], ad_total)
    g_pack = _pack_rows([g_sum[:n_small], _lane_rows(conv_mine)], ad_total)
    blk = whole((ad_total, LANES))
    res_small = _fn_call(lambda g, w, m, v: _adamw_math(w, g, m, v), (g_pack, pack(W), pack(M), pack(V)), [blk] * 4,
                         (jax.ShapeDtypeStruct((ad_total, LANES), f32),) * 3, (blk,) * 3, (1,), "adamw_small")
    res_small = (g_pack,) + tuple(res_small)
    off = 0
    for n, rows in zip(small + ["ssd_conv_w"], sm_rows + [24]):
        size = W[n].size
        out[n] = tuple(r[off:off + rows].reshape(-1)[:size].reshape(W[n].shape) for r in res_small)
        off += rows

    return (loss, grad_x, *[out[n][0] for n in names], *[out[n][1] for n in names],
            *[out[n][2] for n in names], *[out[n][3] for n in names])
```

```python
import functools
import math

import jax
import jax.numpy as jnp
from jax import lax
from jax.experimental import pallas as pl
from jax.experimental.pallas import tpu as pltpu

f32, bf16 = jnp.float32, jnp.bfloat16
MESH = pl.DeviceIdType.MESH

N_DEV = 8
D = 1024
MIX_W = 2048
DEPTH = 4
EPS = 1e-6
MEM_LEN, MEM_HEADS, MEM_W, HD = 256, 4, 512, 128
SSD_INNER, SSD_HEADS, SSD_G, SSD_P, SSD_N, SSD_L = 1536, 24, 4, 64, 128, 128
SSD_GW = SSD_INNER // SSD_G
SSD_CONV_DIM = 2560
SSD_IN = 4632
SSD_IN_PAD = 4736
SB_W, SB_HEADS, SB_IN = 1536, 12, 5120
SB_BLK = 256
SB_SCALE = HD ** -0.5
SB_DEAD = 105.0
FFN_H = 2816
LANES = 128
VMEM_LIMIT = 56 * 1024 * 1024

ADAM_LR, ADAM_B1, ADAM_B2, ADAM_EPS, ADAM_WD, ADAM_STEP = 0.001, 0.9, 0.999, 1e-08, 0.01, 10

HIGHEST = lax.Precision.HIGHEST


def _params(n_grid):
    return pltpu.CompilerParams(dimension_semantics=("arbitrary",) * n_grid, vmem_limit_bytes=VMEM_LIMIT)


def _dg(a, b, ca, cb):
    return lax.dot_general(a.astype(bf16), b.astype(bf16), (((ca,), (cb,)), ((), ())), preferred_element_type=f32)


@jax.custom_vjp
def bdot_nn(a, b):
    return _dg(a, b, 1, 0)


def _nn_fwd(a, b):
    return _dg(a, b, 1, 0), (a, b)


def _nn_bwd(res, ct):
    a, b = res
    return _dg(ct, b, 1, 1).astype(a.dtype), _dg(a, ct, 0, 0).astype(b.dtype)


bdot_nn.defvjp(_nn_fwd, _nn_bwd)


@jax.custom_vjp
def bdot_nt(a, b):
    return _dg(a, b, 1, 1)


def _nt_fwd(a, b):
    return _dg(a, b, 1, 1), (a, b)


def _nt_bwd(res, ct):
    a, b = res
    return _dg(ct, b, 1, 0).astype(a.dtype), _dg(ct, a, 0, 0).astype(b.dtype)


bdot_nt.defvjp(_nt_fwd, _nt_bwd)


@jax.custom_vjp
def bdot_tn(a, b):
    return _dg(a, b, 0, 0)


def _tn_fwd(a, b):
    return _dg(a, b, 0, 0), (a, b)


def _tn_bwd(res, ct):
    a, b = res
    return _dg(b, ct, 1, 1).astype(a.dtype), _dg(a, ct, 1, 0).astype(b.dtype)


bdot_tn.defvjp(_tn_fwd, _tn_bwd)


def _rms(x, g):
    return x * lax.rsqrt(jnp.mean(x * x, axis=-1, keepdims=True) + EPS) * g


def _iota(shape, axis):
    return lax.broadcasted_iota(jnp.int32, shape, axis)


def _fn_call(fn, args, in_specs, out_shapes, out_specs, grid, name, acc=None, into=None):
    n_in = len(args)
    acc = acc or {}
    n_grid = len(grid)
    aliases = {}
    if into is not None:
        args, in_specs, aliases = tuple(args) + (into,), list(in_specs) + [pl.BlockSpec(memory_space=pl.ANY)], {n_in: 0}

    def body(*refs):
        ins, outs = refs[:n_in], refs[len(args):]
        res = fn(*[r[...] for r in ins])
        if not isinstance(res, (tuple, list)):
            res = (res,)
        for k, (o, r) in enumerate(zip(outs, res)):
            mode = acc.get(k)
            if mode is None:
                o[...] = r.astype(o.dtype)
                continue
            if mode == "last":
                first = pl.program_id(n_grid - 1) == 0
            else:
                first = functools.reduce(jnp.logical_and, [pl.program_id(d) == 0 for d in range(n_grid)])

            @pl.when(first)
            def _(o=o, r=r):
                o[...] = r.astype(o.dtype)

            @pl.when(jnp.logical_not(first))
            def _(o=o, r=r):
                o[...] += r.astype(o.dtype)

    return pl.pallas_call(
        body, grid=grid, in_specs=in_specs, out_specs=out_specs, out_shape=out_shapes, name=name,
        input_output_aliases=aliases, compiler_params=_params(n_grid))(*args)


def _matmul(a, b, *, ta=False, tb=False, out_dtype=f32, tm, tn, tk, res=None, name):
    M, K = (a.shape[1], a.shape[0]) if ta else a.shape
    N = b.shape[0] if tb else b.shape[1]
    tm, tn, tk = min(tm, M), min(tn, N), min(tk, K)
    assert M % tm == 0 and N % tn == 0 and K % tk == 0, (name, M, N, K, tm, tn, tk)
    nk = K // tk
    a_spec = pl.BlockSpec((tk, tm), lambda i, j, k: (k, i)) if ta else pl.BlockSpec((tm, tk), lambda i, j, k: (i, k))
    b_spec = pl.BlockSpec((tn, tk), lambda i, j, k: (j, k)) if tb else pl.BlockSpec((tk, tn), lambda i, j, k: (k, j))
    o_spec = pl.BlockSpec((tm, tn), lambda i, j, k: (i, j))
    ca, cb = (0 if ta else 1), (1 if tb else 0)

    def body(*refs):
        a_ref, b_ref = refs[:2]
        r_ref = None if res is None else refs[2]
        o_ref = refs[2 if res is None else 3]
        part = _dg(a_ref[...], b_ref[...], ca, cb)

        def finish(out):
            if r_ref is not None:
                out = out + r_ref[...].astype(f32)
            o_ref[...] = out.astype(o_ref.dtype)

        if nk == 1:
            finish(part)
            return
        acc_ref = refs[-1]
        k = pl.program_id(2)

        @pl.when(k == 0)
        def _():
            acc_ref[...] = part

        @pl.when(k > 0)
        def _():
            acc_ref[...] += part

        @pl.when(k == nk - 1)
        def _():
            finish(acc_ref[...])

    args = (a, b) if res is None else (a, b, res)
    in_specs = [a_spec, b_spec] + ([] if res is None else [o_spec])
    return pl.pallas_call(
        body, grid=(M // tm, N // tn, nk), in_specs=in_specs, out_specs=o_spec,
        out_shape=jax.ShapeDtypeStruct((M, N), out_dtype), name=name,
        scratch_shapes=[] if nk == 1 else [pltpu.VMEM((tm, tn), f32)], compiler_params=_params(3))(*args)


def _row_tile(T):
    return min(T, 512)


def _my_pos():
    return lax.axis_index("x"), lax.axis_index("y"), lax.axis_index("c")


def _allgather_hbm(xs, name):
    R, C = xs.shape

    def body(x_ref, out_ref, send_sems, recv_sems, local_sem):
        x, y, c = _my_pos()
        me, sibling = (x, y, c), (x, y, 1 - c)
        chips = [(1 - x, y), (x, 1 - y), (1 - x, 1 - y)]

        def slot(px, py, pc):
            return out_ref.at[4 * px + 2 * py + pc]

        def copy(k, block, to, src=None):
            return pltpu.make_async_remote_copy(
                src_ref=slot(*block) if src is None else src, dst_ref=slot(*block),
                send_sem=send_sems.at[k], recv_sem=recv_sems.at[k], device_id=to, device_id_type=MESH)

        mine = pltpu.make_async_copy(x_ref, slot(*me), local_sem)
        mine.start()
        first = [copy(0, me, sibling, src=x_ref)]
        first += [copy(1 + j, me, (*chip, c), src=x_ref) for j, chip in enumerate(chips)]
        for cp in first:
            cp.start()
        passed = [copy(4 + j, (*chip, c), sibling) for j, chip in enumerate(chips)]
        for j, chip in enumerate(chips):
            copy(1 + j, (*chip, c), me).wait_recv()
            passed[j].start()
        copy(0, sibling, me).wait_recv()
        for j, chip in enumerate(chips):
            copy(4 + j, (*chip, 1 - c), me).wait_recv()
        for cp in first + passed:
            cp.wait_send()
        mine.wait()

    return pl.pallas_call(
        body, out_shape=jax.ShapeDtypeStruct((N_DEV, R, C), xs.dtype),
        in_specs=[pl.BlockSpec(memory_space=pl.ANY)], out_specs=pl.BlockSpec(memory_space=pl.ANY),
        scratch_shapes=[pltpu.SemaphoreType.DMA((7,)), pltpu.SemaphoreType.DMA((7,)), pltpu.SemaphoreType.DMA],
        name=name)(xs)


def _allgather_vmem(xs, name):
    R, C = xs.shape

    def body(x_ref, out_ref, send_sems, recv_sems):
        x, y, c = _my_pos()
        me = 4 * x + 2 * y + c
        out_ref[me] = x_ref[...]
        copies = []
        for k in range(1, N_DEV):
            px = 1 - x if k & 4 else x
            py = 1 - y if k & 2 else y
            pc = 1 - c if k & 1 else c
            cp = pltpu.make_async_remote_copy(
                src_ref=x_ref, dst_ref=out_ref.at[me], send_sem=send_sems.at[k - 1], recv_sem=recv_sems.at[k - 1],
                device_id=(px, py, pc), device_id_type=MESH)
            cp.start()
            copies.append(cp)
        for cp in copies:
            cp.wait()

    return pl.pallas_call(
        body, out_shape=jax.ShapeDtypeStruct((N_DEV, R, C), xs.dtype),
        in_specs=[pl.BlockSpec(memory_space=pltpu.VMEM)], out_specs=pl.BlockSpec(memory_space=pltpu.VMEM),
        scratch_shapes=[pltpu.SemaphoreType.DMA((7,)), pltpu.SemaphoreType.DMA((7,))], name=name)(xs)


N_CHIP = 4


def _pair_exchange(gs, name):
    n = len(gs)

    def body(*refs):
        g_refs, out_refs = refs[:n], refs[n:2 * n]
        send_sems, recv_sems = refs[2 * n:]
        x, y, c = _my_pos()
        copies = []
        for a in range(n):
            for k in range(N_CHIP):
                cp = pltpu.make_async_remote_copy(
                    src_ref=g_refs[a].at[4 * (k >> 1) + 2 * (k & 1) + 1 - c], dst_ref=out_refs[a].at[k],
                    send_sem=send_sems.at[a, k], recv_sem=recv_sems.at[a, k],
                    device_id=(x, y, 1 - c), device_id_type=MESH)
                cp.start()
                copies.append(cp)
        for cp in copies:
            cp.wait()

    hbm = pl.BlockSpec(memory_space=pl.ANY)
    return pl.pallas_call(
        body, out_shape=[jax.ShapeDtypeStruct((N_CHIP,) + g.shape[1:], g.dtype) for g in gs],
        in_specs=[hbm] * n, out_specs=[hbm] * n,
        scratch_shapes=[pltpu.SemaphoreType.DMA((n, N_CHIP))] * 2, name=name)(*gs)


def _chip_exchange(cs, name):
    n = len(cs)

    def body(*refs):
        c_refs, out_refs = refs[:n], refs[n:2 * n]
        send_sems, recv_sems, local_sems = refs[2 * n:]
        x, y, c = _my_pos()
        my_chip = 2 * x + y
        copies = []
        for a in range(n):
            cp = pltpu.make_async_copy(c_refs[a].at[my_chip], out_refs[a].at[my_chip], local_sems.at[a])
            cp.start()
            copies.append(cp)
            for rel in range(1, N_CHIP):
                px = 1 - x if rel & 2 else x
                py = 1 - y if rel & 1 else y
                cp = pltpu.make_async_remote_copy(
                    src_ref=c_refs[a].at[2 * px + py], dst_ref=out_refs[a].at[my_chip],
                    send_sem=send_sems.at[a, rel - 1], recv_sem=recv_sems.at[a, rel - 1],
                    device_id=(px, py, c), device_id_type=MESH)
                cp.start()
                copies.append(cp)
        for cp in copies:
            cp.wait()

    hbm = pl.BlockSpec(memory_space=pl.ANY)
    return pl.pallas_call(
        body, out_shape=[jax.ShapeDtypeStruct(g.shape, g.dtype) for g in cs],
        in_specs=[hbm] * n, out_specs=[hbm] * n,
        scratch_shapes=[pltpu.SemaphoreType.DMA((n, N_CHIP - 1)), pltpu.SemaphoreType.DMA((n, N_CHIP - 1)),
                        pltpu.SemaphoreType.DMA((n,))], name=name)(*cs)


def _shard_row_tile(rows):
    return next(t for t in (256, 128) if rows % t == 0)


def _pair_sum(g, recv, core, name):
    _, rows, cols = g.shape
    tr = _shard_row_tile(rows)

    def body(core_ref, g0, g1, g2, g3, recv_ref, o_ref):
        for k, g_ref in enumerate((g0, g1, g2, g3)):
            o_ref[k] = (g_ref[...].astype(f32) + recv_ref[k].astype(f32)).astype(o_ref.dtype)

    mine = [pl.BlockSpec((None, tr, cols), lambda i, core, k=k: (4 * (k >> 1) + 2 * (k & 1) + core[0], i, 0))
            for k in range(N_CHIP)]
    four = pl.BlockSpec((N_CHIP, tr, cols), lambda i, core: (0, i, 0))
    return pl.pallas_call(
        body, grid_spec=pltpu.PrefetchScalarGridSpec(
            num_scalar_prefetch=1, grid=(rows // tr,), in_specs=mine + [four], out_specs=four),
        out_shape=jax.ShapeDtypeStruct((N_CHIP, rows, cols), g.dtype), name=name,
        compiler_params=_params(1))(core, g, g, g, g, recv)


def _adamw_math(w, g, m, v):
    m = ADAM_B1 * m + (1.0 - ADAM_B1) * g
    v = ADAM_B2 * v + (1.0 - ADAM_B2) * jnp.square(g)
    m_hat = m / (1.0 - ADAM_B1 ** ADAM_STEP)
    v_hat = v / (1.0 - ADAM_B2 ** ADAM_STEP)
    delta = -ADAM_LR * (m_hat / (jnp.sqrt(v_hat) + ADAM_EPS) + ADAM_WD * w)
    return delta, m, v


def _sum_slots(parts):
    g = parts[0].astype(f32)
    for i in range(1, parts.shape[0]):
        g = g + parts[i].astype(f32)
    return g


def _adamw_reduce(parts, w, m, v, name):
    rows, cols = w.shape
    tr = _shard_row_tile(rows)

    def fn(p, w, m, v):
        g = _sum_slots(p)
        return (g,) + _adamw_math(w, g, m, v)

    row = pl.BlockSpec((tr, cols), lambda i: (i, 0))
    sds = jax.ShapeDtypeStruct((rows, cols), f32)
    return _fn_call(fn, (parts, w, m, v), [pl.BlockSpec((N_CHIP, tr, cols), lambda i: (0, i, 0)), row, row, row],
                    (sds,) * 4, (row,) * 4, (rows // tr,), name)


def _rmsnorm_fwd(x, g, name):
    T = x.shape[0]
    tm = _row_tile(T)
    row = pl.BlockSpec((tm, D), lambda i: (i, 0))
    par = pl.BlockSpec((1, D), lambda i: (0, 0))
    return _fn_call(lambda x, g: _rms(x, g), (x, g), [row, par], jax.ShapeDtypeStruct((T, D), bf16), row, (T // tm,), name)


def _rmsnorm_bwd(x, g, dh, dres, name):
    T = x.shape[0]
    tm = _row_tile(T)
    row = pl.BlockSpec((tm, D), lambda i: (i, 0))
    par = pl.BlockSpec((1, D), lambda i: (0, 0))

    def fn(x, g, dh, dres):
        _, vjp = jax.vjp(_rms, x, g)
        dx, dg = vjp(dh.astype(f32))
        return dx + dres, dg

    return _fn_call(fn, (x, g, dh, dres), [row, par, row, row],
                    (jax.ShapeDtypeStruct((T, D), f32), jax.ShapeDtypeStruct((1, D), f32)), (row, par),
                    (T // tm,), name, acc={1: "all"})


def _swiglu_act(g, u):
    return jax.nn.silu(g) * u


def _ffn_up_act(h2, w_gu, name):
    T = h2.shape[0]
    tm = min(T, 256)

    def body(h_ref, w_ref, gu_ref, act_ref):
        gu = _dg(h_ref[...], w_ref[...], 1, 0)
        gu_ref[...] = gu
        act_ref[...] = _swiglu_act(gu[:, :FFN_H], gu[:, FFN_H:]).astype(act_ref.dtype)

    return pl.pallas_call(
        body, grid=(T // tm,),
        in_specs=[pl.BlockSpec((tm, D), lambda i: (i, 0)), pl.BlockSpec((D, 2 * FFN_H), lambda i: (0, 0))],
        out_specs=(pl.BlockSpec((tm, 2 * FFN_H), lambda i: (i, 0)), pl.BlockSpec((tm, FFN_H), lambda i: (i, 0))),
        out_shape=(jax.ShapeDtypeStruct((T, 2 * FFN_H), f32), jax.ShapeDtypeStruct((T, FFN_H), bf16)),
        name=name, compiler_params=_params(1))(h2, w_gu)


def _ffn_dgu(dx, w_down, gu, name):
    T = dx.shape[0]
    tm = min(T, 256)

    def body(dx_ref, w_ref, gu_ref, o_ref):
        dact = _dg(dx_ref[...], w_ref[...], 1, 1)
        gu = gu_ref[...]
        _, vjp = jax.vjp(_swiglu_act, gu[:, :FFN_H], gu[:, FFN_H:])
        dg, du = vjp(dact)
        o_ref[:, :FFN_H] = dg.astype(o_ref.dtype)
        o_ref[:, FFN_H:] = du.astype(o_ref.dtype)

    return pl.pallas_call(
        body, grid=(T // tm,),
        in_specs=[pl.BlockSpec((tm, D), lambda i: (i, 0)), pl.BlockSpec((FFN_H, D), lambda i: (0, 0)),
                  pl.BlockSpec((tm, 2 * FFN_H), lambda i: (i, 0))],
        out_specs=pl.BlockSpec((tm, 2 * FFN_H), lambda i: (i, 0)),
        out_shape=jax.ShapeDtypeStruct((T, 2 * FFN_H), bf16), name=name, compiler_params=_params(1))(dx, w_down, gu)


def _loss_head(x, target, name):
    T = x.shape[0]
    tm = _row_tile(T)
    row = pl.BlockSpec((tm, D), lambda i: (i, 0))
    par = pl.BlockSpec((1, LANES), lambda i: (0, 0))

    def fn(x, t):
        e = x - t
        s = jnp.sum(e * e, axis=0, keepdims=True)
        part = s[:, 0:LANES]
        for k in range(1, D // LANES):
            part = part + s[:, k * LANES:(k + 1) * LANES]
        return e * (1.0 / D), part * (0.5 / D)

    return _fn_call(fn, (x, target), [row, row],
                    (jax.ShapeDtypeStruct((T, D), f32), jax.ShapeDtypeStruct((1, LANES), f32)), (row, par),
                    (T // tm,), name, acc={1: "all"})


def _memkv_fn(mem, mg, wkv, kg):
    mn = _rms(mem, mg)
    kv = bdot_nn(mn, wkv)
    ks = [_rms(kv[:, h * HD:(h + 1) * HD], kg) for h in range(MEM_HEADS)]
    return jnp.concatenate(ks, axis=1), kv[:, MEM_W:]


def _memkv_fwd(mem, mg, wkv, kg, name):
    whole = lambda s: pl.BlockSpec(s, lambda i: (0,) * len(s))
    sds = jax.ShapeDtypeStruct((MEM_LEN, MEM_W), f32)
    return _fn_call(lambda m, g, w, k: _memkv_fn(m, g, w.astype(f32), k), (mem, mg, wkv, kg),
                    [whole((MEM_LEN, D)), whole((1, D)), whole((D, 2 * MEM_W)), whole((1, HD))],
                    (sds, sds), (whole((MEM_LEN, MEM_W)),) * 2, (1,), name)


def _memkv_bwd(mem, mg, wkv, kg, dk, dv, name):
    whole = lambda s: pl.BlockSpec(s, lambda i: (0,) * len(s))

    def fn(m, g, w, k, dk, dv):
        _, vjp = jax.vjp(lambda g, w, k: _memkv_fn(m, g, w, k), g, w.astype(f32), k)
        return vjp((dk, dv))

    return _fn_call(fn, (mem, mg, wkv, kg, dk, dv),
                    [whole((MEM_LEN, D)), whole((1, D)), whole((D, 2 * MEM_W)), whole((1, HD)),
                     whole((MEM_LEN, MEM_W)), whole((MEM_LEN, MEM_W))],
                    (jax.ShapeDtypeStruct((1, D), f32), jax.ShapeDtypeStruct((D, 2 * MEM_W), f32),
                     jax.ShapeDtypeStruct((1, HD), f32)),
                    (whole((1, D)), whole((D, 2 * MEM_W)), whole((1, HD))), (1,), name)


def _memattn_fn(q, k, v, qg):
    qn = _rms(q, qg)
    s = bdot_nt(qn, k) * (HD ** -0.5)
    s = s - jnp.max(s, axis=-1, keepdims=True)
    p = jnp.exp(s)
    p = p / jnp.sum(p, axis=-1, keepdims=True)
    return bdot_nn(p, v)


def _per_head(fn, n_heads, head_args, *shared):
    outs = [fn(*[a[:, h * HD:(h + 1) * HD] for a in head_args], *shared) for h in range(n_heads)]
    if isinstance(outs[0], tuple):
        return tuple(jnp.concatenate(o, axis=1) for o in zip(*outs))
    return jnp.concatenate(outs, axis=1)


def _memattn_all(q, k, v, qg):
    return _per_head(_memattn_fn, MEM_HEADS, (q, k, v), qg)


def _memattn_fwd(proj, q_col, k, v, qg, into, out_col, name):
    T = proj.shape[0]
    tm = _row_tile(T)
    kv_spec = pl.BlockSpec((MEM_LEN, MEM_W), lambda i: (0, 0))
    return _fn_call(_memattn_all, (proj, k, v, qg),
                    [pl.BlockSpec((tm, MEM_W), lambda i: (i, q_col)), kv_spec, kv_spec, pl.BlockSpec((1, HD), lambda i: (0, 0))],
                    jax.ShapeDtypeStruct(into.shape, into.dtype), pl.BlockSpec((tm, MEM_W), lambda i: (i, out_col)),
                    (T // tm,), name, into=into)


def _memattn_bwd(proj, q_col, k, v, qg, dycat, do_col, into, out_col, name):
    T = proj.shape[0]
    tm = _row_tile(T)

    def fn(q, k, v, qg, do):
        _, vjp = jax.vjp(_memattn_all, q, k, v, qg)
        return vjp(do.astype(f32))

    kv_spec = pl.BlockSpec((MEM_LEN, MEM_W), lambda i: (0, 0))
    kv_sds = jax.ShapeDtypeStruct((MEM_LEN, MEM_W), f32)
    par = pl.BlockSpec((1, HD), lambda i: (0, 0))
    return _fn_call(fn, (proj, k, v, qg, dycat),
                    [pl.BlockSpec((tm, MEM_W), lambda i: (i, q_col)), kv_spec, kv_spec, par,
                     pl.BlockSpec((tm, MEM_W), lambda i: (i, do_col))],
                    (jax.ShapeDtypeStruct(into.shape, into.dtype), kv_sds, kv_sds, jax.ShapeDtypeStruct((1, HD), f32)),
                    (pl.BlockSpec((tm, MEM_W), lambda i: (i, out_col)), kv_spec, kv_spec, par),
                    (T // tm,), name, acc={1: "all", 2: "all", 3: "all"}, into=into)


def _conv_taps(xp, w, first, tm):
    out = w[0:1, :] * xp[first:first + tm, :]
    for k in range(1, 4):
        out = out + w[k:k + 1, :] * xp[first + k:first + k + tm, :]
    return out


def _conv_blocks(T):
    tm, tc = _row_tile(T), 512
    nt = T // tm
    cur = pl.BlockSpec((tm, tc), lambda j, i: (i, 3 + j))
    prev = pl.BlockSpec((8, tc), lambda j, i: (jnp.maximum(i * (tm // 8) - 1, 0), 3 + j))
    par4 = pl.BlockSpec((4, tc), lambda j, i: (0, j))
    par1 = pl.BlockSpec((1, tc), lambda j, i: (0, j))
    out = pl.BlockSpec((tm, tc), lambda j, i: (i, j))
    return tm, tc, nt, cur, prev, par4, par1, out


def _conv_fwd(proj, w, b, name):
    T = proj.shape[0]
    tm, tc, nt, cur, prev, par4, par1, out = _conv_blocks(T)

    def body(prev_ref, cur_ref, w_ref, b_ref, o_ref):
        halo = jnp.where(pl.program_id(1) == 0, 0.0, prev_ref[...])
        xp = jnp.concatenate([halo, cur_ref[...]], axis=0)
        o_ref[...] = jax.nn.silu(_conv_taps(xp, w_ref[...], 5, tm) + b_ref[...])

    return pl.pallas_call(body, grid=(SSD_CONV_DIM // tc, nt), in_specs=[prev, cur, par4, par1], out_specs=out,
                          out_shape=jax.ShapeDtypeStruct((T, SSD_CONV_DIM), f32), name=name,
                          compiler_params=_params(2))(proj, proj, w, b)


def _conv_bwd_pre(proj, w, b, dact, name):
    T = proj.shape[0]
    tm, tc, nt, cur, prev, par4, par1, out = _conv_blocks(T)

    def body(prev_ref, cur_ref, w_ref, b_ref, da_ref, dp_ref, dw_ref, db_ref):
        i = pl.program_id(1)
        halo = jnp.where(i == 0, 0.0, prev_ref[...])
        xp = jnp.concatenate([halo, cur_ref[...]], axis=0)
        pre = _conv_taps(xp, w_ref[...], 5, tm) + b_ref[...]
        sig = jax.nn.sigmoid(pre)
        dpre = da_ref[...] * (sig * (1.0 + pre * (1.0 - sig)))
        dp_ref[...] = dpre
        dw = jnp.concatenate([jnp.sum(dpre * xp[5 + k:5 + k + tm, :], axis=0, keepdims=True) for k in range(4)], axis=0)
        db = jnp.sum(dpre, axis=0, keepdims=True)

        @pl.when(i == 0)
        def _():
            dw_ref[...] = dw
            db_ref[...] = db

        @pl.when(i > 0)
        def _():
            dw_ref[...] += dw
            db_ref[...] += db

    return pl.pallas_call(
        body, grid=(SSD_CONV_DIM // tc, nt), in_specs=[prev, cur, par4, par1, out], out_specs=(out, par4, par1),
        out_shape=(jax.ShapeDtypeStruct((T, SSD_CONV_DIM), f32), jax.ShapeDtypeStruct((4, SSD_CONV_DIM), f32),
                   jax.ShapeDtypeStruct((1, SSD_CONV_DIM), f32)),
        name=name, compiler_params=_params(2))(proj, proj, w, b, dact)


def _conv_bwd_in(dpre, w, into, name):
    T = dpre.shape[0]
    tm, tc, nt, cur, _, par4, _, out = _conv_blocks(T)
    nxt = pl.BlockSpec((8, tc), lambda j, i: (jnp.minimum((i + 1) * (tm // 8), T // 8 - 1), j))

    def body(cur_ref, nxt_ref, w_ref, into_ref, o_ref):
        halo = jnp.where(pl.program_id(1) == nt - 1, 0.0, nxt_ref[...])
        xp = jnp.concatenate([cur_ref[...], halo], axis=0)
        w = w_ref[...]
        acc = w[3:4, :] * xp[0:tm, :]
        for k in range(3):
            acc = acc + w[k:k + 1, :] * xp[3 - k:3 - k + tm, :]
        o_ref[...] = acc.astype(o_ref.dtype)

    return pl.pallas_call(body, grid=(SSD_CONV_DIM // tc, nt),
                          in_specs=[out, nxt, par4, pl.BlockSpec(memory_space=pl.ANY)], out_specs=cur,
                          out_shape=jax.ShapeDtypeStruct(into.shape, into.dtype), input_output_aliases={3: 0}, name=name,
                          compiler_params=_params(2))(dpre, dpre, w, into)


def _ssd_chunk(hbase, xs, bm, cm, z, dtr, dtb, alog, dsk, ng, ht):
    L = SSD_L
    dt = jax.nn.softplus(dtr + dtb)
    da = dt * (-jnp.exp(alog))
    li, si = _iota((L, L), 0), _iota((L, L), 1)
    causal = li >= si
    cs = jnp.dot(causal.astype(f32), da, precision=HIGHEST, preferred_element_type=f32)
    cs_t = cs.T
    chan_head = _iota((1, SSD_GW), 1) // SSD_P
    heads = range(SSD_GW // SSD_P)
    lane_of = [(_iota((1, LANES), 1) == hbase + r).astype(f32) for r in heads]
    cs_cols = [jnp.sum(cs * lane_of[r], axis=1, keepdims=True) for r in heads]
    dt_cols = [jnp.sum(dt * lane_of[r], axis=1, keepdims=True) for r in heads]
    cs_e = jnp.zeros((L, SSD_GW), f32)
    dt_e = jnp.zeros((L, SSD_GW), f32)
    for r in heads:
        cs_e = jnp.where(chan_head == r, cs_cols[r], cs_e)
        dt_e = jnp.where(chan_head == r, dt_cols[r], dt_e)
    xdt = xs * dt_e
    cb = bdot_nt(cm, bm)
    y = jnp.zeros((L, SSD_GW), f32)
    for r in heads:
        cs_row = jnp.sum(cs_t * (_iota((LANES, 1), 0) == hbase + r).astype(f32), axis=0, keepdims=True)
        decay = jnp.where(causal, jnp.exp(jnp.where(causal, cs_cols[r] - cs_row, 0.0)), 0.0)
        y = y + bdot_nn(cb * decay, xdt * (chan_head == r).astype(f32))
    y = y + jnp.exp(cs_e) * bdot_nn(cm, ht)
    cs_last = jnp.sum(cs_e * (_iota((L, 1), 0) == L - 1).astype(f32), axis=0, keepdims=True)
    ht_new = ht * jnp.exp(cs_last) + bdot_tn(bm, xdt * jnp.exp(cs_last - cs_e))
    y = (y + dsk * xs) * jax.nn.silu(z)
    return _rms(y, ng), ht_new


def _ssd_specs(T, rev):
    nc = T // SSD_L
    cidx = (lambda c: nc - 1 - c) if rev else (lambda c: c)
    return nc, dict(
        xs=pl.BlockSpec((SSD_L, SSD_GW), lambda g, c: (cidx(c), g)),
        bm=pl.BlockSpec((SSD_L, SSD_N), lambda g, c: (cidx(c), 12 + g)),
        cm=pl.BlockSpec((SSD_L, SSD_N), lambda g, c: (cidx(c), 16 + g)),
        z=pl.BlockSpec((SSD_L, SSD_GW), lambda g, c: (cidx(c), g)),
        dt=pl.BlockSpec((SSD_L, LANES), lambda g, c: (cidx(c), 36)),
        p128=pl.BlockSpec((1, LANES), lambda g, c: (0, 0)),
        pgw=pl.BlockSpec((1, SSD_GW), lambda g, c: (0, g)),
        hs=pl.BlockSpec((None, None, SSD_N, SSD_GW), lambda g, c: (g, cidx(c), 0, 0)),
        grp=pl.BlockSpec((SSD_L, SSD_N), lambda g, c: (cidx(c), g)),
    )


def _ssd_fwd(xbc, proj, dtb, alog, dsk, ng, name):
    T = proj.shape[0]
    nc, s = _ssd_specs(T, False)

    def body(xs_ref, bm_ref, cm_ref, z_ref, dt_ref, dtb_ref, alog_ref, dsk_ref, ng_ref, y_ref, hs_ref, h_scr):
        @pl.when(pl.program_id(1) == 0)
        def _():
            h_scr[...] = jnp.zeros_like(h_scr)

        ht = h_scr[...]
        hs_ref[...] = ht
        y, ht_new = _ssd_chunk(pl.program_id(0) * (SSD_GW // SSD_P), xs_ref[...], bm_ref[...], cm_ref[...], z_ref[...],
                               dt_ref[...], dtb_ref[...], alog_ref[...], dsk_ref[...], ng_ref[...], ht)
        y_ref[...] = y.astype(y_ref.dtype)
        h_scr[...] = ht_new

    return pl.pallas_call(
        body, grid=(SSD_G, nc),
        in_specs=[s["xs"], s["bm"], s["cm"], s["z"], s["dt"], s["p128"], s["p128"], s["pgw"], s["pgw"]],
        out_specs=(s["xs"], s["hs"]),
        out_shape=(jax.ShapeDtypeStruct((T, MIX_W), bf16), jax.ShapeDtypeStruct((SSD_G, nc, SSD_N, SSD_GW), f32)),
        scratch_shapes=[pltpu.VMEM((SSD_N, SSD_GW), f32)], name=name, compiler_params=_params(2))(
            xbc, xbc, xbc, proj, proj, dtb, alog, dsk, ng)


def _ssd_bwd(xbc, proj, dtb, alog, dsk, ng, hs, dycat, name):
    T = proj.shape[0]
    nc, s = _ssd_specs(T, True)

    def body(xs_ref, bm_ref, cm_ref, z_ref, dt_ref, dtb_ref, alog_ref, dsk_ref, ng_ref, hs_ref, dy_ref,
             dxs_ref, dbm_ref, dcm_ref, dz_ref, ddt_ref, ddtb_ref, dalog_ref, ddsk_ref, dng_ref, dh_scr):
        c = pl.program_id(1)

        @pl.when(c == 0)
        def _():
            dh_scr[...] = jnp.zeros_like(dh_scr)

        hbase = pl.program_id(0) * (SSD_GW // SSD_P)
        _, vjp = jax.vjp(functools.partial(_ssd_chunk, hbase), xs_ref[...], bm_ref[...], cm_ref[...], z_ref[...],
                         dt_ref[...], dtb_ref[...], alog_ref[...], dsk_ref[...], ng_ref[...], hs_ref[...])
        dxs, dbm, dcm, dz, ddt, ddtb, dalog, ddsk, dng, dht = vjp((dy_ref[...].astype(f32), dh_scr[...]))
        dxs_ref[...] = dxs
        dbm_ref[...] = dbm
        dcm_ref[...] = dcm
        dz_ref[...] = dz.astype(dz_ref.dtype)
        ddt_ref[...] = ddt
        dh_scr[...] = dht

        @pl.when(c == 0)
        def _():
            ddtb_ref[...] = ddtb
            dalog_ref[...] = dalog
            ddsk_ref[...] = ddsk
            dng_ref[...] = dng

        @pl.when(c > 0)
        def _():
            ddtb_ref[...] += ddtb
            dalog_ref[...] += dalog
            ddsk_ref[...] += ddsk
            dng_ref[...] += dng

    cidx = lambda c: nc - 1 - c
    g128 = pl.BlockSpec((None, 1, LANES), lambda g, c: (g, 0, 0))
    return pl.pallas_call(
        body, grid=(SSD_G, nc),
        in_specs=[s["xs"], s["bm"], s["cm"], s["z"], s["dt"], s["p128"], s["p128"], s["pgw"], s["pgw"], s["hs"], s["xs"]],
        out_specs=(s["xs"], s["grp"], s["grp"], s["xs"],
                   pl.BlockSpec((None, SSD_L, LANES), lambda g, c: (g, cidx(c), 0)), g128, g128, s["pgw"], s["pgw"]),
        out_shape=(jax.ShapeDtypeStruct((T, SSD_INNER), f32), jax.ShapeDtypeStruct((T, SSD_G * SSD_N), f32),
                   jax.ShapeDtypeStruct((T, SSD_G * SSD_N), f32), jax.ShapeDtypeStruct((T, SSD_IN_PAD), bf16),
                   jax.ShapeDtypeStruct((SSD_G, T, LANES), f32), jax.ShapeDtypeStruct((SSD_G, 1, LANES), f32),
                   jax.ShapeDtypeStruct((SSD_G, 1, LANES), f32), jax.ShapeDtypeStruct((1, SSD_INNER), f32),
                   jax.ShapeDtypeStruct((1, SSD_INNER), f32)),
        scratch_shapes=[pltpu.VMEM((SSD_N, SSD_GW), f32)], name=name, compiler_params=_params(2))(
            xbc, xbc, xbc, proj, proj, dtb, alog, dsk, ng, hs, dycat)


def _qk_norm_fn(q, k, qg, kg):
    return _per_head(_rms, SB_HEADS, (q,), qg), _per_head(_rms, SB_HEADS, (k,), kg)


def _sb_qknorm_fwd(proj, qg, kg, name):
    T = proj.shape[0]
    tm = min(T, 256)
    par = pl.BlockSpec((1, HD), lambda i: (0, 0))
    out = pl.BlockSpec((tm, SB_W), lambda i: (i, 0))
    sds = jax.ShapeDtypeStruct((T, SB_W), bf16)
    return _fn_call(_qk_norm_fn, (proj, proj, qg, kg), [out, pl.BlockSpec((tm, SB_W), lambda i: (i, 1)), par, par],
                    (sds, sds), (out, out), (T // tm,), name)


def _sb_qknorm_bwd(proj, qg, kg, dqn, dkn, dv, name):
    T = proj.shape[0]
    tm = min(T, 256)
    par = pl.BlockSpec((1, HD), lambda i: (0, 0))
    blk = pl.BlockSpec((tm, SB_W), lambda i: (i, 0))

    def fn(q, k, qg, kg, dqn, dkn, dv):
        _, vjp = jax.vjp(_qk_norm_fn, q, k, qg, kg)
        dq, dk, dqg, dkg = vjp((dqn, dkn))
        return jnp.concatenate([dq.astype(bf16), dk.astype(bf16), dv.astype(bf16)], axis=1), dqg, dkg

    gsds = jax.ShapeDtypeStruct((1, HD), f32)
    return _fn_call(fn, (proj, proj, qg, kg, dqn, dkn, dv),
                    [blk, pl.BlockSpec((tm, SB_W), lambda i: (i, 1)), par, par, blk, blk, blk],
                    (jax.ShapeDtypeStruct((T, SB_IN), bf16), gsds, gsds),
                    (pl.BlockSpec((tm, 3 * SB_W), lambda i: (i, 0)), par, par), (T // tm,), name, acc={1: "all", 2: "all"})


def _split_dot(a, tri):
    hi = a.astype(bf16)
    lo = (a - hi.astype(f32)).astype(bf16)
    return jnp.dot(hi, tri, preferred_element_type=f32) + jnp.dot(lo, tri, preferred_element_type=f32)


def _sb_weights(q, kblk, run, later, mask):
    z = _dg(q, kblk, 1, 1) * SB_SCALE
    t = jnp.log(1.0 + jnp.exp(-jnp.abs(z)))
    sp = jnp.maximum(z, 0.0) + t
    log_beta = jnp.minimum(z, 0.0) - t
    if mask is not None:
        sp = jnp.where(mask, sp, 0.0)
    w = jnp.exp(log_beta - _split_dot(sp, later) - run)
    if mask is not None:
        w = jnp.where(mask, w, 0.0)
    return jnp.exp(log_beta), sp, w


def _sb_older_blocks(qb, carry, step, run_of):
    def cond(state):
        i, cr = state
        return jnp.logical_and(i < qb, jnp.min(run_of(cr)) < SB_DEAD)

    def body(state):
        i, cr = state
        return i + 1, step(qb - 1 - i, cr)

    return lax.while_loop(cond, body, (jnp.int32(0), carry))[1]


def _sb_fwd(qn, kn, proj, name):
    T = qn.shape[0]
    B = min(SB_BLK, T)
    nq = T // B

    def body(q_ref, k_ref, v_ref, o_ref, ox_ref):
        qb = pl.program_id(1)
        q = q_ref[...]
        ri, ci = _iota((B, B), 0), _iota((B, B), 1)
        later = (ri > ci).astype(bf16)

        def block(kb, carry, mask):
            acc, acc_lo, run = carry
            off = pl.multiple_of(kb * B, B)
            _, sp, w = _sb_weights(q, k_ref[pl.ds(off, B), :], run, later, mask)
            vblk = v_ref[pl.ds(off, B), :]
            w_hi = w.astype(bf16)
            acc = acc + _dg(w_hi, vblk, 1, 0)
            acc_lo = acc_lo + _dg(w - w_hi.astype(f32), vblk, 1, 0)
            return acc, acc_lo, run + jnp.sum(sp, axis=1, keepdims=True)

        zero = jnp.zeros((B, HD), f32)
        carry = block(qb, (zero, zero, jnp.zeros((B, 1), f32)), ci < ri)
        carry = _sb_older_blocks(qb, carry, lambda kb, cr: block(kb, cr, None), lambda cr: cr[2])
        o_ref[...] = carry[0].astype(o_ref.dtype)
        ox_ref[...] = carry[0] + carry[1]

    blk = pl.BlockSpec((B, HD), lambda h, i: (i, h))
    return pl.pallas_call(
        body, grid=(SB_HEADS, nq),
        in_specs=[blk, pl.BlockSpec((T, HD), lambda h, i: (0, h)), pl.BlockSpec((T, HD), lambda h, i: (0, 2 * SB_HEADS + h))],
        out_specs=(blk, blk), out_shape=(jax.ShapeDtypeStruct((T, MIX_W), bf16), jax.ShapeDtypeStruct((T, SB_W), f32)),
        name=name, compiler_params=_params(2))(qn, kn, proj)


def _sb_bwd(qn, kn, proj, o, dycat, name):
    T = qn.shape[0]
    B = min(SB_BLK, T)
    nq = T // B

    def body(q_ref, k_ref, v_ref, o_ref, do_ref, dq_ref, dk_ref, dv_ref):
        qb = pl.program_id(1)

        @pl.when(qb == 0)
        def _():
            dk_ref[...] = jnp.zeros_like(dk_ref)
            dv_ref[...] = jnp.zeros_like(dv_ref)

        q = q_ref[...]
        do = do_ref[...].astype(f32)
        do_b = do.astype(bf16)
        gtot = jnp.sum(do_b.astype(f32) * o_ref[...], axis=1, keepdims=True)
        ri, ci = _iota((B, B), 0), _iota((B, B), 1)
        later = (ri > ci).astype(bf16)
        from_here = (ri >= ci).astype(bf16)

        def block(kb, carry, mask):
            dq, run, rung = carry
            off = pl.multiple_of(kb * B, B)
            kblk = k_ref[pl.ds(off, B), :]
            sig, sp, w = _sb_weights(q, kblk, run, later, mask)
            g = w * _dg(do_b, v_ref[pl.ds(off, B), :], 1, 1)
            before = gtot - rung - _split_dot(g, from_here)
            dz = (g * (1.0 - sig) - sig * before) * SB_SCALE
            if mask is not None:
                dz = jnp.where(mask, dz, 0.0)
            dz_b = dz.astype(bf16)
            dv_ref[pl.ds(off, B), :] += _dg(w, do_b, 0, 0)
            dk_ref[pl.ds(off, B), :] += _dg(dz_b, q, 0, 0)
            dq = dq + _dg(dz_b, kblk, 1, 0)
            return dq, run + jnp.sum(sp, axis=1, keepdims=True), rung + jnp.sum(g, axis=1, keepdims=True)

        zero = jnp.zeros((B, 1), f32)
        carry = block(qb, (jnp.zeros((B, HD), f32), zero, zero), ci < ri)
        carry = _sb_older_blocks(qb, carry, lambda kb, cr: block(kb, cr, None), lambda cr: cr[1])
        dq_ref[...] = carry[0]

    blk = pl.BlockSpec((B, HD), lambda h, i: (i, h))
    full = pl.BlockSpec((T, HD), lambda h, i: (0, h))
    sds = jax.ShapeDtypeStruct((T, SB_W), f32)
    return pl.pallas_call(
        body, grid=(SB_HEADS, nq),
        in_specs=[blk, full, pl.BlockSpec((T, HD), lambda h, i: (0, 2 * SB_HEADS + h)), blk, blk],
        out_specs=(blk, full, full), out_shape=(sds, sds, sds), name=name, compiler_params=_params(2))(
            qn, kn, proj, o, dycat)


_BIG = (("mem_w_kv", (4, 128, 1024), 1), ("ssd_w_in", (2, 1024, 579), 2), ("ssd_w_out", (2, 256, 1024), 1),
        ("sb_w_in", (2, 1024, 640), 2), ("sb_w_out", (2, 256, 1024), 1), ("ffn_w_gate_up", (4, 1024, 704), 2),
        ("ffn_w_down", (4, 352, 1024), 1))
_BIG_ROWS = tuple(math.prod(s) // LANES for _, s, _ in _BIG)
_BIG_TOTAL = sum(_BIG_ROWS)


def _pack_rows(parts, total):
    rows = sum(p.shape[-2] for p in parts)
    if rows == total:
        return jnp.concatenate(list(parts), axis=-2)
    pad = jnp.zeros(parts[0].shape[:-2] + (total - rows, LANES), parts[0].dtype)
    return jnp.concatenate(list(parts) + [pad], axis=-2)


def _full_from_slots(slots, shard_shape, axis):
    n = shard_shape[0]
    s = slots.reshape((N_DEV,) + shard_shape)
    if axis == 1:
        return s.transpose(1, 0, 2, 3).reshape(n, N_DEV * shard_shape[1], shard_shape[2])
    return s.transpose(1, 2, 0, 3).reshape(n, shard_shape[1], N_DEV * shard_shape[2])


def _slots_from_full(full, shard_shape, axis):
    n = shard_shape[0]
    if axis == 1:
        s = full.reshape(n, N_DEV, shard_shape[1], shard_shape[2]).transpose(1, 0, 2, 3)
    else:
        s = full.reshape(n, shard_shape[1], N_DEV, shard_shape[2]).transpose(2, 0, 1, 3)
    return s.reshape(N_DEV, n * shard_shape[1], shard_shape[2])


def _ssd_in_cols(w):
    pad = jnp.zeros(w.shape[:-1] + (SSD_IN_PAD - SSD_IN,), w.dtype)
    return jnp.concatenate([w[..., :4096], w[..., 4120:4632], w[..., 4096:4120], pad], axis=-1)


def _ssd_in_cols_back(w):
    return jnp.concatenate([w[..., :4096], w[..., 4608:4632], w[..., 4096:4608]], axis=-1)


def _lane_rows(a):
    flat = a.reshape(-1)
    n = -(-flat.shape[0] // (8 * LANES)) * (8 * LANES)
    return jnp.pad(flat, (0, n - flat.shape[0])).reshape(-1, LANES)


def _pad128(a):
    return jnp.pad(a, ((0, 0), (0, LANES - a.shape[1])))


def kernel(x, mem, mix_norm_g, ffn_norm_g, mem_norm_g, mem_w_kv, mem_q_norm_g, mem_k_norm_g, ssd_w_in, ssd_conv_w, ssd_conv_b, ssd_dt_bias, ssd_a_log, ssd_d, ssd_norm_g, ssd_w_out, sb_w_in, sb_q_norm_g, sb_k_norm_g, sb_w_out, ffn_w_gate_up, ffn_w_down, loss_target, m_mix_norm_g, m_ffn_norm_g, m_mem_norm_g, m_mem_w_kv, m_mem_q_norm_g, m_mem_k_norm_g, m_ssd_w_in, m_ssd_conv_w, m_ssd_conv_b, m_ssd_dt_bias, m_ssd_a_log, m_ssd_d, m_ssd_norm_g, m_ssd_w_out, m_sb_w_in, m_sb_q_norm_g, m_sb_k_norm_g, m_sb_w_out, m_ffn_w_gate_up, m_ffn_w_down, v_mix_norm_g, v_ffn_norm_g, v_mem_norm_g, v_mem_w_kv, v_mem_q_norm_g, v_mem_k_norm_g, v_ssd_w_in, v_ssd_conv_w, v_ssd_conv_b, v_ssd_dt_bias, v_ssd_a_log, v_ssd_d, v_ssd_norm_g, v_ssd_w_out, v_sb_w_in, v_sb_q_norm_g, v_sb_k_norm_g, v_sb_w_out, v_ffn_w_gate_up, v_ffn_w_down):
    W = dict(mix_norm_g=mix_norm_g, ffn_norm_g=ffn_norm_g, mem_norm_g=mem_norm_g, mem_w_kv=mem_w_kv, mem_q_norm_g=mem_q_norm_g, mem_k_norm_g=mem_k_norm_g, ssd_w_in=ssd_w_in, ssd_conv_w=ssd_conv_w, ssd_conv_b=ssd_conv_b, ssd_dt_bias=ssd_dt_bias, ssd_a_log=ssd_a_log, ssd_d=ssd_d, ssd_norm_g=ssd_norm_g, ssd_w_out=ssd_w_out, sb_w_in=sb_w_in, sb_q_norm_g=sb_q_norm_g, sb_k_norm_g=sb_k_norm_g, sb_w_out=sb_w_out, ffn_w_gate_up=ffn_w_gate_up, ffn_w_down=ffn_w_down)
    M = dict(mix_norm_g=m_mix_norm_g, ffn_norm_g=m_ffn_norm_g, mem_norm_g=m_mem_norm_g, mem_w_kv=m_mem_w_kv, mem_q_norm_g=m_mem_q_norm_g, mem_k_norm_g=m_mem_k_norm_g, ssd_w_in=m_ssd_w_in, ssd_conv_w=m_ssd_conv_w, ssd_conv_b=m_ssd_conv_b, ssd_dt_bias=m_ssd_dt_bias, ssd_a_log=m_ssd_a_log, ssd_d=m_ssd_d, ssd_norm_g=m_ssd_norm_g, ssd_w_out=m_ssd_w_out, sb_w_in=m_sb_w_in, sb_q_norm_g=m_sb_q_norm_g, sb_k_norm_g=m_sb_k_norm_g, sb_w_out=m_sb_w_out, ffn_w_gate_up=m_ffn_w_gate_up, ffn_w_down=m_ffn_w_down)
    V = dict(mix_norm_g=v_mix_norm_g, ffn_norm_g=v_ffn_norm_g, mem_norm_g=v_mem_norm_g, mem_w_kv=v_mem_w_kv, mem_q_norm_g=v_mem_q_norm_g, mem_k_norm_g=v_mem_k_norm_g, ssd_w_in=v_ssd_w_in, ssd_conv_w=v_ssd_conv_w, ssd_conv_b=v_ssd_conv_b, ssd_dt_bias=v_ssd_dt_bias, ssd_a_log=v_ssd_a_log, ssd_d=v_ssd_d, ssd_norm_g=v_ssd_norm_g, ssd_w_out=v_ssd_w_out, sb_w_in=v_sb_w_in, sb_q_norm_g=v_sb_q_norm_g, sb_k_norm_g=v_sb_k_norm_g, sb_w_out=v_sb_w_out, ffn_w_gate_up=v_ffn_w_gate_up, ffn_w_down=v_ffn_w_down)
    names = list(W)
    T = x.shape[1]
    x0 = x.reshape(T, D)
    mem2 = mem.reshape(MEM_LEN, D)
    target = loss_target.reshape(T, D)
    my_dev = 4 * lax.axis_index("x") + 2 * lax.axis_index("y") + lax.axis_index("c")

    w_flat = _pack_rows([W[n].astype(bf16).reshape(-1, LANES) for n, _, _ in _BIG], _BIG_TOTAL)
    slots = _allgather_hbm(w_flat, "allgather_weights")
    full, off = {}, 0
    for (n, shp, ax), rows in zip(_BIG, _BIG_ROWS):
        full[n] = _full_from_slots(slots[:, off:off + rows], shp, ax)
        off += rows
    full["ssd_w_in"] = _ssd_in_cols(full["ssd_w_in"])
    conv_slots = _allgather_vmem(_lane_rows(ssd_conv_w), "allgather_conv_w")
    conv_w = _full_from_slots(conv_slots[:, :20], (2, 4, 320), 2)

    mem_g = mem_norm_g.reshape(1, D)

    saved = []
    xc = x0
    for i in range(DEPTH):
        j = i // 2
        ssd = i % 2 == 0
        L = f"l{i}_"
        mix_g = mix_norm_g[i:i + 1]
        h = _rmsnorm_fwd(xc, mix_g, L + "mix_norm")
        w_in = full["ssd_w_in"][j] if ssd else full["sb_w_in"][j]
        proj = _matmul(h, w_in, tm=256 if ssd else 1024, tn=w_in.shape[1] if ssd else 1024, tk=D, name=L + "in_proj")
        k_mem, v_mem = _memkv_fwd(mem2, mem_g, full["mem_w_kv"][i], mem_k_norm_g[i:i + 1], L + "mem_kv")
        q_col = 8 if ssd else 9
        st = dict(x_in=xc, h=h, proj=proj, k_mem=k_mem, v_mem=v_mem)
        if ssd:
            xbc = _conv_fwd(proj, conv_w[j], ssd_conv_b[j:j + 1], L + "conv")
            dtb, alog = _pad128(ssd_dt_bias[j:j + 1]), _pad128(ssd_a_log[j:j + 1])
            dsk = jnp.repeat(ssd_d[j], SSD_P).reshape(1, SSD_INNER)
            y, hs = _ssd_fwd(xbc, proj, dtb, alog, dsk, ssd_norm_g[j:j + 1], L + "ssd_scan")
            st.update(xbc=xbc, hs=hs, dtb=dtb, alog=alog, dsk=dsk)
            w_out = full["ssd_w_out"][j]
        else:
            qn, kn = _sb_qknorm_fwd(proj, sb_q_norm_g[j:j + 1], sb_k_norm_g[j:j + 1], L + "qk_norm")
            y, o_exact = _sb_fwd(qn, kn, proj, L + "sb_attn")
            st.update(qn=qn, kn=kn, o=o_exact)
            w_out = full["sb_w_out"][j]
        ycat = _memattn_fwd(proj, q_col, k_mem, v_mem, mem_q_norm_g[i:i + 1], y, 3, L + "mem_attn")
        x_mid = _matmul(ycat, w_out, tm=512, tn=D, tk=2048, res=xc, name=L + "out_proj")
        h2 = _rmsnorm_fwd(x_mid, ffn_norm_g[i:i + 1], L + "ffn_norm")
        gu, act = _ffn_up_act(h2, full["ffn_w_gate_up"][i], L + "ffn_up")
        xc = _matmul(act, full["ffn_w_down"][i], tm=512, tn=D, tk=FFN_H, res=x_mid, name=L + "ffn_down")
        st.update(ycat=ycat, x_mid=x_mid, h2=h2, gu=gu, act=act, w_in=w_in, w_out=w_out)
        saved.append(st)

    dx, loss_part = _loss_head(xc, target, "loss_head")
    loss = lax.psum(jnp.sum(loss_part), ("x", "y", "c"))

    G = {n: [None] * W[n].shape[0] for n in names if W[n].ndim > 1}
    d_mem_g = jnp.zeros((1, D), f32)
    for i in reversed(range(DEPTH)):
        j = i // 2
        ssd = i % 2 == 0
        L = f"l{i}_b_"
        st = saved[i]
        proj = st["proj"]
        G["ffn_w_down"][i] = _matmul(st["act"], dx, ta=True, tm=FFN_H // 2, tn=D, tk=512, name=L + "dw_down")
        dgu = _ffn_dgu(dx, full["ffn_w_down"][i], st["gu"], L + "d_gu")
        dh2 = _matmul(dgu, full["ffn_w_gate_up"][i], tb=True, tm=512, tn=D, tk=2 * FFN_H, name=L + "d_h2")
        G["ffn_w_gate_up"][i] = _matmul(st["h2"], dgu, ta=True, tm=D, tn=FFN_H // 2, tk=512, name=L + "dw_up")
        dx, G["ffn_norm_g"][i] = _rmsnorm_bwd(st["x_mid"], ffn_norm_g[i:i + 1], dh2, dx, L + "d_ffn_norm")
        dycat = _matmul(dx, st["w_out"], tb=True, tm=512, tn=2048, tk=D, name=L + "d_ycat")
        g_out = _matmul(st["ycat"], dx, ta=True, tm=D, tn=D, tk=512, name=L + "dw_out")
        q_col = 8 if ssd else 9
        if ssd:
            G["ssd_w_out"][j] = g_out
            dxs, dbm, dcm, dproj, ddt, ddtb, dalog, ddsk, dng = _ssd_bwd(
                st["xbc"], proj, st["dtb"], st["alog"], st["dsk"], ssd_norm_g[j:j + 1], st["hs"], dycat, L + "d_ssd_scan")
            G["ssd_dt_bias"][j] = jnp.sum(ddtb, axis=0)[:, :SSD_HEADS]
            G["ssd_a_log"][j] = jnp.sum(dalog, axis=0)[:, :SSD_HEADS]
            G["ssd_d"][j] = jnp.sum(ddsk.reshape(SSD_HEADS, SSD_P), axis=1).reshape(1, SSD_HEADS)
            G["ssd_norm_g"][j] = dng
            dxbc_act = jnp.concatenate([dxs, dbm, dcm], axis=1)
            dpre, G["ssd_conv_w"][j], G["ssd_conv_b"][j] = _conv_bwd_pre(proj, conv_w[j], ssd_conv_b[j:j + 1], dxbc_act, L + "d_conv_pre")
            dproj = _conv_bwd_in(dpre, conv_w[j], dproj, L + "d_conv_in")
            tm = _row_tile(T)
            dproj = _fn_call(lambda d: d[0] + d[1] + d[2] + d[3], (ddt,), [pl.BlockSpec((SSD_G, tm, LANES), lambda r: (0, r, 0))],
                             jax.ShapeDtypeStruct(dproj.shape, dproj.dtype), pl.BlockSpec((tm, LANES), lambda r: (r, 36)),
                             (T // tm,), L + "d_dt", into=dproj)
        else:
            G["sb_w_out"][j] = g_out
            dqn, dkn, dv = _sb_bwd(st["qn"], st["kn"], proj, st["o"], dycat, L + "d_sb_attn")
            dproj, G["sb_q_norm_g"][j], G["sb_k_norm_g"][j] = _sb_qknorm_bwd(
                proj, sb_q_norm_g[j:j + 1], sb_k_norm_g[j:j + 1], dqn, dkn, dv, L + "d_qk_norm")
        dproj, dk_mem, dv_mem, G["mem_q_norm_g"][i] = _memattn_bwd(
            proj, q_col, st["k_mem"], st["v_mem"], mem_q_norm_g[i:i + 1], dycat, 3, dproj, q_col, L + "d_mem_attn")
        dmg, G["mem_w_kv"][i], G["mem_k_norm_g"][i] = _memkv_bwd(mem2, mem_g, full["mem_w_kv"][i], mem_k_norm_g[i:i + 1], dk_mem, dv_mem, L + "d_mem_kv")
        d_mem_g = d_mem_g + dmg
        n_in = dproj.shape[1]
        dh = _matmul(dproj, st["w_in"], tb=True, tm=512, tn=D, tk=n_in, name=L + "d_h")
        g_in = _matmul(st["h"], dproj, ta=True, tm=256 if ssd else D, tn=n_in if ssd else 1024, tk=512, name=L + "dw_in")
        if ssd:
            G["ssd_w_in"][j] = _ssd_in_cols_back(g_in)
        else:
            G["sb_w_in"][j] = g_in
        dx, G["mix_norm_g"][i] = _rmsnorm_bwd(st["x_in"], mix_norm_g[i:i + 1], dh, dx, L + "d_mix_norm")

    grad_x = dx.reshape(x.shape)

    g_slots = [_slots_from_full(jnp.stack(G[n]), shp, ax).astype(bf16) for n, shp, ax in _BIG]
    pairs = _pair_exchange(g_slots, "exchange_grads_pair")
    core = lax.axis_index("c").astype(jnp.int32).reshape(1)
    chip_sums = [_pair_sum(g, r, core, "pair_sum_" + n) for g, r, (n, _, _) in zip(g_slots, pairs, _BIG)]
    parts = _chip_exchange(chip_sums, "exchange_grads_chip")
    out = {}
    for p, (n, shp, _) in zip(parts, _BIG):
        view = (shp[0] * shp[1], shp[2])
        res = _adamw_reduce(p, W[n].reshape(view), M[n].reshape(view), V[n].reshape(view), "adamw_" + n)
        out[n] = tuple(r.reshape(shp) for r in res)

    small = [n for n in names if n not in out and n != "ssd_conv_w"]
    G["mem_norm_g"] = d_mem_g.reshape(D)
    small_grads = [_lane_rows(G[n] if n == "mem_norm_g" else jnp.concatenate(G[n], axis=0)) for n in small]
    conv_grad = _lane_rows(jnp.stack(G["ssd_conv_w"]))
    sm_rows = [g.shape[0] for g in small_grads]
    n_small = sum(sm_rows)
    sm_total = -(-(n_small + conv_grad.shape[0]) // 8) * 8
    gathered = _allgather_vmem(_pack_rows(small_grads + [conv_grad], sm_total), "allgather_small_grads")
    whole = lambda s: pl.BlockSpec(s, lambda i: (0,) * len(s))
    g_sum = _fn_call(_sum_slots, (gathered,), [whole((N_DEV, sm_total, LANES))],
                     jax.ShapeDtypeStruct((sm_total, LANES), f32), whole((sm_total, LANES)), (1,), "sum_small_grads")
    conv_full = g_sum[n_small:n_small + 160].reshape(2, 4, SSD_CONV_DIM)
    conv_mine = lax.dynamic_slice_in_dim(conv_full, my_dev * 320, 320, axis=2)
    ad_total = n_small + 24
    pack = lambda d: _pack_rows([_lane_rows(d[n]) for n in small] + [_lane_rows(d["ssd_conv_w"])], ad_total)
    g_pack = _pack_rows([g_sum[:n_small], _lane_rows(conv_mine)], ad_total)
    blk = whole((ad_total, LANES))
    res_small = _fn_call(lambda g, w, m, v: _adamw_math(w, g, m, v), (g_pack, pack(W), pack(M), pack(V)), [blk] * 4,
                         (jax.ShapeDtypeStruct((ad_total, LANES), f32),) * 3, (blk,) * 3, (1,), "adamw_small")
    res_small = (g_pack,) + tuple(res_small)
    off = 0
    for n, rows in zip(small + ["ssd_conv_w"], sm_rows + [24]):
        size = W[n].size
        out[n] = tuple(r[off:off + rows].reshape(-1)[:size].reshape(W[n].shape) for r in res_small)
        off += rows

    return (loss, grad_x, *[out[n][0] for n in names], *[out[n][1] for n in names],
            *[out[n][2] for n in names], *[out[n][3] for n in names])
```

```python
import functools
import math

import jax
import jax.numpy as jnp
from jax import lax
from jax.experimental import pallas as pl
from jax.experimental.pallas import tpu as pltpu

f32, bf16 = jnp.float32, jnp.bfloat16
MESH = pl.DeviceIdType.MESH

N_DEV = 8
D = 1024
MIX_W = 2048
DEPTH = 4
EPS = 1e-6
MEM_LEN, MEM_HEADS, MEM_W, HD = 256, 4, 512, 128
SSD_INNER, SSD_HEADS, SSD_G, SSD_P, SSD_N, SSD_L = 1536, 24, 4, 64, 128, 128
SSD_GW = SSD_INNER // SSD_G
SSD_CONV_DIM = 2560
SSD_IN = 4632
SSD_IN_PAD = 4736
SB_W, SB_HEADS, SB_IN = 1536, 12, 5120
SB_BLK = 256
SB_SCALE = HD ** -0.5
SB_DEAD = 105.0
FFN_H = 2816
LANES = 128
VMEM_LIMIT = 56 * 1024 * 1024

ADAM_LR, ADAM_B1, ADAM_B2, ADAM_EPS, ADAM_WD, ADAM_STEP = 0.001, 0.9, 0.999, 1e-08, 0.01, 10

HIGHEST = lax.Precision.HIGHEST


def _params(n_grid):
    return pltpu.CompilerParams(dimension_semantics=("arbitrary",) * n_grid, vmem_limit_bytes=VMEM_LIMIT)


def _dg(a, b, ca, cb):
    return lax.dot_general(a.astype(bf16), b.astype(bf16), (((ca,), (cb,)), ((), ())), preferred_element_type=f32)


@jax.custom_vjp
def bdot_nn(a, b):
    return _dg(a, b, 1, 0)


def _nn_fwd(a, b):
    return _dg(a, b, 1, 0), (a, b)


def _nn_bwd(res, ct):
    a, b = res
    return _dg(ct, b, 1, 1).astype(a.dtype), _dg(a, ct, 0, 0).astype(b.dtype)


bdot_nn.defvjp(_nn_fwd, _nn_bwd)


@jax.custom_vjp
def bdot_nt(a, b):
    return _dg(a, b, 1, 1)


def _nt_fwd(a, b):
    return _dg(a, b, 1, 1), (a, b)


def _nt_bwd(res, ct):
    a, b = res
    return _dg(ct, b, 1, 0).astype(a.dtype), _dg(ct, a, 0, 0).astype(b.dtype)


bdot_nt.defvjp(_nt_fwd, _nt_bwd)


@jax.custom_vjp
def bdot_tn(a, b):
    return _dg(a, b, 0, 0)


def _tn_fwd(a, b):
    return _dg(a, b, 0, 0), (a, b)


def _tn_bwd(res, ct):
    a, b = res
    return _dg(b, ct, 1, 1).astype(a.dtype), _dg(a, ct, 1, 0).astype(b.dtype)


bdot_tn.defvjp(_tn_fwd, _tn_bwd)


def _rms(x, g):
    return x * lax.rsqrt(jnp.mean(x * x, axis=-1, keepdims=True) + EPS) * g


def _iota(shape, axis):
    return lax.broadcasted_iota(jnp.int32, shape, axis)


def _fn_call(fn, args, in_specs, out_shapes, out_specs, grid, name, acc=None, into=None):
    n_in = len(args)
    acc = acc or {}
    n_grid = len(grid)
    aliases = {}
    if into is not None:
        args, in_specs, aliases = tuple(args) + (into,), list(in_specs) + [pl.BlockSpec(memory_space=pl.ANY)], {n_in: 0}

    def body(*refs):
        ins, outs = refs[:n_in], refs[len(args):]
        res = fn(*[r[...] for r in ins])
        if not isinstance(res, (tuple, list)):
            res = (res,)
        for k, (o, r) in enumerate(zip(outs, res)):
            mode = acc.get(k)
            if mode is None:
                o[...] = r.astype(o.dtype)
                continue
            if mode == "last":
                first = pl.program_id(n_grid - 1) == 0
            else:
                first = functools.reduce(jnp.logical_and, [pl.program_id(d) == 0 for d in range(n_grid)])

            @pl.when(first)
            def _(o=o, r=r):
                o[...] = r.astype(o.dtype)

            @pl.when(jnp.logical_not(first))
            def _(o=o, r=r):
                o[...] += r.astype(o.dtype)

    return pl.pallas_call(
        body, grid=grid, in_specs=in_specs, out_specs=out_specs, out_shape=out_shapes, name=name,
        input_output_aliases=aliases, compiler_params=_params(n_grid))(*args)


def _matmul(a, b, *, ta=False, tb=False, out_dtype=f32, tm, tn, tk, res=None, name):
    M, K = (a.shape[1], a.shape[0]) if ta else a.shape
    N = b.shape[0] if tb else b.shape[1]
    tm, tn, tk = min(tm, M), min(tn, N), min(tk, K)
    assert M % tm == 0 and N % tn == 0 and K % tk == 0, (name, M, N, K, tm, tn, tk)
    nk = K // tk
    a_spec = pl.BlockSpec((tk, tm), lambda i, j, k: (k, i)) if ta else pl.BlockSpec((tm, tk), lambda i, j, k: (i, k))
    b_spec = pl.BlockSpec((tn, tk), lambda i, j, k: (j, k)) if tb else pl.BlockSpec((tk, tn), lambda i, j, k: (k, j))
    o_spec = pl.BlockSpec((tm, tn), lambda i, j, k: (i, j))
    ca, cb = (0 if ta else 1), (1 if tb else 0)

    def body(*refs):
        a_ref, b_ref = refs[:2]
        r_ref = None if res is None else refs[2]
        o_ref = refs[2 if res is None else 3]
        part = _dg(a_ref[...], b_ref[...], ca, cb)

        def finish(out):
            if r_ref is not None:
                out = out + r_ref[...].astype(f32)
            o_ref[...] = out.astype(o_ref.dtype)

        if nk == 1:
            finish(part)
            return
        acc_ref = refs[-1]
        k = pl.program_id(2)

        @pl.when(k == 0)
        def _():
            acc_ref[...] = part

        @pl.when(k > 0)
        def _():
            acc_ref[...] += part

        @pl.when(k == nk - 1)
        def _():
            finish(acc_ref[...])

    args = (a, b) if res is None else (a, b, res)
    in_specs = [a_spec, b_spec] + ([] if res is None else [o_spec])
    return pl.pallas_call(
        body, grid=(M // tm, N // tn, nk), in_specs=in_specs, out_specs=o_spec,
        out_shape=jax.ShapeDtypeStruct((M, N), out_dtype), name=name,
        scratch_shapes=[] if nk == 1 else [pltpu.VMEM((tm, tn), f32)], compiler_params=_params(3))(*args)


def _row_tile(T):
    return min(T, 512)


def _my_pos():
    return lax.axis_index("x"), lax.axis_index("y"), lax.axis_index("c")


def _allgather_hbm(xs, name):
    R, C = xs.shape

    def body(x_ref, out_ref, send_sems, recv_sems, local_sem):
        x, y, c = _my_pos()
        me, sibling = (x, y, c), (x, y, 1 - c)
        chips = [(1 - x, y), (x, 1 - y), (1 - x, 1 - y)]

        def slot(px, py, pc):
            return out_ref.at[4 * px + 2 * py + pc]

        def copy(k, block, to, src=None):
            return pltpu.make_async_remote_copy(
                src_ref=slot(*block) if src is None else src, dst_ref=slot(*block),
                send_sem=send_sems.at[k], recv_sem=recv_sems.at[k], device_id=to, device_id_type=MESH)

        mine = pltpu.make_async_copy(x_ref, slot(*me), local_sem)
        mine.start()
        first = [copy(0, me, sibling, src=x_ref)]
        first += [copy(1 + j, me, (*chip, c), src=x_ref) for j, chip in enumerate(chips)]
        for cp in first:
            cp.start()
        passed = [copy(4 + j, (*chip, c), sibling) for j, chip in enumerate(chips)]
        for j, chip in enumerate(chips):
            copy(1 + j, (*chip, c), me).wait_recv()
            passed[j].start()
        copy(0, sibling, me).wait_recv()
        for j, chip in enumerate(chips):
            copy(4 + j, (*chip, 1 - c), me).wait_recv()
        for cp in first + passed:
            cp.wait_send()
        mine.wait()

    return pl.pallas_call(
        body, out_shape=jax.ShapeDtypeStruct((N_DEV, R, C), xs.dtype),
        in_specs=[pl.BlockSpec(memory_space=pl.ANY)], out_specs=pl.BlockSpec(memory_space=pl.ANY),
        scratch_shapes=[pltpu.SemaphoreType.DMA((7,)), pltpu.SemaphoreType.DMA((7,)), pltpu.SemaphoreType.DMA],
        name=name)(xs)


def _allgather_vmem(xs, name):
    R, C = xs.shape

    def body(x_ref, out_ref, send_sems, recv_sems):
        x, y, c = _my_pos()
        me = 4 * x + 2 * y + c
        out_ref[me] = x_ref[...]
        copies = []
        for k in range(1, N_DEV):
            px = 1 - x if k & 4 else x
            py = 1 - y if k & 2 else y
            pc = 1 - c if k & 1 else c
            cp = pltpu.make_async_remote_copy(
                src_ref=x_ref, dst_ref=out_ref.at[me], send_sem=send_sems.at[k - 1], recv_sem=recv_sems.at[k - 1],
                device_id=(px, py, pc), device_id_type=MESH)
            cp.start()
            copies.append(cp)
        for cp in copies:
            cp.wait()

    return pl.pallas_call(
        body, out_shape=jax.ShapeDtypeStruct((N_DEV, R, C), xs.dtype),
        in_specs=[pl.BlockSpec(memory_space=pltpu.VMEM)], out_specs=pl.BlockSpec(memory_space=pltpu.VMEM),
        scratch_shapes=[pltpu.SemaphoreType.DMA((7,)), pltpu.SemaphoreType.DMA((7,))], name=name)(xs)


N_CHIP = 4


def _pair_exchange(gs, name):
    n = len(gs)

    def body(*refs):
        g_refs, out_refs = refs[:n], refs[n:2 * n]
        send_sems, recv_sems = refs[2 * n:]
        x, y, c = _my_pos()
        copies = []
        for a in range(n):
            for k in range(N_CHIP):
                cp = pltpu.make_async_remote_copy(
                    src_ref=g_refs[a].at[4 * (k >> 1) + 2 * (k & 1) + 1 - c], dst_ref=out_refs[a].at[k],
                    send_sem=send_sems.at[a, k], recv_sem=recv_sems.at[a, k],
                    device_id=(x, y, 1 - c), device_id_type=MESH)
                cp.start()
                copies.append(cp)
        for cp in copies:
            cp.wait()

    hbm = pl.BlockSpec(memory_space=pl.ANY)
    return pl.pallas_call(
        body, out_shape=[jax.ShapeDtypeStruct((N_CHIP,) + g.shape[1:], g.dtype) for g in gs],
        in_specs=[hbm] * n, out_specs=[hbm] * n,
        scratch_shapes=[pltpu.SemaphoreType.DMA((n, N_CHIP))] * 2, name=name)(*gs)


def _chip_exchange(cs, name):
    n = len(cs)

    def body(*refs):
        c_refs, out_refs = refs[:n], refs[n:2 * n]
        send_sems, recv_sems, local_sems = refs[2 * n:]
        x, y, c = _my_pos()
        my_chip = 2 * x + y
        copies = []
        for a in range(n):
            cp = pltpu.make_async_copy(c_refs[a].at[my_chip], out_refs[a].at[my_chip], local_sems.at[a])
            cp.start()
            copies.append(cp)
            for rel in range(1, N_CHIP):
                px = 1 - x if rel & 2 else x
                py = 1 - y if rel & 1 else y
                cp = pltpu.make_async_remote_copy(
                    src_ref=c_refs[a].at[2 * px + py], dst_ref=out_refs[a].at[my_chip],
                    send_sem=send_sems.at[a, rel - 1], recv_sem=recv_sems.at[a, rel - 1],
                    device_id=(px, py, c), device_id_type=MESH)
                cp.start()
                copies.append(cp)
        for cp in copies:
            cp.wait()

    hbm = pl.BlockSpec(memory_space=pl.ANY)
    return pl.pallas_call(
        body, out_shape=[jax.ShapeDtypeStruct(g.shape, g.dtype) for g in cs],
        in_specs=[hbm] * n, out_specs=[hbm] * n,
        scratch_shapes=[pltpu.SemaphoreType.DMA((n, N_CHIP - 1)), pltpu.SemaphoreType.DMA((n, N_CHIP - 1)),
                        pltpu.SemaphoreType.DMA((n,))], name=name)(*cs)


def _shard_row_tile(rows):
    return next(t for t in (256, 128) if rows % t == 0)


def _pair_sum(g, recv, core, name):
    _, rows, cols = g.shape
    tr = _shard_row_tile(rows)

    def body(core_ref, g0, g1, g2, g3, recv_ref, o_ref):
        for k, g_ref in enumerate((g0, g1, g2, g3)):
            o_ref[k] = (g_ref[...].astype(f32) + recv_ref[k].astype(f32)).astype(o_ref.dtype)

    mine = [pl.BlockSpec((None, tr, cols), lambda i, core, k=k: (4 * (k >> 1) + 2 * (k & 1) + core[0], i, 0))
            for k in range(N_CHIP)]
    four = pl.BlockSpec((N_CHIP, tr, cols), lambda i, core: (0, i, 0))
    return pl.pallas_call(
        body, grid_spec=pltpu.PrefetchScalarGridSpec(
            num_scalar_prefetch=1, grid=(rows // tr,), in_specs=mine + [four], out_specs=four),
        out_shape=jax.ShapeDtypeStruct((N_CHIP, rows, cols), g.dtype), name=name,
        compiler_params=_params(1))(core, g, g, g, g, recv)


def _adamw_math(w, g, m, v):
    m = ADAM_B1 * m + (1.0 - ADAM_B1) * g
    v = ADAM_B2 * v + (1.0 - ADAM_B2) * jnp.square(g)
    m_hat = m / (1.0 - ADAM_B1 ** ADAM_STEP)
    v_hat = v / (1.0 - ADAM_B2 ** ADAM_STEP)
    delta = -ADAM_LR * (m_hat / (jnp.sqrt(v_hat) + ADAM_EPS) + ADAM_WD * w)
    return delta, m, v


def _sum_slots(parts):
    g = parts[0].astype(f32)
    for i in range(1, parts.shape[0]):
        g = g + parts[i].astype(f32)
    return g


def _adamw_reduce(parts, w, m, v, name):
    rows, cols = w.shape
    tr = _shard_row_tile(rows)

    def fn(p, w, m, v):
        g = _sum_slots(p)
        return (g,) + _adamw_math(w, g, m, v)

    row = pl.BlockSpec((tr, cols), lambda i: (i, 0))
    sds = jax.ShapeDtypeStruct((rows, cols), f32)
    return _fn_call(fn, (parts, w, m, v), [pl.BlockSpec((N_CHIP, tr, cols), lambda i: (0, i, 0)), row, row, row],
                    (sds,) * 4, (row,) * 4, (rows // tr,), name)


def _rmsnorm_fwd(x, g, name):
    T = x.shape[0]
    tm = _row_tile(T)
    row = pl.BlockSpec((tm, D), lambda i: (i, 0))
    par = pl.BlockSpec((1, D), lambda i: (0, 0))
    return _fn_call(lambda x, g: _rms(x, g), (x, g), [row, par], jax.ShapeDtypeStruct((T, D), bf16), row, (T // tm,), name)


def _rmsnorm_bwd(x, g, dh, dres, name):
    T = x.shape[0]
    tm = _row_tile(T)
    row = pl.BlockSpec((tm, D), lambda i: (i, 0))
    par = pl.BlockSpec((1, D), lambda i: (0, 0))

    def fn(x, g, dh, dres):
        _, vjp = jax.vjp(_rms, x, g)
        dx, dg = vjp(dh.astype(f32))
        dx = dx + dres
        return dx, dx, dg

    return _fn_call(fn, (x, g, dh, dres), [row, par, row, row],
                    (jax.ShapeDtypeStruct((T, D), f32), jax.ShapeDtypeStruct((T, D), bf16), jax.ShapeDtypeStruct((1, D), f32)),
                    (row, row, par), (T // tm,), name, acc={2: "all"})


def _swiglu_act(g, u):
    return jax.nn.silu(g) * u


def _ffn_up_act(h2, w_gu, name):
    T = h2.shape[0]
    tm = min(T, 256)

    def body(h_ref, w_ref, gu_ref, act_ref):
        gu = _dg(h_ref[...], w_ref[...], 1, 0)
        gu_ref[...] = gu
        act_ref[...] = _swiglu_act(gu[:, :FFN_H], gu[:, FFN_H:]).astype(act_ref.dtype)

    return pl.pallas_call(
        body, grid=(T // tm,),
        in_specs=[pl.BlockSpec((tm, D), lambda i: (i, 0)), pl.BlockSpec((D, 2 * FFN_H), lambda i: (0, 0))],
        out_specs=(pl.BlockSpec((tm, 2 * FFN_H), lambda i: (i, 0)), pl.BlockSpec((tm, FFN_H), lambda i: (i, 0))),
        out_shape=(jax.ShapeDtypeStruct((T, 2 * FFN_H), f32), jax.ShapeDtypeStruct((T, FFN_H), bf16)),
        name=name, compiler_params=_params(1))(h2, w_gu)


def _ffn_dgu(dx, w_down, gu, name):
    T = dx.shape[0]
    tm = min(T, 256)

    def body(dx_ref, w_ref, gu_ref, o_ref):
        dact = _dg(dx_ref[...], w_ref[...], 1, 1)
        gu = gu_ref[...]
        _, vjp = jax.vjp(_swiglu_act, gu[:, :FFN_H], gu[:, FFN_H:])
        dg, du = vjp(dact)
        o_ref[:, :FFN_H] = dg.astype(o_ref.dtype)
        o_ref[:, FFN_H:] = du.astype(o_ref.dtype)

    return pl.pallas_call(
        body, grid=(T // tm,),
        in_specs=[pl.BlockSpec((tm, D), lambda i: (i, 0)), pl.BlockSpec((FFN_H, D), lambda i: (0, 0)),
                  pl.BlockSpec((tm, 2 * FFN_H), lambda i: (i, 0))],
        out_specs=pl.BlockSpec((tm, 2 * FFN_H), lambda i: (i, 0)),
        out_shape=jax.ShapeDtypeStruct((T, 2 * FFN_H), bf16), name=name, compiler_params=_params(1))(dx, w_down, gu)


def _loss_head(x, target, name):
    T = x.shape[0]
    tm = _row_tile(T)
    row = pl.BlockSpec((tm, D), lambda i: (i, 0))
    par = pl.BlockSpec((1, LANES), lambda i: (0, 0))

    def fn(x, t):
        e = x - t
        s = jnp.sum(e * e, axis=0, keepdims=True)
        part = s[:, 0:LANES]
        for k in range(1, D // LANES):
            part = part + s[:, k * LANES:(k + 1) * LANES]
        dx = e * (1.0 / D)
        return dx, dx, part * (0.5 / D)

    return _fn_call(fn, (x, target), [row, row],
                    (jax.ShapeDtypeStruct((T, D), f32), jax.ShapeDtypeStruct((T, D), bf16), jax.ShapeDtypeStruct((1, LANES), f32)),
                    (row, row, par), (T // tm,), name, acc={2: "all"})


def _memkv_fn(mem, mg, wkv, kg):
    mn = _rms(mem, mg)
    kv = bdot_nn(mn, wkv)
    ks = [_rms(kv[:, h * HD:(h + 1) * HD], kg) for h in range(MEM_HEADS)]
    return jnp.concatenate(ks, axis=1), kv[:, MEM_W:]


def _memkv_fwd(mem, mg, wkv, kg, name):
    whole = lambda s: pl.BlockSpec(s, lambda i: (0,) * len(s))
    sds = jax.ShapeDtypeStruct((MEM_LEN, MEM_W), f32)
    return _fn_call(lambda m, g, w, k: _memkv_fn(m, g, w.astype(f32), k), (mem, mg, wkv, kg),
                    [whole((MEM_LEN, D)), whole((1, D)), whole((D, 2 * MEM_W)), whole((1, HD))],
                    (sds, sds), (whole((MEM_LEN, MEM_W)),) * 2, (1,), name)


def _memkv_bwd(mem, mg, wkv, kg, dk, dv, name):
    whole = lambda s: pl.BlockSpec(s, lambda i: (0,) * len(s))

    def fn(m, g, w, k, dk, dv):
        _, vjp = jax.vjp(lambda g, w, k: _memkv_fn(m, g, w, k), g, w.astype(f32), k)
        return vjp((dk, dv))

    return _fn_call(fn, (mem, mg, wkv, kg, dk, dv),
                    [whole((MEM_LEN, D)), whole((1, D)), whole((D, 2 * MEM_W)), whole((1, HD)),
                     whole((MEM_LEN, MEM_W)), whole((MEM_LEN, MEM_W))],
                    (jax.ShapeDtypeStruct((1, D), f32), jax.ShapeDtypeStruct((D, 2 * MEM_W), f32),
                     jax.ShapeDtypeStruct((1, HD), f32)),
                    (whole((1, D)), whole((D, 2 * MEM_W)), whole((1, HD))), (1,), name)


def _memattn_fn(q, k, v, qg):
    qn = _rms(q, qg)
    s = bdot_nt(qn, k) * (HD ** -0.5)
    s = s - jnp.max(s, axis=-1, keepdims=True)
    p = jnp.exp(s)
    p = p / jnp.sum(p, axis=-1, keepdims=True)
    return bdot_nn(p, v)


def _per_head(fn, n_heads, head_args, *shared):
    outs = [fn(*[a[:, h * HD:(h + 1) * HD] for a in head_args], *shared) for h in range(n_heads)]
    if isinstance(outs[0], tuple):
        return tuple(jnp.concatenate(o, axis=1) for o in zip(*outs))
    return jnp.concatenate(outs, axis=1)


def _memattn_all(q, k, v, qg):
    return _per_head(_memattn_fn, MEM_HEADS, (q, k, v), qg)


def _memattn_fwd(proj, q_col, k, v, qg, into, out_col, name):
    T = proj.shape[0]
    tm = _row_tile(T)
    kv_spec = pl.BlockSpec((MEM_LEN, MEM_W), lambda i: (0, 0))
    return _fn_call(_memattn_all, (proj, k, v, qg),
                    [pl.BlockSpec((tm, MEM_W), lambda i: (i, q_col)), kv_spec, kv_spec, pl.BlockSpec((1, HD), lambda i: (0, 0))],
                    jax.ShapeDtypeStruct(into.shape, into.dtype), pl.BlockSpec((tm, MEM_W), lambda i: (i, out_col)),
                    (T // tm,), name, into=into)


def _memattn_bwd(proj, q_col, k, v, qg, dycat, do_col, into, out_col, name):
    T = proj.shape[0]
    tm = _row_tile(T)

    def fn(q, k, v, qg, do):
        _, vjp = jax.vjp(_memattn_all, q, k, v, qg)
        return vjp(do.astype(f32))

    kv_spec = pl.BlockSpec((MEM_LEN, MEM_W), lambda i: (0, 0))
    kv_sds = jax.ShapeDtypeStruct((MEM_LEN, MEM_W), f32)
    par = pl.BlockSpec((1, HD), lambda i: (0, 0))
    return _fn_call(fn, (proj, k, v, qg, dycat),
                    [pl.BlockSpec((tm, MEM_W), lambda i: (i, q_col)), kv_spec, kv_spec, par,
                     pl.BlockSpec((tm, MEM_W), lambda i: (i, do_col))],
                    (jax.ShapeDtypeStruct(into.shape, into.dtype), kv_sds, kv_sds, jax.ShapeDtypeStruct((1, HD), f32)),
                    (pl.BlockSpec((tm, MEM_W), lambda i: (i, out_col)), kv_spec, kv_spec, par),
                    (T // tm,), name, acc={1: "all", 2: "all", 3: "all"}, into=into)


def _conv_taps(xp, w, first, tm):
    out = w[0:1, :] * xp[first:first + tm, :]
    for k in range(1, 4):
        out = out + w[k:k + 1, :] * xp[first + k:first + k + tm, :]
    return out


def _conv_blocks(T):
    tm, tc = _row_tile(T), 512
    nt = T // tm
    cur = pl.BlockSpec((tm, tc), lambda j, i: (i, 3 + j))
    prev = pl.BlockSpec((8, tc), lambda j, i: (jnp.maximum(i * (tm // 8) - 1, 0), 3 + j))
    par4 = pl.BlockSpec((4, tc), lambda j, i: (0, j))
    par1 = pl.BlockSpec((1, tc), lambda j, i: (0, j))
    out = pl.BlockSpec((tm, tc), lambda j, i: (i, j))
    return tm, tc, nt, cur, prev, par4, par1, out


def _conv_fwd(proj, w, b, name):
    T = proj.shape[0]
    tm, tc, nt, cur, prev, par4, par1, out = _conv_blocks(T)

    def body(prev_ref, cur_ref, w_ref, b_ref, o_ref):
        halo = jnp.where(pl.program_id(1) == 0, 0.0, prev_ref[...])
        xp = jnp.concatenate([halo, cur_ref[...]], axis=0)
        o_ref[...] = jax.nn.silu(_conv_taps(xp, w_ref[...], 5, tm) + b_ref[...])

    return pl.pallas_call(body, grid=(SSD_CONV_DIM // tc, nt), in_specs=[prev, cur, par4, par1], out_specs=out,
                          out_shape=jax.ShapeDtypeStruct((T, SSD_CONV_DIM), f32), name=name,
                          compiler_params=_params(2))(proj, proj, w, b)


def _conv_bwd_pre(proj, w, b, dact, name):
    T = proj.shape[0]
    tm, tc, nt, cur, prev, par4, par1, out = _conv_blocks(T)

    def body(prev_ref, cur_ref, w_ref, b_ref, da_ref, dp_ref, dw_ref, db_ref):
        i = pl.program_id(1)
        halo = jnp.where(i == 0, 0.0, prev_ref[...])
        xp = jnp.concatenate([halo, cur_ref[...]], axis=0)
        pre = _conv_taps(xp, w_ref[...], 5, tm) + b_ref[...]
        sig = jax.nn.sigmoid(pre)
        dpre = da_ref[...] * (sig * (1.0 + pre * (1.0 - sig)))
        dp_ref[...] = dpre
        dw = jnp.concatenate([jnp.sum(dpre * xp[5 + k:5 + k + tm, :], axis=0, keepdims=True) for k in range(4)], axis=0)
        db = jnp.sum(dpre, axis=0, keepdims=True)

        @pl.when(i == 0)
        def _():
            dw_ref[...] = dw
            db_ref[...] = db

        @pl.when(i > 0)
        def _():
            dw_ref[...] += dw
            db_ref[...] += db

    return pl.pallas_call(
        body, grid=(SSD_CONV_DIM // tc, nt), in_specs=[prev, cur, par4, par1, out], out_specs=(out, par4, par1),
        out_shape=(jax.ShapeDtypeStruct((T, SSD_CONV_DIM), f32), jax.ShapeDtypeStruct((4, SSD_CONV_DIM), f32),
                   jax.ShapeDtypeStruct((1, SSD_CONV_DIM), f32)),
        name=name, compiler_params=_params(2))(proj, proj, w, b, dact)


def _conv_bwd_in(dpre, w, into, name):
    T = dpre.shape[0]
    tm, tc, nt, cur, _, par4, _, out = _conv_blocks(T)
    nxt = pl.BlockSpec((8, tc), lambda j, i: (jnp.minimum((i + 1) * (tm // 8), T // 8 - 1), j))

    def body(cur_ref, nxt_ref, w_ref, into_ref, o_ref):
        halo = jnp.where(pl.program_id(1) == nt - 1, 0.0, nxt_ref[...])
        xp = jnp.concatenate([cur_ref[...], halo], axis=0)
        w = w_ref[...]
        acc = w[3:4, :] * xp[0:tm, :]
        for k in range(3):
            acc = acc + w[k:k + 1, :] * xp[3 - k:3 - k + tm, :]
        o_ref[...] = acc.astype(o_ref.dtype)

    return pl.pallas_call(body, grid=(SSD_CONV_DIM // tc, nt),
                          in_specs=[out, nxt, par4, pl.BlockSpec(memory_space=pl.ANY)], out_specs=cur,
                          out_shape=jax.ShapeDtypeStruct(into.shape, into.dtype), input_output_aliases={3: 0}, name=name,
                          compiler_params=_params(2))(dpre, dpre, w, into)


def _ssd_chunk(hbase, xs, bm, cm, z, dtr, dtb, alog, dsk, ng, ht):
    L = SSD_L
    dt = jax.nn.softplus(dtr + dtb)
    da = dt * (-jnp.exp(alog))
    li, si = _iota((L, L), 0), _iota((L, L), 1)
    causal = li >= si
    cs = jnp.dot(causal.astype(f32), da, precision=HIGHEST, preferred_element_type=f32)
    cs_t = cs.T
    chan_head = _iota((1, SSD_GW), 1) // SSD_P
    heads = range(SSD_GW // SSD_P)
    lane_of = [(_iota((1, LANES), 1) == hbase + r).astype(f32) for r in heads]
    cs_cols = [jnp.sum(cs * lane_of[r], axis=1, keepdims=True) for r in heads]
    dt_cols = [jnp.sum(dt * lane_of[r], axis=1, keepdims=True) for r in heads]
    cs_e = jnp.zeros((L, SSD_GW), f32)
    dt_e = jnp.zeros((L, SSD_GW), f32)
    for r in heads:
        cs_e = jnp.where(chan_head == r, cs_cols[r], cs_e)
        dt_e = jnp.where(chan_head == r, dt_cols[r], dt_e)
    xdt = xs * dt_e
    cb = bdot_nt(cm, bm)
    y = jnp.zeros((L, SSD_GW), f32)
    for r in heads:
        cs_row = jnp.sum(cs_t * (_iota((LANES, 1), 0) == hbase + r).astype(f32), axis=0, keepdims=True)
        decay = jnp.where(causal, jnp.exp(jnp.where(causal, cs_cols[r] - cs_row, 0.0)), 0.0)
        y = y + bdot_nn(cb * decay, xdt * (chan_head == r).astype(f32))
    y = y + jnp.exp(cs_e) * bdot_nn(cm, ht)
    cs_last = jnp.sum(cs_e * (_iota((L, 1), 0) == L - 1).astype(f32), axis=0, keepdims=True)
    ht_new = ht * jnp.exp(cs_last) + bdot_tn(bm, xdt * jnp.exp(cs_last - cs_e))
    y = (y + dsk * xs) * jax.nn.silu(z)
    return _rms(y, ng), ht_new


def _ssd_specs(T, rev):
    nc = T // SSD_L
    cidx = (lambda c: nc - 1 - c) if rev else (lambda c: c)
    return nc, dict(
        xs=pl.BlockSpec((SSD_L, SSD_GW), lambda g, c: (cidx(c), g)),
        bm=pl.BlockSpec((SSD_L, SSD_N), lambda g, c: (cidx(c), 12 + g)),
        cm=pl.BlockSpec((SSD_L, SSD_N), lambda g, c: (cidx(c), 16 + g)),
        z=pl.BlockSpec((SSD_L, SSD_GW), lambda g, c: (cidx(c), g)),
        dt=pl.BlockSpec((SSD_L, LANES), lambda g, c: (cidx(c), 36)),
        p128=pl.BlockSpec((1, LANES), lambda g, c: (0, 0)),
        pgw=pl.BlockSpec((1, SSD_GW), lambda g, c: (0, g)),
        hs=pl.BlockSpec((None, None, SSD_N, SSD_GW), lambda g, c: (g, cidx(c), 0, 0)),
        grp=pl.BlockSpec((SSD_L, SSD_N), lambda g, c: (cidx(c), g)),
    )


def _ssd_fwd(xbc, proj, dtb, alog, dsk, ng, name):
    T = proj.shape[0]
    nc, s = _ssd_specs(T, False)

    def body(xs_ref, bm_ref, cm_ref, z_ref, dt_ref, dtb_ref, alog_ref, dsk_ref, ng_ref, y_ref, hs_ref, h_scr):
        @pl.when(pl.program_id(1) == 0)
        def _():
            h_scr[...] = jnp.zeros_like(h_scr)

        ht = h_scr[...]
        hs_ref[...] = ht
        y, ht_new = _ssd_chunk(pl.program_id(0) * (SSD_GW // SSD_P), xs_ref[...], bm_ref[...], cm_ref[...], z_ref[...],
                               dt_ref[...], dtb_ref[...], alog_ref[...], dsk_ref[...], ng_ref[...], ht)
        y_ref[...] = y.astype(y_ref.dtype)
        h_scr[...] = ht_new

    return pl.pallas_call(
        body, grid=(SSD_G, nc),
        in_specs=[s["xs"], s["bm"], s["cm"], s["z"], s["dt"], s["p128"], s["p128"], s["pgw"], s["pgw"]],
        out_specs=(s["xs"], s["hs"]),
        out_shape=(jax.ShapeDtypeStruct((T, MIX_W), bf16), jax.ShapeDtypeStruct((SSD_G, nc, SSD_N, SSD_GW), f32)),
        scratch_shapes=[pltpu.VMEM((SSD_N, SSD_GW), f32)], name=name, compiler_params=_params(2))(
            xbc, xbc, xbc, proj, proj, dtb, alog, dsk, ng)


def _ssd_bwd(xbc, proj, dtb, alog, dsk, ng, hs, dycat, name):
    T = proj.shape[0]
    nc, s = _ssd_specs(T, True)

    def body(xs_ref, bm_ref, cm_ref, z_ref, dt_ref, dtb_ref, alog_ref, dsk_ref, ng_ref, hs_ref, dy_ref,
             dxs_ref, dbm_ref, dcm_ref, dz_ref, ddt_ref, ddtb_ref, dalog_ref, ddsk_ref, dng_ref, dh_scr):
        c = pl.program_id(1)

        @pl.when(c == 0)
        def _():
            dh_scr[...] = jnp.zeros_like(dh_scr)

        hbase = pl.program_id(0) * (SSD_GW // SSD_P)
        _, vjp = jax.vjp(functools.partial(_ssd_chunk, hbase), xs_ref[...], bm_ref[...], cm_ref[...], z_ref[...],
                         dt_ref[...], dtb_ref[...], alog_ref[...], dsk_ref[...], ng_ref[...], hs_ref[...])
        dxs, dbm, dcm, dz, ddt, ddtb, dalog, ddsk, dng, dht = vjp((dy_ref[...].astype(f32), dh_scr[...]))
        dxs_ref[...] = dxs
        dbm_ref[...] = dbm
        dcm_ref[...] = dcm
        dz_ref[...] = dz.astype(dz_ref.dtype)
        ddt_ref[...] = ddt
        dh_scr[...] = dht

        @pl.when(c == 0)
        def _():
            ddtb_ref[...] = ddtb
            dalog_ref[...] = dalog
            ddsk_ref[...] = ddsk
            dng_ref[...] = dng

        @pl.when(c > 0)
        def _():
            ddtb_ref[...] += ddtb
            dalog_ref[...] += dalog
            ddsk_ref[...] += ddsk
            dng_ref[...] += dng

    cidx = lambda c: nc - 1 - c
    g128 = pl.BlockSpec((None, 1, LANES), lambda g, c: (g, 0, 0))
    return pl.pallas_call(
        body, grid=(SSD_G, nc),
        in_specs=[s["xs"], s["bm"], s["cm"], s["z"], s["dt"], s["p128"], s["p128"], s["pgw"], s["pgw"], s["hs"], s["xs"]],
        out_specs=(s["xs"], s["grp"], s["grp"], s["xs"],
                   pl.BlockSpec((None, SSD_L, LANES), lambda g, c: (g, cidx(c), 0)), g128, g128, s["pgw"], s["pgw"]),
        out_shape=(jax.ShapeDtypeStruct((T, SSD_INNER), f32), jax.ShapeDtypeStruct((T, SSD_G * SSD_N), f32),
                   jax.ShapeDtypeStruct((T, SSD_G * SSD_N), f32), jax.ShapeDtypeStruct((T, SSD_IN_PAD), bf16),
                   jax.ShapeDtypeStruct((SSD_G, T, LANES), f32), jax.ShapeDtypeStruct((SSD_G, 1, LANES), f32),
                   jax.ShapeDtypeStruct((SSD_G, 1, LANES), f32), jax.ShapeDtypeStruct((1, SSD_INNER), f32),
                   jax.ShapeDtypeStruct((1, SSD_INNER), f32)),
        scratch_shapes=[pltpu.VMEM((SSD_N, SSD_GW), f32)], name=name, compiler_params=_params(2))(
            xbc, xbc, xbc, proj, proj, dtb, alog, dsk, ng, hs, dycat)


def _qk_norm_fn(q, k, qg, kg):
    return _per_head(_rms, SB_HEADS, (q,), qg), _per_head(_rms, SB_HEADS, (k,), kg)


def _sb_qknorm_fwd(proj, qg, kg, name):
    T = proj.shape[0]
    tm = min(T, 256)
    par = pl.BlockSpec((1, HD), lambda i: (0, 0))
    out = pl.BlockSpec((tm, SB_W), lambda i: (i, 0))
    sds = jax.ShapeDtypeStruct((T, SB_W), bf16)
    return _fn_call(_qk_norm_fn, (proj, proj, qg, kg), [out, pl.BlockSpec((tm, SB_W), lambda i: (i, 1)), par, par],
                    (sds, sds), (out, out), (T // tm,), name)


def _sb_qknorm_bwd(proj, qg, kg, dqn, dkn, dv, name):
    T = proj.shape[0]
    tm = min(T, 256)
    par = pl.BlockSpec((1, HD), lambda i: (0, 0))
    blk = pl.BlockSpec((tm, SB_W), lambda i: (i, 0))

    def fn(q, k, qg, kg, dqn, dkn, dv):
        _, vjp = jax.vjp(_qk_norm_fn, q, k, qg, kg)
        dq, dk, dqg, dkg = vjp((dqn, dkn))
        return jnp.concatenate([dq.astype(bf16), dk.astype(bf16), dv.astype(bf16)], axis=1), dqg, dkg

    gsds = jax.ShapeDtypeStruct((1, HD), f32)
    return _fn_call(fn, (proj, proj, qg, kg, dqn, dkn, dv),
                    [blk, pl.BlockSpec((tm, SB_W), lambda i: (i, 1)), par, par, blk, blk, blk],
                    (jax.ShapeDtypeStruct((T, SB_IN), bf16), gsds, gsds),
                    (pl.BlockSpec((tm, 3 * SB_W), lambda i: (i, 0)), par, par), (T // tm,), name, acc={1: "all", 2: "all"})


def _split_dot(a, tri):
    hi = a.astype(bf16)
    lo = (a - hi.astype(f32)).astype(bf16)
    return jnp.dot(hi, tri, preferred_element_type=f32) + jnp.dot(lo, tri, preferred_element_type=f32)


def _sb_weights(q, kblk, run, later, mask):
    z = _dg(q, kblk, 1, 1) * SB_SCALE
    t = jnp.log(1.0 + jnp.exp(-jnp.abs(z)))
    sp = jnp.maximum(z, 0.0) + t
    log_beta = jnp.minimum(z, 0.0) - t
    if mask is not None:
        sp = jnp.where(mask, sp, 0.0)
    w = jnp.exp(log_beta - _split_dot(sp, later) - run)
    if mask is not None:
        w = jnp.where(mask, w, 0.0)
    return jnp.exp(log_beta), sp, w


def _sb_older_blocks(qb, carry, step, run_of):
    def cond(state):
        i, cr = state
        return jnp.logical_and(i < qb, jnp.min(run_of(cr)) < SB_DEAD)

    def body(state):
        i, cr = state
        return i + 1, step(qb - 1 - i, cr)

    return lax.while_loop(cond, body, (jnp.int32(0), carry))[1]


def _sb_fwd(qn, kn, proj, name):
    T = qn.shape[0]
    B = min(SB_BLK, T)
    nq = T // B

    def body(q_ref, k_ref, v_ref, o_ref, ox_ref):
        qb = pl.program_id(1)
        q = q_ref[...]
        ri, ci = _iota((B, B), 0), _iota((B, B), 1)
        later = (ri > ci).astype(bf16)

        def block(kb, carry, mask):
            acc, acc_lo, run = carry
            off = pl.multiple_of(kb * B, B)
            _, sp, w = _sb_weights(q, k_ref[pl.ds(off, B), :], run, later, mask)
            vblk = v_ref[pl.ds(off, B), :]
            w_hi = w.astype(bf16)
            acc = acc + _dg(w_hi, vblk, 1, 0)
            acc_lo = acc_lo + _dg(w - w_hi.astype(f32), vblk, 1, 0)
            return acc, acc_lo, run + jnp.sum(sp, axis=1, keepdims=True)

        zero = jnp.zeros((B, HD), f32)
        carry = block(qb, (zero, zero, jnp.zeros((B, 1), f32)), ci < ri)
        carry = _sb_older_blocks(qb, carry, lambda kb, cr: block(kb, cr, None), lambda cr: cr[2])
        o_ref[...] = carry[0].astype(o_ref.dtype)
        ox_ref[...] = carry[0] + carry[1]

    blk = pl.BlockSpec((B, HD), lambda h, i: (i, h))
    return pl.pallas_call(
        body, grid=(SB_HEADS, nq),
        in_specs=[blk, pl.BlockSpec((T, HD), lambda h, i: (0, h)), pl.BlockSpec((T, HD), lambda h, i: (0, 2 * SB_HEADS + h))],
        out_specs=(blk, blk), out_shape=(jax.ShapeDtypeStruct((T, MIX_W), bf16), jax.ShapeDtypeStruct((T, SB_W), f32)),
        name=name, compiler_params=_params(2))(qn, kn, proj)


def _sb_bwd(qn, kn, proj, o, dycat, name):
    T = qn.shape[0]
    B = min(SB_BLK, T)
    nq = T // B

    def body(q_ref, k_ref, v_ref, o_ref, do_ref, dq_ref, dk_ref, dv_ref):
        qb = pl.program_id(1)

        @pl.when(qb == 0)
        def _():
            dk_ref[...] = jnp.zeros_like(dk_ref)
            dv_ref[...] = jnp.zeros_like(dv_ref)

        q = q_ref[...]
        do = do_ref[...].astype(f32)
        do_b = do.astype(bf16)
        gtot = jnp.sum(do_b.astype(f32) * o_ref[...], axis=1, keepdims=True)
        ri, ci = _iota((B, B), 0), _iota((B, B), 1)
        later = (ri > ci).astype(bf16)
        from_here = (ri >= ci).astype(bf16)

        def block(kb, carry, mask):
            dq, run, rung = carry
            off = pl.multiple_of(kb * B, B)
            kblk = k_ref[pl.ds(off, B), :]
            sig, sp, w = _sb_weights(q, kblk, run, later, mask)
            g = w * _dg(do_b, v_ref[pl.ds(off, B), :], 1, 1)
            before = gtot - rung - _split_dot(g, from_here)
            dz = (g * (1.0 - sig) - sig * before) * SB_SCALE
            if mask is not None:
                dz = jnp.where(mask, dz, 0.0)
            dz_b = dz.astype(bf16)
            dv_ref[pl.ds(off, B), :] += _dg(w, do_b, 0, 0)
            dk_ref[pl.ds(off, B), :] += _dg(dz_b, q, 0, 0)
            dq = dq + _dg(dz_b, kblk, 1, 0)
            return dq, run + jnp.sum(sp, axis=1, keepdims=True), rung + jnp.sum(g, axis=1, keepdims=True)

        zero = jnp.zeros((B, 1), f32)
        carry = block(qb, (jnp.zeros((B, HD), f32), zero, zero), ci < ri)
        carry = _sb_older_blocks(qb, carry, lambda kb, cr: block(kb, cr, None), lambda cr: cr[1])
        dq_ref[...] = carry[0]

    blk = pl.BlockSpec((B, HD), lambda h, i: (i, h))
    full = pl.BlockSpec((T, HD), lambda h, i: (0, h))
    sds = jax.ShapeDtypeStruct((T, SB_W), f32)
    return pl.pallas_call(
        body, grid=(SB_HEADS, nq),
        in_specs=[blk, full, pl.BlockSpec((T, HD), lambda h, i: (0, 2 * SB_HEADS + h)), blk, blk],
        out_specs=(blk, full, full), out_shape=(sds, sds, sds), name=name, compiler_params=_params(2))(
            qn, kn, proj, o, dycat)


_BIG = (("mem_w_kv", (4, 128, 1024), 1), ("ssd_w_in", (2, 1024, 579), 2), ("ssd_w_out", (2, 256, 1024), 1),
        ("sb_w_in", (2, 1024, 640), 2), ("sb_w_out", (2, 256, 1024), 1), ("ffn_w_gate_up", (4, 1024, 704), 2),
        ("ffn_w_down", (4, 352, 1024), 1))
_BIG_ROWS = tuple(math.prod(s) // LANES for _, s, _ in _BIG)
_BIG_TOTAL = sum(_BIG_ROWS)


def _pack_rows(parts, total):
    rows = sum(p.shape[-2] for p in parts)
    if rows == total:
        return jnp.concatenate(list(parts), axis=-2)
    pad = jnp.zeros(parts[0].shape[:-2] + (total - rows, LANES), parts[0].dtype)
    return jnp.concatenate(list(parts) + [pad], axis=-2)


def _full_from_slots(slots, shard_shape, axis):
    n = shard_shape[0]
    s = slots.reshape((N_DEV,) + shard_shape)
    if axis == 1:
        return s.transpose(1, 0, 2, 3).reshape(n, N_DEV * shard_shape[1], shard_shape[2])
    return s.transpose(1, 2, 0, 3).reshape(n, shard_shape[1], N_DEV * shard_shape[2])


def _slots_from_full(full, shard_shape, axis):
    n = shard_shape[0]
    if axis == 1:
        s = full.reshape(n, N_DEV, shard_shape[1], shard_shape[2]).transpose(1, 0, 2, 3)
    else:
        s = full.reshape(n, shard_shape[1], N_DEV, shard_shape[2]).transpose(2, 0, 1, 3)
    return s.reshape(N_DEV, n * shard_shape[1], shard_shape[2])


def _ssd_in_cols(w):
    pad = jnp.zeros(w.shape[:-1] + (SSD_IN_PAD - SSD_IN,), w.dtype)
    return jnp.concatenate([w[..., :4096], w[..., 4120:4632], w[..., 4096:4120], pad], axis=-1)


def _ssd_in_cols_back(w):
    return jnp.concatenate([w[..., :4096], w[..., 4608:4632], w[..., 4096:4608]], axis=-1)


def _lane_rows(a):
    flat = a.reshape(-1)
    n = -(-flat.shape[0] // (8 * LANES)) * (8 * LANES)
    return jnp.pad(flat, (0, n - flat.shape[0])).reshape(-1, LANES)


def _pad128(a):
    return jnp.pad(a, ((0, 0), (0, LANES - a.shape[1])))


def kernel(x, mem, mix_norm_g, ffn_norm_g, mem_norm_g, mem_w_kv, mem_q_norm_g, mem_k_norm_g, ssd_w_in, ssd_conv_w, ssd_conv_b, ssd_dt_bias, ssd_a_log, ssd_d, ssd_norm_g, ssd_w_out, sb_w_in, sb_q_norm_g, sb_k_norm_g, sb_w_out, ffn_w_gate_up, ffn_w_down, loss_target, m_mix_norm_g, m_ffn_norm_g, m_mem_norm_g, m_mem_w_kv, m_mem_q_norm_g, m_mem_k_norm_g, m_ssd_w_in, m_ssd_conv_w, m_ssd_conv_b, m_ssd_dt_bias, m_ssd_a_log, m_ssd_d, m_ssd_norm_g, m_ssd_w_out, m_sb_w_in, m_sb_q_norm_g, m_sb_k_norm_g, m_sb_w_out, m_ffn_w_gate_up, m_ffn_w_down, v_mix_norm_g, v_ffn_norm_g, v_mem_norm_g, v_mem_w_kv, v_mem_q_norm_g, v_mem_k_norm_g, v_ssd_w_in, v_ssd_conv_w, v_ssd_conv_b, v_ssd_dt_bias, v_ssd_a_log, v_ssd_d, v_ssd_norm_g, v_ssd_w_out, v_sb_w_in, v_sb_q_norm_g, v_sb_k_norm_g, v_sb_w_out, v_ffn_w_gate_up, v_ffn_w_down):
    W = dict(mix_norm_g=mix_norm_g, ffn_norm_g=ffn_norm_g, mem_norm_g=mem_norm_g, mem_w_kv=mem_w_kv, mem_q_norm_g=mem_q_norm_g, mem_k_norm_g=mem_k_norm_g, ssd_w_in=ssd_w_in, ssd_conv_w=ssd_conv_w, ssd_conv_b=ssd_conv_b, ssd_dt_bias=ssd_dt_bias, ssd_a_log=ssd_a_log, ssd_d=ssd_d, ssd_norm_g=ssd_norm_g, ssd_w_out=ssd_w_out, sb_w_in=sb_w_in, sb_q_norm_g=sb_q_norm_g, sb_k_norm_g=sb_k_norm_g, sb_w_out=sb_w_out, ffn_w_gate_up=ffn_w_gate_up, ffn_w_down=ffn_w_down)
    M = dict(mix_norm_g=m_mix_norm_g, ffn_norm_g=m_ffn_norm_g, mem_norm_g=m_mem_norm_g, mem_w_kv=m_mem_w_kv, mem_q_norm_g=m_mem_q_norm_g, mem_k_norm_g=m_mem_k_norm_g, ssd_w_in=m_ssd_w_in, ssd_conv_w=m_ssd_conv_w, ssd_conv_b=m_ssd_conv_b, ssd_dt_bias=m_ssd_dt_bias, ssd_a_log=m_ssd_a_log, ssd_d=m_ssd_d, ssd_norm_g=m_ssd_norm_g, ssd_w_out=m_ssd_w_out, sb_w_in=m_sb_w_in, sb_q_norm_g=m_sb_q_norm_g, sb_k_norm_g=m_sb_k_norm_g, sb_w_out=m_sb_w_out, ffn_w_gate_up=m_ffn_w_gate_up, ffn_w_down=m_ffn_w_down)
    V = dict(mix_norm_g=v_mix_norm_g, ffn_norm_g=v_ffn_norm_g, mem_norm_g=v_mem_norm_g, mem_w_kv=v_mem_w_kv, mem_q_norm_g=v_mem_q_norm_g, mem_k_norm_g=v_mem_k_norm_g, ssd_w_in=v_ssd_w_in, ssd_conv_w=v_ssd_conv_w, ssd_conv_b=v_ssd_conv_b, ssd_dt_bias=v_ssd_dt_bias, ssd_a_log=v_ssd_a_log, ssd_d=v_ssd_d, ssd_norm_g=v_ssd_norm_g, ssd_w_out=v_ssd_w_out, sb_w_in=v_sb_w_in, sb_q_norm_g=v_sb_q_norm_g, sb_k_norm_g=v_sb_k_norm_g, sb_w_out=v_sb_w_out, ffn_w_gate_up=v_ffn_w_gate_up, ffn_w_down=v_ffn_w_down)
    names = list(W)
    T = x.shape[1]
    x0 = x.reshape(T, D)
    mem2 = mem.reshape(MEM_LEN, D)
    target = loss_target.reshape(T, D)
    my_dev = 4 * lax.axis_index("x") + 2 * lax.axis_index("y") + lax.axis_index("c")

    w_flat = _pack_rows([W[n].astype(bf16).reshape(-1, LANES) for n, _, _ in _BIG], _BIG_TOTAL)
    slots = _allgather_hbm(w_flat, "allgather_weights")
    full, off = {}, 0
    for (n, shp, ax), rows in zip(_BIG, _BIG_ROWS):
        full[n] = _full_from_slots(slots[:, off:off + rows], shp, ax)
        off += rows
    full["ssd_w_in"] = _ssd_in_cols(full["ssd_w_in"])
    conv_slots = _allgather_vmem(_lane_rows(ssd_conv_w), "allgather_conv_w")
    conv_w = _full_from_slots(conv_slots[:, :20], (2, 4, 320), 2)

    mem_g = mem_norm_g.reshape(1, D)

    saved = []
    xc = x0
    for i in range(DEPTH):
        j = i // 2
        ssd = i % 2 == 0
        L = f"l{i}_"
        mix_g = mix_norm_g[i:i + 1]
        h = _rmsnorm_fwd(xc, mix_g, L + "mix_norm")
        w_in = full["ssd_w_in"][j] if ssd else full["sb_w_in"][j]
        proj = _matmul(h, w_in, tm=256 if ssd else 1024, tn=w_in.shape[1] if ssd else 1024, tk=D, name=L + "in_proj")
        k_mem, v_mem = _memkv_fwd(mem2, mem_g, full["mem_w_kv"][i], mem_k_norm_g[i:i + 1], L + "mem_kv")
        q_col = 8 if ssd else 9
        st = dict(x_in=xc, h=h, proj=proj, k_mem=k_mem, v_mem=v_mem)
        if ssd:
            xbc = _conv_fwd(proj, conv_w[j], ssd_conv_b[j:j + 1], L + "conv")
            dtb, alog = _pad128(ssd_dt_bias[j:j + 1]), _pad128(ssd_a_log[j:j + 1])
            dsk = jnp.repeat(ssd_d[j], SSD_P).reshape(1, SSD_INNER)
            y, hs = _ssd_fwd(xbc, proj, dtb, alog, dsk, ssd_norm_g[j:j + 1], L + "ssd_scan")
            st.update(xbc=xbc, hs=hs, dtb=dtb, alog=alog, dsk=dsk)
            w_out = full["ssd_w_out"][j]
        else:
            qn, kn = _sb_qknorm_fwd(proj, sb_q_norm_g[j:j + 1], sb_k_norm_g[j:j + 1], L + "qk_norm")
            y, o_exact = _sb_fwd(qn, kn, proj, L + "sb_attn")
            st.update(qn=qn, kn=kn, o=o_exact)
            w_out = full["sb_w_out"][j]
        ycat = _memattn_fwd(proj, q_col, k_mem, v_mem, mem_q_norm_g[i:i + 1], y, 3, L + "mem_attn")
        x_mid = _matmul(ycat, w_out, tm=512, tn=D, tk=2048, res=xc, name=L + "out_proj")
        h2 = _rmsnorm_fwd(x_mid, ffn_norm_g[i:i + 1], L + "ffn_norm")
        gu, act = _ffn_up_act(h2, full["ffn_w_gate_up"][i], L + "ffn_up")
        xc = _matmul(act, full["ffn_w_down"][i], tm=512, tn=D, tk=FFN_H, res=x_mid, name=L + "ffn_down")
        st.update(ycat=ycat, x_mid=x_mid, h2=h2, gu=gu, act=act, w_in=w_in, w_out=w_out)
        saved.append(st)

    dx, dx_b, loss_part = _loss_head(xc, target, "loss_head")
    loss = lax.psum(jnp.sum(loss_part), ("x", "y", "c"))

    G = {n: [None] * W[n].shape[0] for n in names if W[n].ndim > 1}
    d_mem_g = jnp.zeros((1, D), f32)
    for i in reversed(range(DEPTH)):
        j = i // 2
        ssd = i % 2 == 0
        L = f"l{i}_b_"
        st = saved[i]
        proj = st["proj"]
        G["ffn_w_down"][i] = _matmul(st["act"], dx_b, ta=True, tm=FFN_H // 2, tn=D, tk=512, name=L + "dw_down")
        dgu = _ffn_dgu(dx_b, full["ffn_w_down"][i], st["gu"], L + "d_gu")
        dh2 = _matmul(dgu, full["ffn_w_gate_up"][i], tb=True, tm=512, tn=D, tk=2 * FFN_H, name=L + "d_h2")
        G["ffn_w_gate_up"][i] = _matmul(st["h2"], dgu, ta=True, tm=D, tn=FFN_H // 2, tk=512, name=L + "dw_up")
        dx, dx_b, G["ffn_norm_g"][i] = _rmsnorm_bwd(st["x_mid"], ffn_norm_g[i:i + 1], dh2, dx, L + "d_ffn_norm")
        dycat = _matmul(dx_b, st["w_out"], tb=True, tm=512, tn=2048, tk=D, name=L + "d_ycat")
        g_out = _matmul(st["ycat"], dx_b, ta=True, tm=D, tn=D, tk=512, name=L + "dw_out")
        q_col = 8 if ssd else 9
        if ssd:
            G["ssd_w_out"][j] = g_out
            dxs, dbm, dcm, dproj, ddt, ddtb, dalog, ddsk, dng = _ssd_bwd(
                st["xbc"], proj, st["dtb"], st["alog"], st["dsk"], ssd_norm_g[j:j + 1], st["hs"], dycat, L + "d_ssd_scan")
            G["ssd_dt_bias"][j] = jnp.sum(ddtb, axis=0)[:, :SSD_HEADS]
            G["ssd_a_log"][j] = jnp.sum(dalog, axis=0)[:, :SSD_HEADS]
            G["ssd_d"][j] = jnp.sum(ddsk.reshape(SSD_HEADS, SSD_P), axis=1).reshape(1, SSD_HEADS)
            G["ssd_norm_g"][j] = dng
            dxbc_act = jnp.concatenate([dxs, dbm, dcm], axis=1)
            dpre, G["ssd_conv_w"][j], G["ssd_conv_b"][j] = _conv_bwd_pre(proj, conv_w[j], ssd_conv_b[j:j + 1], dxbc_act, L + "d_conv_pre")
            dproj = _conv_bwd_in(dpre, conv_w[j], dproj, L + "d_conv_in")
            tm = _row_tile(T)
            dproj = _fn_call(lambda d: d[0] + d[1] + d[2] + d[3], (ddt,), [pl.BlockSpec((SSD_G, tm, LANES), lambda r: (0, r, 0))],
                             jax.ShapeDtypeStruct(dproj.shape, dproj.dtype), pl.BlockSpec((tm, LANES), lambda r: (r, 36)),
                             (T // tm,), L + "d_dt", into=dproj)
        else:
            G["sb_w_out"][j] = g_out
            dqn, dkn, dv = _sb_bwd(st["qn"], st["kn"], proj, st["o"], dycat, L + "d_sb_attn")
            dproj, G["sb_q_norm_g"][j], G["sb_k_norm_g"][j] = _sb_qknorm_bwd(
                proj, sb_q_norm_g[j:j + 1], sb_k_norm_g[j:j + 1], dqn, dkn, dv, L + "d_qk_norm")
        dproj, dk_mem, dv_mem, G["mem_q_norm_g"][i] = _memattn_bwd(
            proj, q_col, st["k_mem"], st["v_mem"], mem_q_norm_g[i:i + 1], dycat, 3, dproj, q_col, L + "d_mem_attn")
        dmg, G["mem_w_kv"][i], G["mem_k_norm_g"][i] = _memkv_bwd(mem2, mem_g, full["mem_w_kv"][i], mem_k_norm_g[i:i + 1], dk_mem, dv_mem, L + "d_mem_kv")
        d_mem_g = d_mem_g + dmg
        n_in = dproj.shape[1]
        dh = _matmul(dproj, st["w_in"], tb=True, tm=512, tn=D, tk=n_in, name=L + "d_h")
        g_in = _matmul(st["h"], dproj, ta=True, tm=256 if ssd else D, tn=n_in if ssd else 1024, tk=512, name=L + "dw_in")
        if ssd:
            G["ssd_w_in"][j] = _ssd_in_cols_back(g_in)
        else:
            G["sb_w_in"][j] = g_in
        dx, dx_b, G["mix_norm_g"][i] = _rmsnorm_bwd(st["x_in"], mix_norm_g[i:i + 1], dh, dx, L + "d_mix_norm")

    grad_x = dx.reshape(x.shape)

    g_slots = [_slots_from_full(jnp.stack(G[n]), shp, ax).astype(bf16) for n, shp, ax in _BIG]
    pairs = _pair_exchange(g_slots, "exchange_grads_pair")
    core = lax.axis_index("c").astype(jnp.int32).reshape(1)
    chip_sums = [_pair_sum(g, r, core, "pair_sum_" + n) for g, r, (n, _, _) in zip(g_slots, pairs, _BIG)]
    parts = _chip_exchange(chip_sums, "exchange_grads_chip")
    out = {}
    for p, (n, shp, _) in zip(parts, _BIG):
        view = (shp[0] * shp[1], shp[2])
        res = _adamw_reduce(p, W[n].reshape(view), M[n].reshape(view), V[n].reshape(view), "adamw_" + n)
        out[n] = tuple(r.reshape(shp) for r in res)

    small = [n for n in names if n not in out and n != "ssd_conv_w"]
    G["mem_norm_g"] = d_mem_g.reshape(D)
    small_grads = [_lane_rows(G[n] if n == "mem_norm_g" else jnp.concatenate(G[n], axis=0)) for n in small]
    conv_grad = _lane_rows(jnp.stack(G["ssd_conv_w"]))
    sm_rows = [g.shape[0] for g in small_grads]
    n_small = sum(sm_rows)
    sm_total = -(-(n_small + conv_grad.shape[0]) // 8) * 8
    gathered = _allgather_vmem(_pack_rows(small_grads + [conv_grad], sm_total), "allgather_small_grads")
    whole = lambda s: pl.BlockSpec(s, lambda i: (0,) * len(s))
    g_sum = _fn_call(_sum_slots, (gathered,), [whole((N_DEV, sm_total, LANES))],
                     jax.ShapeDtypeStruct((sm_total, LANES), f32), whole((sm_total, LANES)), (1,), "sum_small_grads")
    conv_full = g_sum[n_small:n_small + 160].reshape(2, 4, SSD_CONV_DIM)
    conv_mine = lax.dynamic_slice_in_dim(conv_full, my_dev * 320, 320, axis=2)
    ad_total = n_small + 24
    pack = lambda d: _pack_rows([_lane_rows(d[n]) for n in small] + [_lane_rows(d["ssd_conv_w"])], ad_total)
    g_pack = _pack_rows([g_sum[:n_small], _lane_rows(conv_mine)], ad_total)
    blk = whole((ad_total, LANES))
    res_small = _fn_call(lambda g, w, m, v: _adamw_math(w, g, m, v), (g_pack, pack(W), pack(M), pack(V)), [blk] * 4,
                         (jax.ShapeDtypeStruct((ad_total, LANES), f32),) * 3, (blk,) * 3, (1,), "adamw_small")
    res_small = (g_pack,) + tuple(res_small)
    off = 0
    for n, rows in zip(small + ["ssd_conv_w"], sm_rows + [24]):
        size = W[n].size
        out[n] = tuple(r[off:off + rows].reshape(-1)[:size].reshape(W[n].shape) for r in res_small)
        off += rows

    return (loss, grad_x, *[out[n][0] for n in names], *[out[n][1] for n in names],
            *[out[n][2] for n in names], *[out[n][3] for n in names])
```
